```python
import jax, jax.numpy as jnp
from jax import lax
import numpy as np

D_MODEL = 1024
BATCH = 8
SEQ = 2048
DEPTH = 2

GRID_W = 64
CTX_LEN = 256
N_MIXERS = 2
RET_HEADS = 4
RET_QK_DIM = D_MODEL // RET_HEADS
RET_V_DIM = 2 * D_MODEL // RET_HEADS
RET_CHUNK = 128
ROPE_BASE = 10000.0
NA_HEADS = 16
NA_HEAD_DIM = D_MODEL // NA_HEADS
NA_MAX_KH = 8
NA_KW = 16
FFN_DIM = 2816
CONV_W = 3
NORM_EPS = 1e-6

kernel_name = "hybrid_retention_natten_convffn_dit"


def _rmsnorm(x, w):
    xf = x.astype(jnp.float32)
    y = xf * lax.rsqrt(jnp.mean(xf * xf, axis=-1, keepdims=True) + NORM_EPS)
    return (y * w.astype(jnp.float32)).astype(x.dtype)


def _modulate(h, shift, scale):
    return h * (1 + scale) + shift


def _axial_rope(x, pos_r, pos_c):
    half = x.shape[-1] // 2
    quarter = half // 2
    inv = ROPE_BASE ** (-jnp.arange(quarter, dtype=jnp.float32) / quarter)

    def rot(xp, pos):
        ang = pos[:, None] * inv[None, :]
        cos = jnp.cos(ang)[None, :, None, :]
        sin = jnp.sin(ang)[None, :, None, :]
        x1, x2 = xp[..., :quarter], xp[..., quarter:]
        return jnp.concatenate([x1 * cos - x2 * sin, x2 * cos + x1 * sin], axis=-1)

    return jnp.concatenate([rot(x[..., :half], pos_r), rot(x[..., half:], pos_c)], axis=-1)


def _retention_scan(q, k, v, log_g, s0, strict):
    b, l, h, _ = q.shape
    dv = v.shape[-1]
    n = l // RET_CHUNK

    def blocks(a):
        return a.reshape(b, n, RET_CHUNK, h, a.shape[-1]).transpose(1, 0, 3, 2, 4)

    idx = jnp.arange(RET_CHUNK, dtype=jnp.float32)
    diff = idx[:, None] - idx[None, :]
    mask = diff > 0 if strict else diff >= 0
    dmat = jnp.where(mask, jnp.exp(log_g[:, None, None] * jnp.maximum(diff, 0.0)), 0.0)
    q_dec = jnp.exp(log_g[:, None] * (idx + 1.0))[..., None]
    k_dec = jnp.exp(log_g[:, None] * (RET_CHUNK - 1.0 - idx))[..., None]
    c_dec = jnp.exp(log_g * RET_CHUNK)[:, None, None]

    def step(state, blk):
        qb, kb, vb = blk
        inner = jnp.einsum('bhij,bhjv->bhiv', jnp.einsum('bhid,bhjd->bhij', qb, kb) * dmat, vb)
        cross = jnp.einsum('bhid,bhdv->bhiv', qb * q_dec, state)
        state = state * c_dec + jnp.einsum('bhjd,bhjv->bhdv', kb * k_dec, vb)
        return state, inner + cross

    s_final, out = lax.scan(step, s0, (blocks(q), blocks(k), blocks(v)))
    return out.transpose(1, 0, 3, 2, 4).reshape(b, l, h, dv), s_final


def _retention(h, hc, w_in, dlog_f, dlog_b, w_out, pos_r, pos_c):
    hk = RET_HEADS * RET_QK_DIM
    hv = RET_HEADS * RET_V_DIM

    def proj(z):
        b, l, _ = z.shape
        q, k, v, g = jnp.split(z @ w_in, [hk, 2 * hk, 2 * hk + hv], axis=-1)
        hd = lambda a, d: a.reshape(b, l, RET_HEADS, d).astype(jnp.float32)
        return hd(q, RET_QK_DIM), hd(k, RET_QK_DIM) * RET_QK_DIM ** -0.5, hd(v, RET_V_DIM), hd(g, RET_V_DIM)

    ql, kl, vl, gl = proj(h)
    qc, kc, vc, gc = proj(hc)
    ql = _axial_rope(ql, pos_r, pos_c)
    kl = _axial_rope(kl, pos_r, pos_c)
    log_gf = -jnp.exp(dlog_f.astype(jnp.float32))
    log_gb = -jnp.exp(dlog_b.astype(jnp.float32))
    zero = jnp.zeros((h.shape[0], RET_HEADS, RET_QK_DIM, RET_V_DIM), jnp.float32)
    flip = lambda a: jnp.flip(a, axis=1)
    oc_f, sc_f = _retention_scan(qc, kc, vc, log_gf, zero, False)
    ol_f, _ = _retention_scan(ql, kl, vl, log_gf, sc_f, False)
    oc_b, sc_b = _retention_scan(flip(qc), flip(kc), flip(vc), log_gb, zero, True)
    ol_b, _ = _retention_scan(flip(ql), flip(kl), flip(vl), log_gb, sc_b, True)

    def out(o, g, dtype):
        b, l = o.shape[:2]
        o = o * lax.rsqrt(jnp.mean(o * o, axis=-1, keepdims=True) + NORM_EPS)
        return (jax.nn.silu(g) * o).reshape(b, l, hv).astype(dtype) @ w_out

    return out(ol_f + flip(ol_b), gl, h.dtype), out(oc_f + flip(oc_b), gc, hc.dtype)


def _neighbourhood_attention(h, hc, w_qkv, rpb, w_out, rows, with_ctx_out):
    b, s, d = h.shape
    kh = min(NA_MAX_KH, rows)
    kw = NA_KW
    scale = NA_HEAD_DIM ** -0.5

    def proj(z):
        q, k, v = jnp.split(z @ w_qkv, 3, axis=-1)
        hd = lambda a: a.reshape(z.shape[0], z.shape[1], NA_HEADS, NA_HEAD_DIM)
        return hd(q) * scale, hd(k), hd(v)

    ql, kl, vl = proj(h)
    qc, kc, vc = proj(hc)
    grid = lambda a: a.reshape(b, rows, GRID_W, NA_HEADS, NA_HEAD_DIM)
    qg, kg, vg = grid(ql), grid(kl), grid(vl)
    cols = jnp.arange(GRID_W)
    cs = jnp.clip(cols - kw // 2, 0, GRID_W - kw)
    col_valid = (cols[None, :] >= cs[:, None]) & (cols[None, :] < cs[:, None] + kw)
    dc_idx = jnp.clip(cols[None, :] - cols[:, None] + kw - 1, 0, 2 * kw - 2)
    rpb_c = rpb[:, :, dc_idx]
    n_loc = kh * GRID_W

    def row_block(r):
        rs = jnp.clip(r - kh // 2, 0, rows - kh)
        q_r = lax.dynamic_index_in_dim(qg, r, axis=1, keepdims=False)
        k_b = lax.dynamic_slice_in_dim(kg, rs, kh, axis=1)
        v_b = lax.dynamic_slice_in_dim(vg, rs, kh, axis=1)
        dr_idx = rs + jnp.arange(kh) - r + kh - 1
        bias = jnp.take(rpb_c, dr_idx, axis=1).transpose(0, 2, 1, 3)
        s_loc = jnp.einsum('bqhd,bikhd->bhqik', q_r, k_b).astype(jnp.float32) + bias[None].astype(jnp.float32)
        s_loc = jnp.where(col_valid[:, None, :], s_loc, -jnp.inf)
        s_ctx = jnp.einsum('bqhd,bchd->bhqc', q_r, kc).astype(jnp.float32)
        p = jax.nn.softmax(jnp.concatenate([s_loc.reshape(b, NA_HEADS, GRID_W, n_loc), s_ctx], axis=-1), axis=-1)
        p_loc = p[..., :n_loc].reshape(b, NA_HEADS, GRID_W, kh, GRID_W).astype(vg.dtype)
        p_ctx = p[..., n_loc:].astype(vc.dtype)
        return jnp.einsum('bhqik,bikhd->bqhd', p_loc, v_b) + jnp.einsum('bhqc,bchd->bqhd', p_ctx, vc)

    o = lax.map(row_block, jnp.arange(rows))
    y = o.transpose(1, 0, 2, 3, 4).reshape(b, s, d) @ w_out
    if not with_ctx_out:
        return y, None
    sc = jnp.einsum('bqhd,bkhd->bhqk', qc, kc).astype(jnp.float32)
    pc = jax.nn.softmax(sc, axis=-1).astype(vc.dtype)
    yc = jnp.einsum('bhqk,bkhd->bqhd', pc, vc).reshape(hc.shape) @ w_out
    return y, yc


def _conv_ffn(h, w_up, conv_w, conv_b, w_down):
    u = h @ w_up
    u = lax.conv_general_dilated(u, conv_w[:, None, :], window_strides=(1,),
                                 padding=[(CONV_W // 2, CONV_W // 2)],
                                 dimension_numbers=('NWC', 'WIO', 'NWC'),
                                 feature_group_count=u.shape[-1]) + conv_b
    a, g = jnp.split(u, 2, axis=-1)
    return (jax.nn.silu(a) * g) @ w_down


def setup_inputs(seed: int = 0) -> dict:
    key = jax.random.key(seed)
    ks = jax.random.split(key, 24)
    f32 = jnp.float32
    n_ret = (DEPTH + 1) // 2
    n_na = DEPTH // 2
    kh = min(NA_MAX_KH, SEQ // GRID_W)
    nrm = lambda k, shape, fan_in: jax.random.normal(k, shape, f32) * fan_in ** -0.5
    hk = RET_HEADS * RET_QK_DIM
    hv = RET_HEADS * RET_V_DIM
    dlog0 = jnp.asarray(np.log(-np.log(1.0 - 2.0 ** (-5.0 - np.arange(RET_HEADS)))).astype(np.float32))
    return {
        "x": jax.random.normal(ks[0], (BATCH, SEQ, D_MODEL), f32),
        "c": jax.random.normal(ks[1], (BATCH, D_MODEL), f32),
        "ctx": jax.random.normal(ks[2], (BATCH, CTX_LEN, D_MODEL), f32),
        "c_ctx": jax.random.normal(ks[3], (D_MODEL,), f32),
        "ada_w": nrm(ks[4], (DEPTH, D_MODEL, 6 * D_MODEL), D_MODEL),
        "ada_b": 0.02 * jax.random.normal(ks[5], (DEPTH, 6 * D_MODEL), f32),
        "norm1_w": 1.0 + 0.02 * jax.random.normal(ks[6], (DEPTH, D_MODEL), f32),
        "norm2_w": 1.0 + 0.02 * jax.random.normal(ks[7], (DEPTH, D_MODEL), f32),
        "ret_w_in": nrm(ks[8], (n_ret, D_MODEL, 2 * hk + 2 * hv), D_MODEL),
        "ret_decay_fwd": dlog0[None] + 0.01 * jax.random.normal(ks[9], (n_ret, RET_HEADS), f32),
        "ret_decay_bwd": dlog0[None] + 0.01 * jax.random.normal(ks[10], (n_ret, RET_HEADS), f32),
        "ret_w_out": nrm(ks[11], (n_ret, hv, D_MODEL), hv),
        "na_w_qkv": nrm(ks[12], (n_na, D_MODEL, 3 * D_MODEL), D_MODEL),
        "na_rpb": 0.02 * jax.random.normal(ks[13], (n_na, NA_HEADS, 2 * kh - 1, 2 * NA_KW - 1), f32),
        "na_w_out": nrm(ks[14], (n_na, D_MODEL, D_MODEL), D_MODEL),
        "ffn_w_up": nrm(ks[15], (DEPTH, D_MODEL, 2 * FFN_DIM), D_MODEL),
        "ffn_conv_w": nrm(ks[16], (DEPTH, CONV_W, 2 * FFN_DIM), CONV_W),
        "ffn_conv_b": 0.02 * jax.random.normal(ks[17], (DEPTH, 2 * FFN_DIM), f32),
        "ffn_w_down": nrm(ks[18], (DEPTH, FFN_DIM, D_MODEL), FFN_DIM),
        "final_norm_w": 1.0 + 0.02 * jax.random.normal(ks[19], (D_MODEL,), f32),
    }


def reference(x, c, ctx, c_ctx, ada_w, ada_b, norm1_w, norm2_w, ret_w_in, ret_decay_fwd, ret_decay_bwd,
              ret_w_out, na_w_qkv, na_rpb, na_w_out, ffn_w_up, ffn_conv_w, ffn_conv_b, ffn_w_down,
              final_norm_w):
    seq = x.shape[1]
    rows = seq // GRID_W
    t = jnp.arange(seq)
    pos_r = (t // GRID_W).astype(jnp.float32)
    pos_c = (t % GRID_W).astype(jnp.float32)
    sc = jax.nn.silu(c)
    sc_ctx = jax.nn.silu(c_ctx)
    for i in range(DEPTH):
        last = i == DEPTH - 1
        j = i // N_MIXERS
        sh1, sc1, g1, sh2, sc2, g2 = jnp.split((sc @ ada_w[i] + ada_b[i])[:, None, :], 6, axis=-1)
        csh1, csc1, cg1, csh2, csc2, cg2 = jnp.split(sc_ctx @ ada_w[i] + ada_b[i], 6, axis=-1)
        h = _modulate(_rmsnorm(x, norm1_w[i]), sh1, sc1)
        hc = _modulate(_rmsnorm(ctx, norm1_w[i]), csh1, csc1)
        if i % N_MIXERS == 0:
            y, yc = _retention(h, hc, ret_w_in[j], ret_decay_fwd[j], ret_decay_bwd[j], ret_w_out[j], pos_r, pos_c)
        else:
            y, yc = _neighbourhood_attention(h, hc, na_w_qkv[j], na_rpb[j], na_w_out[j], rows, not last)
        x = x + g1 * y
        h2 = _modulate(_rmsnorm(x, norm2_w[i]), sh2, sc2)
        x = x + g2 * _conv_ffn(h2, ffn_w_up[i], ffn_conv_w[i], ffn_conv_b[i], ffn_w_down[i])
        if not last:
            ctx = ctx + cg1 * yc
            hc2 = _modulate(_rmsnorm(ctx, norm2_w[i]), csh2, csc2)
            ctx = ctx + cg2 * _conv_ffn(hc2, ffn_w_up[i], ffn_conv_w[i], ffn_conv_b[i], ffn_w_down[i])
    return _rmsnorm(x, final_norm_w)
```

```python
import functools

import jax
import jax.numpy as jnp
import numpy as np
from jax import lax
from jax.experimental import pallas as pl
from jax.experimental.pallas import tpu as pltpu

F32 = jnp.float32
BF16 = jnp.bfloat16

GRID_W = 64
RET_HEADS = 4
ROPE_BASE = 10000.0
NA_HEADS = 16
NA_MAX_KH = 8
NA_KW = 16
CONV_W = 3
NORM_EPS = 1e-6

LANES = 128
BF16_SUBLANES = 16
VMEM_LIMIT_BYTES = 56 * 1024 * 1024

ROW_TILE = 1024
RET_CHUNK = 256
NA_QROWS = 4
MASK_VALUE = -1e30


def _cparams(sem):
    return pltpu.CompilerParams(dimension_semantics=sem, vmem_limit_bytes=VMEM_LIMIT_BYTES)


def _bsel(arr):
    if arr.shape[0] == 1:
        return lambda b, *_: (0, 0, 0)
    return lambda b, *_: (b, 0, 0)


def _norm_mod(x, nw, shift, scale):
    y = x * lax.rsqrt(jnp.mean(x * x, axis=-1, keepdims=True) + NORM_EPS)
    return (y * nw) * (1.0 + scale) + shift


def _silu(x):
    return x / (1.0 + jnp.exp(-x))


def _ada_kernel(c_ref, w_ref, b_ref, o_ref):
    s = _silu(c_ref[...])
    o_ref[0] = jnp.dot(s, w_ref[0], preferred_element_type=F32) + b_ref[0]


def _ada(cc, ada_w, ada_b):
    depth, d, n = ada_w.shape
    tn = 1536
    return pl.pallas_call(
        _ada_kernel,
        grid=(depth, n // tn),
        in_specs=[
            pl.BlockSpec(cc.shape, lambda i, j: (0, 0)),
            pl.BlockSpec((1, d, tn), lambda i, j: (i, 0, j)),
            pl.BlockSpec((1, 1, tn), lambda i, j: (i, 0, j)),
        ],
        out_specs=pl.BlockSpec((1, cc.shape[0], tn), lambda i, j: (i, 0, j)),
        out_shape=jax.ShapeDtypeStruct((depth, cc.shape[0], n), F32),
        compiler_params=_cparams(("parallel", "parallel")),
        name="ada",
    )(cc, ada_w, ada_b.reshape(depth, 1, n))


def _proj_kernel(*refs, rope_tiles, scale_lo, scale_hi, scale_val):
    if rope_tiles:
        x_ref, nw_ref, sh_ref, sc_ref, w_ref, cos_ref, sin_ref, o_ref, h_ref = refs
    else:
        x_ref, nw_ref, sh_ref, sc_ref, w_ref, o_ref, h_ref = refs
    j = pl.program_id(2)

    @pl.when(j == 0)
    def _():
        h_ref[...] = _norm_mod(x_ref[0], nw_ref[...], sh_ref[0], sc_ref[0]).astype(BF16)

    acc = jnp.dot(h_ref[...], w_ref[...], preferred_element_type=F32)
    if scale_hi > scale_lo:
        acc = acc * jnp.where((j >= scale_lo) & (j < scale_hi), F32(scale_val), F32(1.0))
    if not rope_tiles:
        o_ref[0] = acc.astype(o_ref.dtype)
        return

    @pl.when(j < rope_tiles)
    def _():
        for g in range(acc.shape[1] // LANES):
            part = acc[:, g * LANES:(g + 1) * LANES]
            t = (g % 2) * LANES
            rot = part * cos_ref[:, t:t + LANES] + pltpu.roll(part, LANES // 2, 1) * sin_ref[:, t:t + LANES]
            o_ref[0, :, g * LANES:(g + 1) * LANES] = rot.astype(o_ref.dtype)

    @pl.when(j >= rope_tiles)
    def _():
        o_ref[0] = acc.astype(o_ref.dtype)


def _proj(x, nw, shift, scale, w, *, tn=512, rope=None, rope_cols=0, scale_cols=(0, 0), scale_val=1.0):
    b, l, d = x.shape
    n = w.shape[1]
    tm = min(ROW_TILE, l)
    in_specs = [
        pl.BlockSpec((1, tm, d), lambda b_, i, j: (b_, i, 0)),
        pl.BlockSpec((1, d), lambda b_, i, j: (0, 0)),
        pl.BlockSpec((1, 1, d), _bsel(shift)),
        pl.BlockSpec((1, 1, d), _bsel(scale)),
        pl.BlockSpec((d, tn), lambda b_, i, j: (0, j)),
    ]
    args = [x, nw.reshape(1, d), shift, scale, w]
    if rope is not None:
        cos, sin = rope
        in_specs += [pl.BlockSpec((tm, cos.shape[1]), lambda b_, i, j: (i, 0))] * 2
        args += [cos, sin]
    kern = functools.partial(
        _proj_kernel,
        rope_tiles=(rope_cols // tn if rope is not None else 0),
        scale_lo=scale_cols[0] // tn,
        scale_hi=scale_cols[1] // tn,
        scale_val=scale_val,
    )
    return pl.pallas_call(
        kern,
        grid=(b, l // tm, n // tn),
        in_specs=in_specs,
        out_specs=pl.BlockSpec((1, tm, tn), lambda b_, i, j: (b_, i, j)),
        out_shape=jax.ShapeDtypeStruct((b, l, n), BF16),
        scratch_shapes=[pltpu.VMEM((tm, d), BF16)],
        compiler_params=_cparams(("parallel", "parallel", "arbitrary")),
        name="proj",
    )(*args)


def _outproj_kernel(a_ref, w_ref, res_ref, gate_ref, o_ref):
    acc = jnp.dot(a_ref[0], w_ref[...], preferred_element_type=F32)
    o_ref[0] = res_ref[0] + gate_ref[0] * acc


def _outproj(a, w, res, gate, *, tn=512):
    b, l, k = a.shape
    n = w.shape[1]
    tm = min(ROW_TILE, l)
    return pl.pallas_call(
        _outproj_kernel,
        grid=(b, l // tm, n // tn),
        in_specs=[
            pl.BlockSpec((1, tm, k), lambda b_, i, j: (b_, i, 0)),
            pl.BlockSpec((k, tn), lambda b_, i, j: (0, j)),
            pl.BlockSpec((1, tm, tn), lambda b_, i, j: (b_, i, j)),
            pl.BlockSpec((1, 1, tn), (lambda b_, i, j: (0, 0, j)) if gate.shape[0] == 1 else (lambda b_, i, j: (b_, 0, j))),
        ],
        out_specs=pl.BlockSpec((1, tm, tn), lambda b_, i, j: (b_, i, j)),
        out_shape=jax.ShapeDtypeStruct((b, l, n), F32),
        compiler_params=_cparams(("parallel", "parallel", "parallel")),
        name="outproj",
    )(a, w, res, gate)


FFN_TILE = 256
FFN_HALO = BF16_SUBLANES


def _ffn_kernel(x_ref, xp_ref, xn_ref, nw_ref, sh_ref, sc_ref, g_ref, wu_ref, cw_ref, cb_ref, wd_ref, fw_ref,
                o_ref, h_ref, acc_ref, *, final_norm):
    i = pl.program_id(1)
    j = pl.program_id(2)
    tm = x_ref.shape[1]
    hl = FFN_HALO

    @pl.when(j == 0)
    def _():
        nw, sh, sc = nw_ref[...], sh_ref[0], sc_ref[0]
        keep_p = jnp.where(i > 0, F32(1.0), F32(0.0))
        keep_n = jnp.where(i < pl.num_programs(1) - 1, F32(1.0), F32(0.0))
        h_ref[0:hl] = (_norm_mod(xp_ref[0], nw, sh, sc) * keep_p).astype(BF16)
        h_ref[hl:hl + tm] = _norm_mod(x_ref[0], nw, sh, sc).astype(BF16)
        h_ref[hl + tm:hl + tm + hl] = (_norm_mod(xn_ref[0], nw, sh, sc) * keep_n).astype(BF16)

    u = jnp.dot(h_ref[...], wu_ref[...], preferred_element_type=F32)
    cw = cw_ref[0]
    uc = (u[hl - 1:hl - 1 + tm] * cw[0:1] + u[hl:hl + tm] * cw[1:2] + u[hl + 1:hl + 1 + tm] * cw[2:3]) + cb_ref[0]
    act = (_silu(uc[:, :FFN_TILE]) * uc[:, FFN_TILE:]).astype(BF16)
    part = jnp.dot(act, wd_ref[...], preferred_element_type=F32)

    @pl.when(j == 0)
    def _():
        acc_ref[...] = part

    @pl.when(j > 0)
    def _():
        acc_ref[...] += part

    @pl.when(j == pl.num_programs(2) - 1)
    def _():
        y = x_ref[0] + g_ref[0] * acc_ref[...]
        if final_norm:
            y = (y * lax.rsqrt(jnp.mean(y * y, axis=-1, keepdims=True) + NORM_EPS)) * fw_ref[...]
        o_ref[0] = y


def _ffn(x, nw, shift, scale, gate, wu, cw, cb, wd, final_w, *, final_norm):
    b, l, d = x.shape
    nt = wd.shape[0] // FFN_TILE
    tm = min(ROW_TILE, l)
    hl = FFN_HALO
    per = tm // hl
    nblk = l // hl
    vec = lambda arr: pl.BlockSpec((1, 1, d), _bsel(arr))
    return pl.pallas_call(
        functools.partial(_ffn_kernel, final_norm=final_norm),
        grid=(b, l // tm, nt),
        in_specs=[
            pl.BlockSpec((1, tm, d), lambda b_, i, j: (b_, i, 0)),
            pl.BlockSpec((1, hl, d), lambda b_, i, j: (b_, jnp.maximum(i * per - 1, 0), 0)),
            pl.BlockSpec((1, hl, d), lambda b_, i, j: (b_, jnp.minimum((i + 1) * per, nblk - 1), 0)),
            pl.BlockSpec((1, d), lambda b_, i, j: (0, 0)),
            vec(shift), vec(scale), vec(gate),
            pl.BlockSpec((d, 2 * FFN_TILE), lambda b_, i, j: (0, j)),
            pl.BlockSpec((1, CONV_W, 2 * FFN_TILE), lambda b_, i, j: (j, 0, 0)),
            pl.BlockSpec((1, 1, 2 * FFN_TILE), lambda b_, i, j: (j, 0, 0)),
            pl.BlockSpec((FFN_TILE, d), lambda b_, i, j: (j, 0)),
            pl.BlockSpec((1, d), lambda b_, i, j: (0, 0)),
        ],
        out_specs=pl.BlockSpec((1, tm, d), lambda b_, i, j: (b_, i, 0)),
        out_shape=jax.ShapeDtypeStruct((b, l, d), F32),
        scratch_shapes=[pltpu.VMEM((tm + 2 * hl, d), BF16), pltpu.VMEM((tm, d), F32)],
        compiler_params=_cparams(("parallel", "parallel", "arbitrary")),
        name="ffn",
    )(x, x, x, nw.reshape(1, d), shift, scale, gate, wu, cw, cb, wd, final_w.reshape(1, d))


def _ffn_weights(w_up, conv_w, conv_b, w_down):
    d, f2 = w_up.shape
    f = f2 // 2
    nt = f // FFN_TILE
    wu = w_up.reshape(d, 2, nt, FFN_TILE).transpose(0, 2, 1, 3).reshape(d, f2).astype(BF16)
    cw = conv_w.reshape(CONV_W, 2, nt, FFN_TILE).transpose(2, 0, 1, 3).reshape(nt, CONV_W, 2 * FFN_TILE)
    cb = conv_b.reshape(2, nt, FFN_TILE).transpose(1, 0, 2).reshape(nt, 1, 2 * FFN_TILE)
    return wu, cw, cb, w_down.astype(BF16)


def _ret_kernel(lg_ref, ql_ref, kl_ref, vl_ref, gl_ref, qc_ref, kc_ref, vc_ref, gc_ref, ol_ref, oc_ref,
                s_ref, sb_ref):
    c = RET_CHUNK
    h = pl.program_id(1)
    lgf = lg_ref[0, h]
    lgb = lg_ref[1, h]
    n_ctx = qc_ref.shape[1] // c
    n_lat = ql_ref.shape[1] // c
    dv = vl_ref.shape[2]

    pos = lax.broadcasted_iota(jnp.int32, (c, 1), 0).astype(F32)
    kdec_f = jnp.exp(lgf * (c - 1.0 - pos))
    kdec_b = jnp.exp(lgb * pos)
    qdec_f = jnp.exp(lgf * (pos + 1.0))
    qdec_b = jnp.exp(lgb * (c - pos))
    one = jnp.ones((1, 1), F32)
    cdec_f = jnp.exp(lgf * c * one)
    cdec_b = jnp.exp(lgb * c * one)
    diff = (lax.broadcasted_iota(jnp.int32, (c, c), 0) - lax.broadcasted_iota(jnp.int32, (c, c), 1)).astype(F32)
    dmat = jnp.exp(jnp.where(diff >= 0, lgf, lgb) * jnp.abs(diff))

    def kv(k, v, kdec):
        kd = (k.astype(F32) * kdec).astype(BF16)
        return lax.dot_general(kd, v, (((0,), (0,)), ((), ())), preferred_element_type=F32)

    sb_ref[n_ctx - 1] = jnp.zeros((sb_ref.shape[1], dv), BF16)
    s_ref[...] = jnp.zeros(s_ref.shape, F32)
    for t in range(n_ctx - 1, -1, -1):
        sl = pl.ds(t * c, c)
        s_ref[...] = s_ref[...] * cdec_b + kv(kc_ref[0, sl, :], vc_ref[0, sl, :], kdec_b)
        if t > 0:
            sb_ref[t - 1] = s_ref[...].astype(BF16)

    def bwd_body(t, carry):
        ci = n_lat - 1 - t
        sb_ref[n_ctx + ci] = s_ref[...].astype(BF16)
        sl = pl.ds(pl.multiple_of(ci * c, c), c)
        s_ref[...] = s_ref[...] * cdec_b + kv(kl_ref[0, sl, :], vl_ref[0, sl, :], kdec_b)
        return carry

    lax.fori_loop(0, n_lat, bwd_body, 0)

    def chunk_out(q, k, v, g, sb):
        qf = q.astype(F32)
        s = lax.dot_general(q, k, (((1,), (1,)), ((), ())), preferred_element_type=F32) * dmat
        o = jnp.dot(s.astype(BF16), v, preferred_element_type=F32)
        o += jnp.dot((qf * qdec_f).astype(BF16), s_ref[...].astype(BF16), preferred_element_type=F32)
        o += jnp.dot((qf * qdec_b).astype(BF16), sb, preferred_element_type=F32)
        o = o * lax.rsqrt(jnp.mean(o * o, axis=-1, keepdims=True) + NORM_EPS)
        return (_silu(g.astype(F32)) * o).astype(BF16)

    s_ref[...] = jnp.zeros(s_ref.shape, F32)
    for t in range(n_ctx):
        sl = pl.ds(t * c, c)
        k, v = kc_ref[0, sl, :], vc_ref[0, sl, :]
        oc_ref[0, sl, :] = chunk_out(qc_ref[0, sl, :], k, v, gc_ref[0, sl, :], sb_ref[t])
        s_ref[...] = s_ref[...] * cdec_f + kv(k, v, kdec_f)

    def fwd_body(ci, carry):
        sl = pl.ds(pl.multiple_of(ci * c, c), c)
        k, v = kl_ref[0, sl, :], vl_ref[0, sl, :]
        ol_ref[0, sl, :] = chunk_out(ql_ref[0, sl, :], k, v, gl_ref[0, sl, :], sb_ref[n_ctx + ci])
        s_ref[...] = s_ref[...] * cdec_f + kv(k, v, kdec_f)
        return carry

    lax.fori_loop(0, n_lat, fwd_body, 0)


def _retention(lg, pl_, pc):
    b, l, n = pl_.shape
    lc = pc.shape[1]
    hh = RET_HEADS
    dk = n // (6 * hh)
    dv = 2 * dk
    c = RET_CHUNK

    def specs(length):
        return [
            pl.BlockSpec((1, length, dk), lambda b_, h: (b_, 0, h)),
            pl.BlockSpec((1, length, dk), lambda b_, h: (b_, 0, hh + h)),
            pl.BlockSpec((1, length, dv), lambda b_, h: (b_, 0, hh + h)),
            pl.BlockSpec((1, length, dv), lambda b_, h: (b_, 0, 2 * hh + h)),
        ]

    return pl.pallas_call(
        _ret_kernel,
        grid=(b, hh),
        in_specs=[pl.BlockSpec(memory_space=pltpu.SMEM)] + specs(l) + specs(lc),
        out_specs=[
            pl.BlockSpec((1, l, dv), lambda b_, h: (b_, 0, h)),
            pl.BlockSpec((1, lc, dv), lambda b_, h: (b_, 0, h)),
        ],
        out_shape=[jax.ShapeDtypeStruct((b, l, hh * dv), BF16), jax.ShapeDtypeStruct((b, lc, hh * dv), BF16)],
        scratch_shapes=[pltpu.VMEM((dk, dv), F32), pltpu.VMEM((lc // c + l // c, dk, dv), BF16)],
        compiler_params=_cparams(("parallel", "arbitrary")),
        name="retention",
    )(lg, pl_, pl_, pl_, pl_, pc, pc, pc, pc)


def _rope_tables(seq, dk):
    quarter = dk // 4
    t = jnp.arange(seq)
    pos_r = (t // GRID_W).astype(F32)
    pos_c = (t % GRID_W).astype(F32)
    inv = ROPE_BASE ** (-jnp.arange(quarter, dtype=F32) / quarter)
    ang_r = pos_r[:, None] * inv[None, :]
    ang_c = pos_c[:, None] * inv[None, :]
    cos = jnp.concatenate([jnp.cos(ang_r)] * 2 + [jnp.cos(ang_c)] * 2, axis=-1)
    sin = jnp.concatenate([-jnp.sin(ang_r), jnp.sin(ang_r), -jnp.sin(ang_c), jnp.sin(ang_c)], axis=-1)
    return cos, sin


def _na_window(rows):
    kh = min(NA_MAX_KH, rows)
    return kh, NA_QROWS + kh - 1


def _na_kernel(m_ref, q_ref, k_ref, v_ref, kc_ref, vc_ref, o_ref, bias_ref, *, rows):
    b = pl.program_id(1)
    rb = pl.program_id(2)
    nrb = pl.num_programs(2)
    kh, wrows = _na_window(rows)
    qr_n = NA_QROWS
    w = GRID_W
    hd = q_ref.shape[2] // 2

    @pl.when((b == 0) & (rb == 0))
    def _():
        for variant in range(3):
            r0 = (0, qr_n, rows - qr_n)[variant]
            ws = min(max(r0 - kh // 2, 0), rows - wrows)
            for qr in range(qr_n):
                r = r0 + qr
                rs = min(max(r - kh // 2, 0), rows - kh)
                for krr in range(wrows):
                    kr = ws + krr
                    for e in range(2):
                        dst = (e, variant, slice(qr * w, (qr + 1) * w), slice(krr * w, (krr + 1) * w))
                        if rs <= kr < rs + kh:
                            bias_ref[dst] = m_ref[e, kr - r + kh - 1]
                        else:
                            bias_ref[dst] = jnp.full((w, w), MASK_VALUE, F32)

    variant = jnp.where(rb == 0, 0, jnp.where(rb == nrb - 1, 2, 1))
    ws = jnp.clip(rb * qr_n - kh // 2, 0, rows - wrows)
    win = pl.ds(pl.multiple_of(ws * w, w), wrows * w)
    kwin = k_ref[0, win, :]
    vwin = v_ref[0, win, :]
    kc = kc_ref[0]
    vc = vc_ref[0]
    q = q_ref[0]
    lane = lax.broadcasted_iota(jnp.int32, q.shape, 1)
    outs = []
    for e in range(2):
        qe = jnp.where((lane >= e * hd) & (lane < (e + 1) * hd), q, jnp.zeros_like(q))
        s_loc = lax.dot_general(qe, kwin, (((1,), (1,)), ((), ())), preferred_element_type=F32)
        s_loc = s_loc + bias_ref[e, variant]
        s_ctx = lax.dot_general(qe, kc, (((1,), (1,)), ((), ())), preferred_element_type=F32)
        m = jnp.maximum(jnp.max(s_loc, axis=-1, keepdims=True), jnp.max(s_ctx, axis=-1, keepdims=True))
        p_loc = jnp.exp(s_loc - m)
        p_ctx = jnp.exp(s_ctx - m)
        den = jnp.sum(p_loc, axis=-1, keepdims=True) + jnp.sum(p_ctx, axis=-1, keepdims=True)
        o = jnp.dot(p_loc.astype(BF16), vwin, preferred_element_type=F32)
        o += jnp.dot(p_ctx.astype(BF16), vc, preferred_element_type=F32)
        outs.append(o / den)
    o_ref[0] = jnp.where(lane < hd, outs[0], outs[1]).astype(o_ref.dtype)


def _na_bias_table(rpb):
    kw = NA_KW
    cols = jnp.arange(GRID_W)
    cs = jnp.clip(cols - kw // 2, 0, GRID_W - kw)
    col_valid = (cols[None, :] >= cs[:, None]) & (cols[None, :] < cs[:, None] + kw)
    dc_idx = jnp.clip(cols[None, :] - cols[:, None] + kw - 1, 0, 2 * kw - 2)
    return jnp.where(col_valid[None, None], rpb[:, :, dc_idx], MASK_VALUE)


def _na_attention(qkv, kvc, rpb):
    b, s, d3 = qkv.shape
    d = d3 // 3
    lc = kvc.shape[1]
    rows = s // GRID_W
    pairs = d // LANES
    _, wrows = _na_window(rows)
    qrows = NA_QROWS * GRID_W
    mtab = _na_bias_table(rpb)
    return pl.pallas_call(
        functools.partial(_na_kernel, rows=rows),
        grid=(pairs, b, rows // NA_QROWS),
        in_specs=[
            pl.BlockSpec((2,) + mtab.shape[1:], lambda p, b_, r: (p, 0, 0, 0)),
            pl.BlockSpec((1, qrows, LANES), lambda p, b_, r: (b_, r, p)),
            pl.BlockSpec((1, s, LANES), lambda p, b_, r: (b_, 0, pairs + p)),
            pl.BlockSpec((1, s, LANES), lambda p, b_, r: (b_, 0, 2 * pairs + p)),
            pl.BlockSpec((1, lc, LANES), lambda p, b_, r: (b_, 0, p)),
            pl.BlockSpec((1, lc, LANES), lambda p, b_, r: (b_, 0, pairs + p)),
        ],
        out_specs=pl.BlockSpec((1, qrows, LANES), lambda p, b_, r: (b_, r, p)),
        out_shape=jax.ShapeDtypeStruct((b, s, d), BF16),
        scratch_shapes=[pltpu.VMEM((2, 3, qrows, wrows * GRID_W), F32)],
        compiler_params=_cparams(("arbitrary", "arbitrary", "arbitrary")),
        name="na_attention",
    )(mtab, qkv, qkv, qkv, kvc, kvc)


def kernel(x, c, ctx, c_ctx, ada_w, ada_b, norm1_w, norm2_w, ret_w_in, ret_decay_fwd, ret_decay_bwd, ret_w_out,
           na_w_qkv, na_rpb, na_w_out, ffn_w_up, ffn_conv_w, ffn_conv_b, ffn_w_down, final_norm_w):
    b, seq, d = x.shape
    depth = ada_w.shape[0]
    n_mixers = 2

    pad = BF16_SUBLANES - (b + 1) % BF16_SUBLANES
    cc = jnp.concatenate([c, c_ctx[None, :], jnp.zeros((pad, d), F32)], axis=0)
    mods = _ada(cc, ada_w, ada_b).reshape(depth, cc.shape[0], 6, d)

    def mod(i, k):
        return mods[i, :b, k][:, None, :], mods[i, b:b + 1, k][:, None, :]

    for i in range(depth):
        last = i == depth - 1
        j = i // n_mixers
        (sh1, csh1), (sc1, csc1), (g1, cg1) = mod(i, 0), mod(i, 1), mod(i, 2)
        (sh2, csh2), (sc2, csc2), (g2, cg2) = mod(i, 3), mod(i, 4), mod(i, 5)
        if i % n_mixers == 0:
            hk = ret_w_in.shape[2] // 6
            dk = hk // RET_HEADS
            w_in = ret_w_in[j].astype(BF16)
            kw = dict(scale_cols=(hk, 2 * hk), scale_val=dk ** -0.5)
            pl_ = _proj(x, norm1_w[i], sh1, sc1, w_in, rope=_rope_tables(seq, dk), rope_cols=2 * hk, **kw)
            pc = _proj(ctx, norm1_w[i], csh1, csc1, w_in, **kw)
            lg = jnp.stack([-jnp.exp(ret_decay_fwd[j].astype(F32)), -jnp.exp(ret_decay_bwd[j].astype(F32))])
            y, yc = _retention(lg, pl_, pc)
            w_out = ret_w_out[j].astype(BF16)
        else:
            w_qkv = na_w_qkv[j].astype(BF16)
            qkv = _proj(x, norm1_w[i], sh1, sc1, w_qkv, scale_cols=(0, d), scale_val=(d // NA_HEADS) ** -0.5)
            kvc = _proj(ctx, norm1_w[i], csh1, csc1, w_qkv[:, d:])
            if not last:
                raise NotImplementedError("context output of the attention mixer is only needed for depth > 2")
            y, yc = _na_attention(qkv, kvc, na_rpb[j]), None
            w_out = na_w_out[j].astype(BF16)
        ffn_w = _ffn_weights(ffn_w_up[i], ffn_conv_w[i], ffn_conv_b[i], ffn_w_down[i])
        x = _outproj(y, w_out, x, g1)
        x = _ffn(x, norm2_w[i], sh2, sc2, g2, *ffn_w, final_norm_w, final_norm=last)
        if not last:
            ctx = _outproj(yc, w_out, ctx, cg1)
            ctx = _ffn(ctx, norm2_w[i], csh2, csc2, cg2, *ffn_w, final_norm_w, final_norm=False)
    return x
```

```python
import functools

import jax
import jax.numpy as jnp
import numpy as np
from jax import lax
from jax.experimental import pallas as pl
from jax.experimental.pallas import tpu as pltpu

F32 = jnp.float32
BF16 = jnp.bfloat16

GRID_W = 64
RET_HEADS = 4
ROPE_BASE = 10000.0
NA_HEADS = 16
NA_MAX_KH = 8
NA_KW = 16
CONV_W = 3
NORM_EPS = 1e-6

LANES = 128
BF16_SUBLANES = 16
VMEM_LIMIT_BYTES = 56 * 1024 * 1024

ROW_TILE = 1024
RET_CHUNK = 256
NA_QROWS = 4
MASK_VALUE = -1e30


def _cparams(sem):
    return pltpu.CompilerParams(dimension_semantics=sem, vmem_limit_bytes=VMEM_LIMIT_BYTES)


def _bsel(arr):
    if arr.shape[0] == 1:
        return lambda b, *_: (0, 0, 0)
    return lambda b, *_: (b, 0, 0)


def _norm_mod(x, nw, shift, scale):
    y = x * lax.rsqrt(jnp.mean(x * x, axis=-1, keepdims=True) + NORM_EPS)
    return (y * nw) * (1.0 + scale) + shift


def _silu(x):
    return x / (1.0 + jnp.exp(-x))


def _ada_kernel(c_ref, w_ref, b_ref, o_ref):
    s = _silu(c_ref[...])
    o_ref[0] = jnp.dot(s, w_ref[0], preferred_element_type=F32) + b_ref[0]


def _ada(cc, ada_w, ada_b):
    depth, d, n = ada_w.shape
    tn = 1536
    return pl.pallas_call(
        _ada_kernel,
        grid=(depth, n // tn),
        in_specs=[
            pl.BlockSpec(cc.shape, lambda i, j: (0, 0)),
            pl.BlockSpec((1, d, tn), lambda i, j: (i, 0, j)),
            pl.BlockSpec((1, 1, tn), lambda i, j: (i, 0, j)),
        ],
        out_specs=pl.BlockSpec((1, cc.shape[0], tn), lambda i, j: (i, 0, j)),
        out_shape=jax.ShapeDtypeStruct((depth, cc.shape[0], n), F32),
        compiler_params=_cparams(("parallel", "parallel")),
        name="ada",
    )(cc, ada_w, ada_b.reshape(depth, 1, n))


def _proj_kernel(*refs, rope_tiles, scale_lo, scale_hi, scale_val):
    if rope_tiles:
        x_ref, nw_ref, sh_ref, sc_ref, w_ref, cos_ref, sin_ref, o_ref, h_ref = refs
    else:
        x_ref, nw_ref, sh_ref, sc_ref, w_ref, o_ref, h_ref = refs
    j = pl.program_id(2)

    @pl.when(j == 0)
    def _():
        h_ref[...] = _norm_mod(x_ref[0], nw_ref[...], sh_ref[0], sc_ref[0]).astype(BF16)

    acc = jnp.dot(h_ref[...], w_ref[...], preferred_element_type=F32)
    if scale_hi > scale_lo:
        acc = acc * jnp.where((j >= scale_lo) & (j < scale_hi), F32(scale_val), F32(1.0))
    if not rope_tiles:
        o_ref[0] = acc.astype(o_ref.dtype)
        return

    @pl.when(j < rope_tiles)
    def _():
        for g in range(acc.shape[1] // LANES):
            part = acc[:, g * LANES:(g + 1) * LANES]
            t = (g % 2) * LANES
            rot = part * cos_ref[:, t:t + LANES] + pltpu.roll(part, LANES // 2, 1) * sin_ref[:, t:t + LANES]
            o_ref[0, :, g * LANES:(g + 1) * LANES] = rot.astype(o_ref.dtype)

    @pl.when(j >= rope_tiles)
    def _():
        o_ref[0] = acc.astype(o_ref.dtype)


def _proj(x, nw, shift, scale, w, *, tn=512, rope=None, rope_cols=0, scale_cols=(0, 0), scale_val=1.0):
    b, l, d = x.shape
    n = w.shape[1]
    tm = min(ROW_TILE, l)
    in_specs = [
        pl.BlockSpec((1, tm, d), lambda b_, i, j: (b_, i, 0)),
        pl.BlockSpec((1, d), lambda b_, i, j: (0, 0)),
        pl.BlockSpec((1, 1, d), _bsel(shift)),
        pl.BlockSpec((1, 1, d), _bsel(scale)),
        pl.BlockSpec((d, tn), lambda b_, i, j: (0, j)),
    ]
    args = [x, nw.reshape(1, d), shift, scale, w]
    if rope is not None:
        cos, sin = rope
        in_specs += [pl.BlockSpec((tm, cos.shape[1]), lambda b_, i, j: (i, 0))] * 2
        args += [cos, sin]
    kern = functools.partial(
        _proj_kernel,
        rope_tiles=(rope_cols // tn if rope is not None else 0),
        scale_lo=scale_cols[0] // tn,
        scale_hi=scale_cols[1] // tn,
        scale_val=scale_val,
    )
    return pl.pallas_call(
        kern,
        grid=(b, l // tm, n // tn),
        in_specs=in_specs,
        out_specs=pl.BlockSpec((1, tm, tn), lambda b_, i, j: (b_, i, j)),
        out_shape=jax.ShapeDtypeStruct((b, l, n), BF16),
        scratch_shapes=[pltpu.VMEM((tm, d), BF16)],
        compiler_params=_cparams(("parallel", "parallel", "arbitrary")),
        name="proj",
    )(*args)


def _outproj_kernel(a_ref, w_ref, res_ref, gate_ref, o_ref):
    acc = jnp.dot(a_ref[0], w_ref[...], preferred_element_type=F32)
    o_ref[0] = res_ref[0] + gate_ref[0] * acc


def _outproj(a, w, res, gate, *, tn=512):
    b, l, k = a.shape
    n = w.shape[1]
    tm = min(ROW_TILE, l)
    return pl.pallas_call(
        _outproj_kernel,
        grid=(b, l // tm, n // tn),
        in_specs=[
            pl.BlockSpec((1, tm, k), lambda b_, i, j: (b_, i, 0)),
            pl.BlockSpec((k, tn), lambda b_, i, j: (0, j)),
            pl.BlockSpec((1, tm, tn), lambda b_, i, j: (b_, i, j)),
            pl.BlockSpec((1, 1, tn), (lambda b_, i, j: (0, 0, j)) if gate.shape[0] == 1 else (lambda b_, i, j: (b_, 0, j))),
        ],
        out_specs=pl.BlockSpec((1, tm, tn), lambda b_, i, j: (b_, i, j)),
        out_shape=jax.ShapeDtypeStruct((b, l, n), F32),
        compiler_params=_cparams(("parallel", "parallel", "parallel")),
        name="outproj",
    )(a, w, res, gate)


FFN_TILE = 256
FFN_HALO = BF16_SUBLANES


def _ffn_kernel(x_ref, xp_ref, xn_ref, nw_ref, sh_ref, sc_ref, g_ref, wa_ref, wg_ref, cwa_ref, cwg_ref, cba_ref, cbg_ref,
                wd_ref, fw_ref, o_ref, h_ref, acc_ref, *, final_norm):
    i = pl.program_id(1)
    j = pl.program_id(2)
    tm = x_ref.shape[1]
    hl = FFN_HALO

    @pl.when(j == 0)
    def _():
        nw, sh, sc = nw_ref[...], sh_ref[0], sc_ref[0]
        keep_p = jnp.where(i > 0, F32(1.0), F32(0.0))
        keep_n = jnp.where(i < pl.num_programs(1) - 1, F32(1.0), F32(0.0))
        h_ref[0:hl] = (_norm_mod(xp_ref[0], nw, sh, sc) * keep_p).astype(BF16)
        h_ref[hl:hl + tm] = _norm_mod(x_ref[0], nw, sh, sc).astype(BF16)
        h_ref[hl + tm:hl + tm + hl] = (_norm_mod(xn_ref[0], nw, sh, sc) * keep_n).astype(BF16)

    def up_conv(w_ref, cw_ref, cb_ref):
        u = jnp.dot(h_ref[...], w_ref[...], preferred_element_type=F32)
        cw = cw_ref[...]
        return (u[hl - 1:hl - 1 + tm] * cw[0:1] + u[hl:hl + tm] * cw[1:2] + u[hl + 1:hl + 1 + tm] * cw[2:3]) + cb_ref[...]

    act = (_silu(up_conv(wa_ref, cwa_ref, cba_ref)) * up_conv(wg_ref, cwg_ref, cbg_ref)).astype(BF16)
    part = jnp.dot(act, wd_ref[...], preferred_element_type=F32)

    @pl.when(j == 0)
    def _():
        acc_ref[...] = part

    @pl.when(j > 0)
    def _():
        acc_ref[...] += part

    @pl.when(j == pl.num_programs(2) - 1)
    def _():
        y = x_ref[0] + g_ref[0] * acc_ref[...]
        if final_norm:
            y = (y * lax.rsqrt(jnp.mean(y * y, axis=-1, keepdims=True) + NORM_EPS)) * fw_ref[...]
        o_ref[0] = y


def _ffn(x, nw, shift, scale, gate, wu, cw, cb, wd, final_w, *, final_norm):
    b, l, d = x.shape
    tf = FFN_TILE
    nt = wd.shape[0] // tf
    tm = min(ROW_TILE, l)
    hl = FFN_HALO
    per = tm // hl
    nblk = l // hl
    vec = lambda arr: pl.BlockSpec((1, 1, d), _bsel(arr))
    half_a = lambda b_, i, j: (0, j)
    half_g = lambda b_, i, j: (0, nt + j)
    return pl.pallas_call(
        functools.partial(_ffn_kernel, final_norm=final_norm),
        grid=(b, l // tm, nt),
        in_specs=[
            pl.BlockSpec((1, tm, d), lambda b_, i, j: (b_, i, 0)),
            pl.BlockSpec((1, hl, d), lambda b_, i, j: (b_, jnp.maximum(i * per - 1, 0), 0)),
            pl.BlockSpec((1, hl, d), lambda b_, i, j: (b_, jnp.minimum((i + 1) * per, nblk - 1), 0)),
            pl.BlockSpec((1, d), lambda b_, i, j: (0, 0)),
            vec(shift), vec(scale), vec(gate),
            pl.BlockSpec((d, tf), half_a), pl.BlockSpec((d, tf), half_g),
            pl.BlockSpec((CONV_W, tf), half_a), pl.BlockSpec((CONV_W, tf), half_g),
            pl.BlockSpec((1, tf), half_a), pl.BlockSpec((1, tf), half_g),
            pl.BlockSpec((tf, d), lambda b_, i, j: (j, 0)),
            pl.BlockSpec((1, d), lambda b_, i, j: (0, 0)),
        ],
        out_specs=pl.BlockSpec((1, tm, d), lambda b_, i, j: (b_, i, 0)),
        out_shape=jax.ShapeDtypeStruct((b, l, d), F32),
        scratch_shapes=[pltpu.VMEM((tm + 2 * hl, d), BF16), pltpu.VMEM((tm, d), F32)],
        compiler_params=_cparams(("parallel", "parallel", "arbitrary")),
        name="ffn",
    )(x, x, x, nw.reshape(1, d), shift, scale, gate, wu, wu, cw, cw, cb, cb, wd, final_w.reshape(1, d))


def _ret_kernel(lg_ref, ql_ref, kl_ref, vl_ref, gl_ref, qc_ref, kc_ref, vc_ref, gc_ref, ol_ref, oc_ref,
                s_ref, sb_ref):
    c = RET_CHUNK
    h = pl.program_id(1)
    lgf = lg_ref[0, h]
    lgb = lg_ref[1, h]
    n_ctx = qc_ref.shape[1] // c
    n_lat = ql_ref.shape[1] // c
    dv = vl_ref.shape[2]

    pos = lax.broadcasted_iota(jnp.int32, (c, 1), 0).astype(F32)
    kdec_f = jnp.exp(lgf * (c - 1.0 - pos))
    kdec_b = jnp.exp(lgb * pos)
    qdec_f = jnp.exp(lgf * (pos + 1.0))
    qdec_b = jnp.exp(lgb * (c - pos))
    one = jnp.ones((1, 1), F32)
    cdec_f = jnp.exp(lgf * c * one)
    cdec_b = jnp.exp(lgb * c * one)
    diff = (lax.broadcasted_iota(jnp.int32, (c, c), 0) - lax.broadcasted_iota(jnp.int32, (c, c), 1)).astype(F32)
    dmat = jnp.exp(jnp.where(diff >= 0, lgf, lgb) * jnp.abs(diff))

    def kv(k, v, kdec):
        kd = (k.astype(F32) * kdec).astype(BF16)
        return lax.dot_general(kd, v, (((0,), (0,)), ((), ())), preferred_element_type=F32)

    sb_ref[n_ctx - 1] = jnp.zeros((sb_ref.shape[1], dv), BF16)
    s_ref[...] = jnp.zeros(s_ref.shape, F32)
    for t in range(n_ctx - 1, -1, -1):
        sl = pl.ds(t * c, c)
        s_ref[...] = s_ref[...] * cdec_b + kv(kc_ref[0, sl, :], vc_ref[0, sl, :], kdec_b)
        if t > 0:
            sb_ref[t - 1] = s_ref[...].astype(BF16)

    def bwd_body(t, carry):
        ci = n_lat - 1 - t
        sb_ref[n_ctx + ci] = s_ref[...].astype(BF16)
        sl = pl.ds(pl.multiple_of(ci * c, c), c)
        s_ref[...] = s_ref[...] * cdec_b + kv(kl_ref[0, sl, :], vl_ref[0, sl, :], kdec_b)
        return carry

    lax.fori_loop(0, n_lat, bwd_body, 0)

    def chunk_out(q, k, v, g, sb):
        qf = q.astype(F32)
        s = lax.dot_general(q, k, (((1,), (1,)), ((), ())), preferred_element_type=F32) * dmat
        o = jnp.dot(s.astype(BF16), v, preferred_element_type=F32)
        o += jnp.dot((qf * qdec_f).astype(BF16), s_ref[...].astype(BF16), preferred_element_type=F32)
        o += jnp.dot((qf * qdec_b).astype(BF16), sb, preferred_element_type=F32)
        o = o * lax.rsqrt(jnp.mean(o * o, axis=-1, keepdims=True) + NORM_EPS)
        return (_silu(g.astype(F32)) * o).astype(BF16)

    s_ref[...] = jnp.zeros(s_ref.shape, F32)
    for t in range(n_ctx):
        sl = pl.ds(t * c, c)
        k, v = kc_ref[0, sl, :], vc_ref[0, sl, :]
        oc_ref[0, sl, :] = chunk_out(qc_ref[0, sl, :], k, v, gc_ref[0, sl, :], sb_ref[t])
        s_ref[...] = s_ref[...] * cdec_f + kv(k, v, kdec_f)

    def fwd_body(ci, carry):
        sl = pl.ds(pl.multiple_of(ci * c, c), c)
        k, v = kl_ref[0, sl, :], vl_ref[0, sl, :]
        ol_ref[0, sl, :] = chunk_out(ql_ref[0, sl, :], k, v, gl_ref[0, sl, :], sb_ref[n_ctx + ci])
        s_ref[...] = s_ref[...] * cdec_f + kv(k, v, kdec_f)
        return carry

    lax.fori_loop(0, n_lat, fwd_body, 0)


def _retention(lg, pl_, pc):
    b, l, n = pl_.shape
    lc = pc.shape[1]
    hh = RET_HEADS
    dk = n // (6 * hh)
    dv = 2 * dk
    c = RET_CHUNK

    def specs(length):
        return [
            pl.BlockSpec((1, length, dk), lambda b_, h: (b_, 0, h)),
            pl.BlockSpec((1, length, dk), lambda b_, h: (b_, 0, hh + h)),
            pl.BlockSpec((1, length, dv), lambda b_, h: (b_, 0, hh + h)),
            pl.BlockSpec((1, length, dv), lambda b_, h: (b_, 0, 2 * hh + h)),
        ]

    return pl.pallas_call(
        _ret_kernel,
        grid=(b, hh),
        in_specs=[pl.BlockSpec(memory_space=pltpu.SMEM)] + specs(l) + specs(lc),
        out_specs=[
            pl.BlockSpec((1, l, dv), lambda b_, h: (b_, 0, h)),
            pl.BlockSpec((1, lc, dv), lambda b_, h: (b_, 0, h)),
        ],
        out_shape=[jax.ShapeDtypeStruct((b, l, hh * dv), BF16), jax.ShapeDtypeStruct((b, lc, hh * dv), BF16)],
        scratch_shapes=[pltpu.VMEM((dk, dv), F32), pltpu.VMEM((lc // c + l // c, dk, dv), BF16)],
        compiler_params=_cparams(("parallel", "arbitrary")),
        name="retention",
    )(lg, pl_, pl_, pl_, pl_, pc, pc, pc, pc)


def _rope_tables(seq, dk):
    quarter = dk // 4
    t = jnp.arange(seq)
    pos_r = (t // GRID_W).astype(F32)
    pos_c = (t % GRID_W).astype(F32)
    inv = ROPE_BASE ** (-jnp.arange(quarter, dtype=F32) / quarter)
    ang_r = pos_r[:, None] * inv[None, :]
    ang_c = pos_c[:, None] * inv[None, :]
    cos = jnp.concatenate([jnp.cos(ang_r)] * 2 + [jnp.cos(ang_c)] * 2, axis=-1)
    sin = jnp.concatenate([-jnp.sin(ang_r), jnp.sin(ang_r), -jnp.sin(ang_c), jnp.sin(ang_c)], axis=-1)
    return cos, sin


def _na_window(rows):
    kh = min(NA_MAX_KH, rows)
    return kh, NA_QROWS + kh - 1


def _na_kernel(m_ref, q_ref, k_ref, v_ref, kc_ref, vc_ref, o_ref, bias_ref, *, rows):
    b = pl.program_id(1)
    rb = pl.program_id(2)
    nrb = pl.num_programs(2)
    kh, wrows = _na_window(rows)
    qr_n = NA_QROWS
    w = GRID_W
    hd = q_ref.shape[2] // 2

    @pl.when((b == 0) & (rb == 0))
    def _():
        qc = lax.broadcasted_iota(jnp.int32, (w, LANES), 0)
        kc = lax.broadcasted_iota(jnp.int32, (w, LANES), 1)
        cs = jnp.clip(qc - NA_KW // 2, 0, w - NA_KW)
        col_valid = (kc >= cs) & (kc < cs + NA_KW)
        toe = [[jnp.where(col_valid,
                          pltpu.roll(jnp.broadcast_to(m_ref[e, dr:dr + 1, :], (w, LANES)), 0, 1, stride=1, stride_axis=0),
                          MASK_VALUE)[:, :w]
                for dr in range(2 * kh - 1)] for e in range(2)]
        for variant in range(3):
            r0 = (0, qr_n, rows - qr_n)[variant]
            ws = min(max(r0 - kh // 2, 0), rows - wrows)
            for qr in range(qr_n):
                r = r0 + qr
                rs = min(max(r - kh // 2, 0), rows - kh)
                for krr in range(wrows):
                    kr = ws + krr
                    for e in range(2):
                        dst = (e, variant, slice(qr * w, (qr + 1) * w), slice(krr * w, (krr + 1) * w))
                        if rs <= kr < rs + kh:
                            bias_ref[dst] = toe[e][kr - r + kh - 1]
                        else:
                            bias_ref[dst] = jnp.full((w, w), MASK_VALUE, F32)

    variant = jnp.where(rb == 0, 0, jnp.where(rb == nrb - 1, 2, 1))
    ws = jnp.clip(rb * qr_n - kh // 2, 0, rows - wrows)
    win = pl.ds(pl.multiple_of(ws * w, w), wrows * w)
    kwin = k_ref[0, win, :]
    vwin = v_ref[0, win, :]
    kc = kc_ref[0]
    vc = vc_ref[0]
    q = q_ref[0]
    lane = lax.broadcasted_iota(jnp.int32, q.shape, 1)
    outs = []
    for e in range(2):
        qe = jnp.where((lane >= e * hd) & (lane < (e + 1) * hd), q, jnp.zeros_like(q))
        s_loc = lax.dot_general(qe, kwin, (((1,), (1,)), ((), ())), preferred_element_type=F32)
        s_loc = s_loc + bias_ref[e, variant]
        s_ctx = lax.dot_general(qe, kc, (((1,), (1,)), ((), ())), preferred_element_type=F32)
        m = jnp.maximum(jnp.max(s_loc, axis=-1, keepdims=True), jnp.max(s_ctx, axis=-1, keepdims=True))
        p_loc = jnp.exp(s_loc - m)
        p_ctx = jnp.exp(s_ctx - m)
        den = jnp.sum(p_loc, axis=-1, keepdims=True) + jnp.sum(p_ctx, axis=-1, keepdims=True)
        o = jnp.dot(p_loc.astype(BF16), vwin, preferred_element_type=F32)
        o += jnp.dot(p_ctx.astype(BF16), vc, preferred_element_type=F32)
        outs.append(o / den)
    o_ref[0] = jnp.where(lane < hd, outs[0], outs[1]).astype(o_ref.dtype)


def _na_bias_table(rpb):
    padded = jnp.pad(rpb, ((0, 0), (0, 0), (0, LANES - rpb.shape[2])))
    return jnp.roll(padded, -(NA_KW - 1), axis=2)


def _na_attention(qkv, kvc, rpb):
    b, s, d3 = qkv.shape
    d = d3 // 3
    lc = kvc.shape[1]
    rows = s // GRID_W
    pairs = d // LANES
    _, wrows = _na_window(rows)
    qrows = NA_QROWS * GRID_W
    mtab = _na_bias_table(rpb)
    return pl.pallas_call(
        functools.partial(_na_kernel, rows=rows),
        grid=(pairs, b, rows // NA_QROWS),
        in_specs=[
            pl.BlockSpec((2,) + mtab.shape[1:], lambda p, b_, r: (p, 0, 0)),
            pl.BlockSpec((1, qrows, LANES), lambda p, b_, r: (b_, r, p)),
            pl.BlockSpec((1, s, LANES), lambda p, b_, r: (b_, 0, pairs + p)),
            pl.BlockSpec((1, s, LANES), lambda p, b_, r: (b_, 0, 2 * pairs + p)),
            pl.BlockSpec((1, lc, LANES), lambda p, b_, r: (b_, 0, p)),
            pl.BlockSpec((1, lc, LANES), lambda p, b_, r: (b_, 0, pairs + p)),
        ],
        out_specs=pl.BlockSpec((1, qrows, LANES), lambda p, b_, r: (b_, r, p)),
        out_shape=jax.ShapeDtypeStruct((b, s, d), BF16),
        scratch_shapes=[pltpu.VMEM((2, 3, qrows, wrows * GRID_W), F32)],
        compiler_params=_cparams(("arbitrary", "arbitrary", "arbitrary")),
        name="na_attention",
    )(mtab, qkv, qkv, qkv, kvc, kvc)


def kernel(x, c, ctx, c_ctx, ada_w, ada_b, norm1_w, norm2_w, ret_w_in, ret_decay_fwd, ret_decay_bwd, ret_w_out,
           na_w_qkv, na_rpb, na_w_out, ffn_w_up, ffn_conv_w, ffn_conv_b, ffn_w_down, final_norm_w):
    b, seq, d = x.shape
    depth = ada_w.shape[0]
    n_mixers = 2

    pad = BF16_SUBLANES - (b + 1) % BF16_SUBLANES
    cc = jnp.concatenate([c, c_ctx[None, :], jnp.zeros((pad, d), F32)], axis=0)
    mods = _ada(cc, ada_w, ada_b).reshape(depth, cc.shape[0], 6, d)

    def mod(i, k):
        return mods[i, :b, k][:, None, :], mods[i, b:b + 1, k][:, None, :]

    for i in range(depth):
        last = i == depth - 1
        j = i // n_mixers
        (sh1, csh1), (sc1, csc1), (g1, cg1) = mod(i, 0), mod(i, 1), mod(i, 2)
        (sh2, csh2), (sc2, csc2), (g2, cg2) = mod(i, 3), mod(i, 4), mod(i, 5)
        if i % n_mixers == 0:
            hk = ret_w_in.shape[2] // 6
            dk = hk // RET_HEADS
            w_in = ret_w_in[j].astype(BF16)
            kw = dict(scale_cols=(hk, 2 * hk), scale_val=dk ** -0.5)
            pl_ = _proj(x, norm1_w[i], sh1, sc1, w_in, rope=_rope_tables(seq, dk), rope_cols=2 * hk, **kw)
            pc = _proj(ctx, norm1_w[i], csh1, csc1, w_in, **kw)
            lg = jnp.stack([-jnp.exp(ret_decay_fwd[j].astype(F32)), -jnp.exp(ret_decay_bwd[j].astype(F32))])
            y, yc = _retention(lg, pl_, pc)
            w_out = ret_w_out[j].astype(BF16)
        else:
            w_qkv = na_w_qkv[j].astype(BF16)
            qkv = _proj(x, norm1_w[i], sh1, sc1, w_qkv, scale_cols=(0, d), scale_val=(d // NA_HEADS) ** -0.5)
            kvc = _proj(ctx, norm1_w[i], csh1, csc1, w_qkv[:, d:])
            if not last:
                raise NotImplementedError("context output of the attention mixer is only needed for depth > 2")
            y, yc = _na_attention(qkv, kvc, na_rpb[j]), None
            w_out = na_w_out[j].astype(BF16)
        ffn_w = (ffn_w_up[i].astype(BF16), ffn_conv_w[i], ffn_conv_b[i][None, :], ffn_w_down[i].astype(BF16))
        x = _outproj(y, w_out, x, g1)
        x = _ffn(x, norm2_w[i], sh2, sc2, g2, *ffn_w, final_norm_w, final_norm=last)
        if not last:
            ctx = _outproj(yc, w_out, ctx, cg1)
            ctx = _ffn(ctx, norm2_w[i], csh2, csc2, cg2, *ffn_w, final_norm_w, final_norm=False)
    return x
```

```python
import functools

import jax
import jax.numpy as jnp
import numpy as np
from jax import lax
from jax.experimental import pallas as pl
from jax.experimental.pallas import tpu as pltpu

F32 = jnp.float32
BF16 = jnp.bfloat16

GRID_W = 64
RET_HEADS = 4
ROPE_BASE = 10000.0
NA_HEADS = 16
NA_MAX_KH = 8
NA_KW = 16
CONV_W = 3
NORM_EPS = 1e-6

LANES = 128
BF16_SUBLANES = 16
VMEM_LIMIT_BYTES = 56 * 1024 * 1024

ROW_TILE = 1024
PROJ_ROW_TILE = 512
MM_SUB = 256
RET_CHUNK = 256
NA_QROWS = 4
NA_BLOCKS_PER_STEP = 2
MASK_VALUE = -1e30


def _cparams(sem):
    return pltpu.CompilerParams(dimension_semantics=sem, vmem_limit_bytes=VMEM_LIMIT_BYTES)


def _bsel(arr):
    if arr.shape[0] == 1:
        return lambda b, *_: (0, 0, 0)
    return lambda b, *_: (b, 0, 0)


def _norm_mod(x, nw, shift, scale):
    y = x * lax.rsqrt(jnp.mean(x * x, axis=-1, keepdims=True) + NORM_EPS)
    return (y * nw) * (1.0 + scale) + shift


def _silu(x):
    return x / (1.0 + jnp.exp(-x))


def _ada_kernel(c_ref, w_ref, b_ref, o_ref):
    s = _silu(c_ref[...])
    o_ref[0] = jnp.dot(s, w_ref[0], preferred_element_type=F32) + b_ref[0]


def _ada(cc, ada_w, ada_b):
    depth, d, n = ada_w.shape
    tn = 1536
    return pl.pallas_call(
        _ada_kernel,
        grid=(depth, n // tn),
        in_specs=[
            pl.BlockSpec(cc.shape, lambda i, j: (0, 0)),
            pl.BlockSpec((1, d, tn), lambda i, j: (i, 0, j)),
            pl.BlockSpec((1, 1, tn), lambda i, j: (i, 0, j)),
        ],
        out_specs=pl.BlockSpec((1, cc.shape[0], tn), lambda i, j: (i, 0, j)),
        out_shape=jax.ShapeDtypeStruct((depth, cc.shape[0], n), F32),
        compiler_params=_cparams(("parallel", "parallel")),
        name="ada",
    )(cc, ada_w, ada_b.reshape(depth, 1, n))


def _pipelined_units(units, matmul, epilogue):
    acc = matmul(units[0])
    for n, u in enumerate(units):
        nxt = matmul(units[n + 1]) if n + 1 < len(units) else None
        epilogue(acc, u)
        acc = nxt


def _proj_kernel(*refs, tn, rope_cols, scale_cols, scale_val):
    if rope_cols:
        x_ref, nw_ref, sh_ref, sc_ref, w_ref, cos_ref, sin_ref, o_ref, h_ref = refs
    else:
        x_ref, nw_ref, sh_ref, sc_ref, w_ref, o_ref, h_ref = refs
    tm = x_ref.shape[1]
    sub = min(MM_SUB, tm)

    def matmul(u):
        s, j = u
        rows = slice(s * sub, (s + 1) * sub)
        if j == 0:
            h_ref[rows] = _norm_mod(x_ref[0, rows], nw_ref[...], sh_ref[0], sc_ref[0]).astype(BF16)
        return jnp.dot(h_ref[rows], w_ref[:, j * tn:(j + 1) * tn], preferred_element_type=F32)

    def epilogue(acc, u):
        s, j = u
        rows = slice(s * sub, (s + 1) * sub)
        col = j * tn
        if scale_cols[0] <= col < scale_cols[1]:
            acc = acc * scale_val
        if col >= rope_cols:
            o_ref[0, rows, col:col + tn] = acc.astype(o_ref.dtype)
            return
        for g in range(tn // LANES):
            part = acc[:, g * LANES:(g + 1) * LANES]
            t = (g % 2) * LANES
            rot = part * cos_ref[rows, t:t + LANES] + pltpu.roll(part, LANES // 2, 1) * sin_ref[rows, t:t + LANES]
            o_ref[0, rows, col + g * LANES:col + (g + 1) * LANES] = rot.astype(o_ref.dtype)

    units = [(s, j) for s in range(tm // sub) for j in range(w_ref.shape[1] // tn)]
    _pipelined_units(units, matmul, epilogue)


def _proj(x, nw, shift, scale, w, *, tn=512, rope=None, rope_cols=0, scale_cols=(0, 0), scale_val=1.0):
    b, l, d = x.shape
    n = w.shape[1]
    tm = min(PROJ_ROW_TILE, l)
    assert n % tn == 0 and rope_cols % tn == 0 and scale_cols[0] % tn == 0 and scale_cols[1] % tn == 0
    in_specs = [
        pl.BlockSpec((1, tm, d), lambda b_, i: (b_, i, 0)),
        pl.BlockSpec((1, d), lambda b_, i: (0, 0)),
        pl.BlockSpec((1, 1, d), _bsel(shift)),
        pl.BlockSpec((1, 1, d), _bsel(scale)),
        pl.BlockSpec((d, n), lambda b_, i: (0, 0), pipeline_mode=pl.Buffered(1)),
    ]
    args = [x, nw.reshape(1, d), shift, scale, w]
    if rope is not None:
        cos, sin = rope
        in_specs += [pl.BlockSpec((tm, cos.shape[1]), lambda b_, i: (i, 0))] * 2
        args += [cos, sin]
    kern = functools.partial(_proj_kernel, tn=tn, rope_cols=(rope_cols if rope is not None else 0),
                             scale_cols=scale_cols, scale_val=scale_val)
    return pl.pallas_call(
        kern,
        grid=(b, l // tm),
        in_specs=in_specs,
        out_specs=pl.BlockSpec((1, tm, n), lambda b_, i: (b_, i, 0)),
        out_shape=jax.ShapeDtypeStruct((b, l, n), BF16),
        scratch_shapes=[pltpu.VMEM((tm, d), BF16)],
        compiler_params=_cparams(("parallel", "parallel")),
        name="proj",
    )(*args)


def _outproj_kernel(a_ref, w_ref, res_ref, gate_ref, o_ref, *, tn):
    tm = a_ref.shape[1]
    sub = min(MM_SUB, tm)

    def matmul(u):
        s, j = u
        return jnp.dot(a_ref[0, s * sub:(s + 1) * sub], w_ref[:, j * tn:(j + 1) * tn], preferred_element_type=F32)

    def epilogue(acc, u):
        s, j = u
        rows, cols = slice(s * sub, (s + 1) * sub), slice(j * tn, (j + 1) * tn)
        o_ref[0, rows, cols] = res_ref[0, rows, cols] + gate_ref[0, :, cols] * acc

    units = [(s, j) for s in range(tm // sub) for j in range(w_ref.shape[1] // tn)]
    _pipelined_units(units, matmul, epilogue)


def _outproj(a, w, res, gate, *, tn=512):
    b, l, k = a.shape
    n = w.shape[1]
    tm = min(ROW_TILE, l)
    return pl.pallas_call(
        functools.partial(_outproj_kernel, tn=tn),
        grid=(b, l // tm),
        in_specs=[
            pl.BlockSpec((1, tm, k), lambda b_, i: (b_, i, 0)),
            pl.BlockSpec((k, n), lambda b_, i: (0, 0), pipeline_mode=pl.Buffered(1)),
            pl.BlockSpec((1, tm, n), lambda b_, i: (b_, i, 0)),
            pl.BlockSpec((1, 1, n), _bsel(gate)),
        ],
        out_specs=pl.BlockSpec((1, tm, n), lambda b_, i: (b_, i, 0)),
        out_shape=jax.ShapeDtypeStruct((b, l, n), F32),
        compiler_params=_cparams(("parallel", "parallel")),
        name="outproj",
    )(a, w, res, gate)


FFN_TILE = 256
FFN_HALO = BF16_SUBLANES
FFN_SUB = 256
FFN_ROW_TILE = 512


def _ffn_kernel(x_ref, xp_ref, xn_ref, nw_ref, sh_ref, sc_ref, g_ref, wu_ref, cw_ref, cb_ref, wd_ref, fw_ref,
                o_ref, h_ref, ua_ref, ug_ref, act_ref, *, final_norm):
    i = pl.program_id(1)
    tm = x_ref.shape[1]
    hl = FFN_HALO
    tf = FFN_TILE
    f = wd_ref.shape[0]
    sub = min(FFN_SUB, tm)
    nsub = tm // sub
    chunks = [(0, sub + 2 * hl)] + [(2 * hl + c * sub, 2 * hl + (c + 1) * sub) for c in range(1, nsub)]

    def normalise(lo, hi):
        nw, sh, sc = nw_ref[...], sh_ref[0], sc_ref[0]
        keep_p = jnp.where(i > 0, F32(1.0), F32(0.0))
        keep_n = jnp.where(i < pl.num_programs(1) - 1, F32(1.0), F32(0.0))
        regions = ((0, hl, xp_ref, 0, keep_p), (hl, hl + tm, x_ref, hl, None), (hl + tm, 2 * hl + tm, xn_ref, hl + tm, keep_n))
        for r0, r1, src, off, keep in regions:
            a, b = max(lo, r0), min(hi, r1)
            if a < b:
                v = _norm_mod(src[0, a - off:b - off, :], nw, sh, sc)
                h_ref[a:b] = (v if keep is None else v * keep).astype(BF16)

    def conv(u_ref, slot, col, s):
        base = hl + s * sub
        cw = cw_ref[:, col:col + tf]
        return (u_ref[slot, base - 1:base - 1 + sub] * cw[0:1] + u_ref[slot, base:base + sub] * cw[1:2]
                + u_ref[slot, base + 1:base + 1 + sub] * cw[2:3]) + cb_ref[:, col:col + tf]

    def up(t, c):
        lo, hi = chunks[c]
        if t == 0:
            normalise(lo, hi)
        ua_ref[t % 2, lo:hi] = jnp.dot(h_ref[lo:hi], wu_ref[:, t * tf:(t + 1) * tf], preferred_element_type=F32)
        ug_ref[t % 2, lo:hi] = jnp.dot(h_ref[lo:hi], wu_ref[:, f + t * tf:f + (t + 1) * tf], preferred_element_type=F32)

    units = [(t, c) for t in range(f // tf) for c in range(nsub)]
    up(*units[0])
    for n, (t, s) in enumerate(units):
        if n + 1 < len(units):
            up(*units[n + 1])
        act = _silu(conv(ua_ref, t % 2, t * tf, s)) * conv(ug_ref, t % 2, f + t * tf, s)
        act_ref[s * sub:(s + 1) * sub, t * tf:(t + 1) * tf] = act.astype(BF16)

    for s in range(nsub):
        rows = slice(s * sub, (s + 1) * sub)
        y = x_ref[0, rows] + g_ref[0] * jnp.dot(act_ref[rows], wd_ref[...], preferred_element_type=F32)
        if final_norm:
            y = (y * lax.rsqrt(jnp.mean(y * y, axis=-1, keepdims=True) + NORM_EPS)) * fw_ref[...]
        o_ref[0, rows] = y


def _ffn(x, nw, shift, scale, gate, wu, cw, cb, wd, final_w, *, final_norm):
    b, l, d = x.shape
    f = wd.shape[0]
    tf = FFN_TILE
    assert f % tf == 0, wd.shape
    tm = min(FFN_ROW_TILE, l)
    hl = FFN_HALO
    per = tm // hl
    nblk = l // hl
    vec = lambda arr: pl.BlockSpec((1, 1, d), _bsel(arr))
    whole = lambda arr: pl.BlockSpec(arr.shape, lambda b_, i: (0, 0), pipeline_mode=pl.Buffered(1))
    return pl.pallas_call(
        functools.partial(_ffn_kernel, final_norm=final_norm),
        grid=(b, l // tm),
        in_specs=[
            pl.BlockSpec((1, tm, d), lambda b_, i: (b_, i, 0)),
            pl.BlockSpec((1, hl, d), lambda b_, i: (b_, jnp.maximum(i * per - 1, 0), 0)),
            pl.BlockSpec((1, hl, d), lambda b_, i: (b_, jnp.minimum((i + 1) * per, nblk - 1), 0)),
            pl.BlockSpec((1, d), lambda b_, i: (0, 0)),
            vec(shift), vec(scale), vec(gate),
            whole(wu), whole(cw), whole(cb), whole(wd),
            pl.BlockSpec((1, d), lambda b_, i: (0, 0)),
        ],
        out_specs=pl.BlockSpec((1, tm, d), lambda b_, i: (b_, i, 0)),
        out_shape=jax.ShapeDtypeStruct((b, l, d), F32),
        scratch_shapes=[pltpu.VMEM((tm + 2 * hl, d), BF16), pltpu.VMEM((2, tm + 2 * hl, tf), F32),
                        pltpu.VMEM((2, tm + 2 * hl, tf), F32), pltpu.VMEM((tm, f), BF16)],
        compiler_params=_cparams(("parallel", "parallel")),
        name="ffn",
    )(x, x, x, nw.reshape(1, d), shift, scale, gate, wu, cw, cb, wd, final_w.reshape(1, d))


def _ret_kernel(lg_ref, ql_ref, kl_ref, vl_ref, gl_ref, qc_ref, kc_ref, vc_ref, gc_ref, ol_ref, oc_ref, sb_ref):
    c = RET_CHUNK
    h = pl.program_id(1)
    lgf = lg_ref[0, h]
    lgb = lg_ref[1, h]
    n_ctx = qc_ref.shape[1] // c
    n_lat = ql_ref.shape[1] // c

    pos = lax.broadcasted_iota(jnp.int32, (c, 1), 0).astype(F32)
    kdec_f = jnp.exp(lgf * (c - 1.0 - pos))
    kdec_b = jnp.exp(lgb * pos)
    qdec_f = jnp.exp(lgf * (pos + 1.0))
    qdec_b = jnp.exp(lgb * (c - pos))
    one = jnp.ones((1, 1), F32)
    cdec_f = jnp.exp(lgf * c * one)
    cdec_b = jnp.exp(lgb * c * one)
    diff = (lax.broadcasted_iota(jnp.int32, (c, c), 0) - lax.broadcasted_iota(jnp.int32, (c, c), 1)).astype(F32)
    dmat = jnp.exp(jnp.where(diff >= 0, lgf, lgb) * jnp.abs(diff))

    seq = [(qc_ref, kc_ref, vc_ref, gc_ref, oc_ref, slice(t * c, (t + 1) * c)) for t in range(n_ctx)]
    seq += [(ql_ref, kl_ref, vl_ref, gl_ref, ol_ref, slice(t * c, (t + 1) * c)) for t in range(n_lat)]
    n = len(seq)

    def kv(idx, kdec):
        _, k_ref, v_ref, _, _, sl = seq[idx]
        kd = (k_ref[0, sl, :].astype(F32) * kdec).astype(BF16)
        return lax.dot_general(kd, v_ref[0, sl, :], (((0,), (0,)), ((), ())), preferred_element_type=F32)

    def qk(idx):
        q_ref, k_ref, _, _, _, sl = seq[idx]
        return lax.dot_general(q_ref[0, sl, :], k_ref[0, sl, :], (((1,), (1,)), ((), ())), preferred_element_type=F32)

    bwd_order = list(range(n_ctx - 1, -1, -1)) + list(range(n - 1, n_ctx - 1, -1))
    s = None
    for pos_, idx in enumerate(bwd_order):
        if s is not None:
            sb_ref[idx] = s.astype(BF16)
        if pos_ + 1 < n:
            upd = kv(idx, kdec_b)
            s = upd if s is None else s * cdec_b + upd

    s = None
    scores = qk(0)
    for idx in range(n):
        q_ref, _, v_ref, g_ref, o_ref, sl = seq[idx]
        qf = q_ref[0, sl, :].astype(F32)
        cross = None
        if idx != bwd_order[0]:
            cross = jnp.dot((qf * qdec_b).astype(BF16), sb_ref[idx], preferred_element_type=F32)
        if s is not None:
            cf = jnp.dot((qf * qdec_f).astype(BF16), s.astype(BF16), preferred_element_type=F32)
            cross = cf if cross is None else cross + cf
        upd = kv(idx, kdec_f) if idx + 1 < n else None
        nxt = qk(idx + 1) if idx + 1 < n else None
        o = jnp.dot((scores * dmat).astype(BF16), v_ref[0, sl, :], preferred_element_type=F32)
        if cross is not None:
            o = o + cross
        if upd is not None:
            s = upd if s is None else s * cdec_f + upd
        o = o * lax.rsqrt(jnp.mean(o * o, axis=-1, keepdims=True) + NORM_EPS)
        o_ref[0, sl, :] = (_silu(g_ref[0, sl, :].astype(F32)) * o).astype(BF16)
        scores = nxt


def _retention(lg, pl_, pc):
    b, l, n = pl_.shape
    lc = pc.shape[1]
    hh = RET_HEADS
    dk = n // (6 * hh)
    dv = 2 * dk
    c = RET_CHUNK

    def specs(length):
        return [
            pl.BlockSpec((1, length, dk), lambda b_, h: (b_, 0, h)),
            pl.BlockSpec((1, length, dk), lambda b_, h: (b_, 0, hh + h)),
            pl.BlockSpec((1, length, dv), lambda b_, h: (b_, 0, hh + h)),
            pl.BlockSpec((1, length, dv), lambda b_, h: (b_, 0, 2 * hh + h)),
        ]

    return pl.pallas_call(
        _ret_kernel,
        grid=(b, hh),
        in_specs=[pl.BlockSpec(memory_space=pltpu.SMEM)] + specs(l) + specs(lc),
        out_specs=[
            pl.BlockSpec((1, l, dv), lambda b_, h: (b_, 0, h)),
            pl.BlockSpec((1, lc, dv), lambda b_, h: (b_, 0, h)),
        ],
        out_shape=[jax.ShapeDtypeStruct((b, l, hh * dv), BF16), jax.ShapeDtypeStruct((b, lc, hh * dv), BF16)],
        scratch_shapes=[pltpu.VMEM((lc // c + l // c, dk, dv), BF16)],
        compiler_params=_cparams(("parallel", "arbitrary")),
        name="retention",
    )(lg, pl_, pl_, pl_, pl_, pc, pc, pc, pc)


def _rope_tables(seq, dk):
    quarter = dk // 4
    t = jnp.arange(seq)
    pos_r = (t // GRID_W).astype(F32)
    pos_c = (t % GRID_W).astype(F32)
    inv = ROPE_BASE ** (-jnp.arange(quarter, dtype=F32) / quarter)
    ang_r = pos_r[:, None] * inv[None, :]
    ang_c = pos_c[:, None] * inv[None, :]
    cos = jnp.concatenate([jnp.cos(ang_r)] * 2 + [jnp.cos(ang_c)] * 2, axis=-1)
    sin = jnp.concatenate([-jnp.sin(ang_r), jnp.sin(ang_r), -jnp.sin(ang_c), jnp.sin(ang_c)], axis=-1)
    return cos, sin


def _na_window(rows):
    kh = min(NA_MAX_KH, rows)
    return kh, NA_QROWS + kh - 1


def _na_kernel(m_ref, q_ref, k_ref, v_ref, kc_ref, vc_ref, o_ref, bias_ref, *, rows):
    b = pl.program_id(1)
    rb = pl.program_id(2)
    nrb = pl.num_programs(2)
    kh, wrows = _na_window(rows)
    qr_n = NA_QROWS
    w = GRID_W
    hd = q_ref.shape[2] // 2

    @pl.when((b == 0) & (rb == 0))
    def _():
        qc = lax.broadcasted_iota(jnp.int32, (w, LANES), 0)
        kc = lax.broadcasted_iota(jnp.int32, (w, LANES), 1)
        cs = jnp.clip(qc - NA_KW // 2, 0, w - NA_KW)
        col_valid = (kc >= cs) & (kc < cs + NA_KW)
        toe = [[jnp.where(col_valid,
                          pltpu.roll(jnp.broadcast_to(m_ref[e, dr:dr + 1, :], (w, LANES)), 0, 1, stride=1, stride_axis=0),
                          MASK_VALUE)[:, :w]
                for dr in range(2 * kh - 1)] for e in range(2)]
        for variant in range(3):
            r0 = (0, qr_n, rows - qr_n)[variant]
            ws = min(max(r0 - kh // 2, 0), rows - wrows)
            for qr in range(qr_n):
                r = r0 + qr
                rs = min(max(r - kh // 2, 0), rows - kh)
                for krr in range(wrows):
                    kr = ws + krr
                    for e in range(2):
                        dst = (e, variant, slice(qr * w, (qr + 1) * w), slice(krr * w, (krr + 1) * w))
                        if rs <= kr < rs + kh:
                            bias_ref[dst] = toe[e][kr - r + kh - 1]
                        else:
                            bias_ref[dst] = jnp.full((w, w), MASK_VALUE, F32)

    kc = kc_ref[0]
    vc = vc_ref[0]
    nq = qr_n * w
    lane = lax.broadcasted_iota(jnp.int32, (nq, q_ref.shape[2]), 1)
    nblk = NA_BLOCKS_PER_STEP
    trans_b = (((1,), (1,)), ((), ()))

    chains = []
    for k in range(nblk):
        blk = rb * nblk + k
        variant = jnp.where(blk == 0, 0, jnp.where(blk == nrb * nblk - 1, 2, 1))
        ws = jnp.clip(blk * qr_n - kh // 2, 0, rows - wrows)
        win = pl.ds(pl.multiple_of(ws * w, w), wrows * w)
        kwin = k_ref[0, win, :]
        q = q_ref[0, k * nq:(k + 1) * nq, :]
        for e in range(2):
            qe = jnp.where((lane >= e * hd) & (lane < (e + 1) * hd), q, jnp.zeros_like(q))
            s_loc = lax.dot_general(qe, kwin, trans_b, preferred_element_type=F32)
            s_ctx = lax.dot_general(qe, kc, trans_b, preferred_element_type=F32)
            chains.append((k, e, variant, win, s_loc, s_ctx))

    outs = {}
    for k, e, variant, win, s_loc, s_ctx in chains:
        s_loc = s_loc + bias_ref[e, variant]
        m = jnp.maximum(jnp.max(s_loc, axis=-1, keepdims=True), jnp.max(s_ctx, axis=-1, keepdims=True))
        p_loc = jnp.exp(s_loc - m)
        p_ctx = jnp.exp(s_ctx - m)
        den = jnp.sum(p_loc, axis=-1, keepdims=True) + jnp.sum(p_ctx, axis=-1, keepdims=True)
        o = jnp.dot(p_loc.astype(BF16), v_ref[0, win, :], preferred_element_type=F32)
        o += jnp.dot(p_ctx.astype(BF16), vc, preferred_element_type=F32)
        outs[k, e] = o / den
        if e == 1:
            o_ref[0, k * nq:(k + 1) * nq, :] = jnp.where(lane < hd, outs[k, 0], outs[k, 1]).astype(o_ref.dtype)


def _na_bias_table(rpb):
    padded = jnp.pad(rpb, ((0, 0), (0, 0), (0, LANES - rpb.shape[2])))
    return jnp.roll(padded, -(NA_KW - 1), axis=2)


def _na_attention(qkv, kvc, rpb):
    b, s, d3 = qkv.shape
    d = d3 // 3
    lc = kvc.shape[1]
    rows = s // GRID_W
    pairs = d // LANES
    kh, wrows = _na_window(rows)
    qrows = NA_QROWS * GRID_W
    srows = NA_BLOCKS_PER_STEP * qrows
    assert kh == 2 * NA_QROWS and rows % (NA_QROWS * NA_BLOCKS_PER_STEP) == 0 and rows >= wrows + NA_QROWS, rows
    mtab = _na_bias_table(rpb)
    return pl.pallas_call(
        functools.partial(_na_kernel, rows=rows),
        grid=(pairs, b, s // srows),
        in_specs=[
            pl.BlockSpec((2,) + mtab.shape[1:], lambda p, b_, r: (p, 0, 0)),
            pl.BlockSpec((1, srows, LANES), lambda p, b_, r: (b_, r, p)),
            pl.BlockSpec((1, s, LANES), lambda p, b_, r: (b_, 0, pairs + p)),
            pl.BlockSpec((1, s, LANES), lambda p, b_, r: (b_, 0, 2 * pairs + p)),
            pl.BlockSpec((1, lc, LANES), lambda p, b_, r: (b_, 0, p)),
            pl.BlockSpec((1, lc, LANES), lambda p, b_, r: (b_, 0, pairs + p)),
        ],
        out_specs=pl.BlockSpec((1, srows, LANES), lambda p, b_, r: (b_, r, p)),
        out_shape=jax.ShapeDtypeStruct((b, s, d), BF16),
        scratch_shapes=[pltpu.VMEM((2, 3, qrows, wrows * GRID_W), F32)],
        compiler_params=_cparams(("arbitrary", "arbitrary", "arbitrary")),
        name="na_attention",
    )(mtab, qkv, qkv, qkv, kvc, kvc)


def kernel(x, c, ctx, c_ctx, ada_w, ada_b, norm1_w, norm2_w, ret_w_in, ret_decay_fwd, ret_decay_bwd, ret_w_out,
           na_w_qkv, na_rpb, na_w_out, ffn_w_up, ffn_conv_w, ffn_conv_b, ffn_w_down, final_norm_w):
    b, seq, d = x.shape
    depth = ada_w.shape[0]
    n_mixers = 2

    pad = BF16_SUBLANES - (b + 1) % BF16_SUBLANES
    cc = jnp.concatenate([c, c_ctx[None, :], jnp.zeros((pad, d), F32)], axis=0)
    mods = _ada(cc, ada_w, ada_b).reshape(depth, cc.shape[0], 6, d)

    def mod(i, k):
        return mods[i, :b, k][:, None, :], mods[i, b:b + 1, k][:, None, :]

    for i in range(depth):
        last = i == depth - 1
        j = i // n_mixers
        (sh1, csh1), (sc1, csc1), (g1, cg1) = mod(i, 0), mod(i, 1), mod(i, 2)
        (sh2, csh2), (sc2, csc2), (g2, cg2) = mod(i, 3), mod(i, 4), mod(i, 5)
        if i % n_mixers == 0:
            hk = ret_w_in.shape[2] // 6
            dk = hk // RET_HEADS
            w_in = ret_w_in[j].astype(BF16)
            kw = dict(scale_cols=(hk, 2 * hk), scale_val=dk ** -0.5)
            pl_ = _proj(x, norm1_w[i], sh1, sc1, w_in, rope=_rope_tables(seq, dk), rope_cols=2 * hk, **kw)
            pc = _proj(ctx, norm1_w[i], csh1, csc1, w_in, **kw)
            lg = jnp.stack([-jnp.exp(ret_decay_fwd[j].astype(F32)), -jnp.exp(ret_decay_bwd[j].astype(F32))])
            y, yc = _retention(lg, pl_, pc)
            w_out = ret_w_out[j].astype(BF16)
        else:
            w_qkv = na_w_qkv[j].astype(BF16)
            qkv = _proj(x, norm1_w[i], sh1, sc1, w_qkv, scale_cols=(0, d), scale_val=(d // NA_HEADS) ** -0.5)
            kvc = _proj(ctx, norm1_w[i], csh1, csc1, w_qkv[:, d:])
            if not last:
                raise NotImplementedError("context output of the attention mixer is only needed for depth > 2")
            y, yc = _na_attention(qkv, kvc, na_rpb[j]), None
            w_out = na_w_out[j].astype(BF16)
        ffn_w = (ffn_w_up[i].astype(BF16), ffn_conv_w[i], ffn_conv_b[i][None, :], ffn_w_down[i].astype(BF16))
        x = _outproj(y, w_out, x, g1)
        x = _ffn(x, norm2_w[i], sh2, sc2, g2, *ffn_w, final_norm_w, final_norm=last)
        if not last:
            ctx = _outproj(yc, w_out, ctx, cg1)
            ctx = _ffn(ctx, norm2_w[i], csh2, csc2, cg2, *ffn_w, final_norm_w, final_norm=False)
    return x
```

```python
import functools

import jax
import jax.numpy as jnp
import numpy as np
from jax import lax
from jax.experimental import pallas as pl
from jax.experimental.pallas import tpu as pltpu

F32 = jnp.float32
BF16 = jnp.bfloat16

GRID_W = 64
RET_HEADS = 4
ROPE_BASE = 10000.0
NA_HEADS = 16
NA_MAX_KH = 8
NA_KW = 16
CONV_W = 3
NORM_EPS = 1e-6

LANES = 128
BF16_SUBLANES = 16
VMEM_LIMIT_BYTES = 56 * 1024 * 1024

ROW_TILE = 1024
PROJ_ROW_TILE = 512
MM_SUB = 256
RET_CHUNK = 256
NA_QROWS = 4
NA_BLOCKS_PER_STEP = 8
MASK_VALUE = -1e30
LOG2E = 1.4426950408889634


def _cparams(sem):
    return pltpu.CompilerParams(dimension_semantics=sem, vmem_limit_bytes=VMEM_LIMIT_BYTES)


def _bsel(arr):
    if arr.shape[0] == 1:
        return lambda b, *_: (0, 0, 0)
    return lambda b, *_: (b, 0, 0)


def _norm_mod(x, nw, shift, scale):
    y = x * lax.rsqrt(jnp.mean(x * x, axis=-1, keepdims=True) + NORM_EPS)
    return (y * nw) * (1.0 + scale) + shift


def _silu(x):
    return x / (1.0 + jnp.exp(-x))


def _ada_kernel(c_ref, w_ref, b_ref, o_ref):
    s = _silu(c_ref[...])
    o_ref[0] = jnp.dot(s, w_ref[0], preferred_element_type=F32) + b_ref[0]


def _ada(cc, ada_w, ada_b):
    depth, d, n = ada_w.shape
    tn = 1536
    return pl.pallas_call(
        _ada_kernel,
        grid=(depth, n // tn),
        in_specs=[
            pl.BlockSpec(cc.shape, lambda i, j: (0, 0)),
            pl.BlockSpec((1, d, tn), lambda i, j: (i, 0, j)),
            pl.BlockSpec((1, 1, tn), lambda i, j: (i, 0, j)),
        ],
        out_specs=pl.BlockSpec((1, cc.shape[0], tn), lambda i, j: (i, 0, j)),
        out_shape=jax.ShapeDtypeStruct((depth, cc.shape[0], n), F32),
        compiler_params=_cparams(("parallel", "parallel")),
        name="ada",
    )(cc, ada_w, ada_b.reshape(depth, 1, n))


def _pipelined_units(units, matmul, epilogue):
    acc = matmul(units[0])
    for n, u in enumerate(units):
        nxt = matmul(units[n + 1]) if n + 1 < len(units) else None
        epilogue(acc, u)
        acc = nxt


def _proj_kernel(*refs, tn, rope_cols, scale_cols, scale_val):
    if rope_cols:
        x_ref, nw_ref, sh_ref, sc_ref, w_ref, cos_ref, sin_ref, o_ref, h_ref = refs
    else:
        x_ref, nw_ref, sh_ref, sc_ref, w_ref, o_ref, h_ref = refs
    tm = x_ref.shape[1]
    sub = min(MM_SUB, tm)

    def matmul(u):
        s, j = u
        rows = slice(s * sub, (s + 1) * sub)
        if j == 0:
            h_ref[rows] = _norm_mod(x_ref[0, rows], nw_ref[...], sh_ref[0], sc_ref[0]).astype(BF16)
        return jnp.dot(h_ref[rows], w_ref[:, j * tn:(j + 1) * tn], preferred_element_type=F32)

    def epilogue(acc, u):
        s, j = u
        rows = slice(s * sub, (s + 1) * sub)
        col = j * tn
        if scale_cols[0] <= col < scale_cols[1]:
            acc = acc * scale_val
        if col >= rope_cols:
            o_ref[0, rows, col:col + tn] = acc.astype(o_ref.dtype)
            return
        for g in range(tn // LANES):
            part = acc[:, g * LANES:(g + 1) * LANES]
            t = (g % 2) * LANES
            rot = part * cos_ref[rows, t:t + LANES] + pltpu.roll(part, LANES // 2, 1) * sin_ref[rows, t:t + LANES]
            o_ref[0, rows, col + g * LANES:col + (g + 1) * LANES] = rot.astype(o_ref.dtype)

    units = [(s, j) for s in range(tm // sub) for j in range(w_ref.shape[1] // tn)]
    _pipelined_units(units, matmul, epilogue)


def _proj(x, nw, shift, scale, w, *, tn=512, rope=None, rope_cols=0, scale_cols=(0, 0), scale_val=1.0):
    b, l, d = x.shape
    n = w.shape[1]
    tm = min(PROJ_ROW_TILE, l)
    assert n % tn == 0 and rope_cols % tn == 0 and scale_cols[0] % tn == 0 and scale_cols[1] % tn == 0
    in_specs = [
        pl.BlockSpec((1, tm, d), lambda b_, i: (b_, i, 0)),
        pl.BlockSpec((1, d), lambda b_, i: (0, 0)),
        pl.BlockSpec((1, 1, d), _bsel(shift)),
        pl.BlockSpec((1, 1, d), _bsel(scale)),
        pl.BlockSpec((d, n), lambda b_, i: (0, 0), pipeline_mode=pl.Buffered(1)),
    ]
    args = [x, nw.reshape(1, d), shift, scale, w]
    if rope is not None:
        cos, sin = rope
        in_specs += [pl.BlockSpec((tm, cos.shape[1]), lambda b_, i: (i, 0))] * 2
        args += [cos, sin]
    kern = functools.partial(_proj_kernel, tn=tn, rope_cols=(rope_cols if rope is not None else 0),
                             scale_cols=scale_cols, scale_val=scale_val)
    return pl.pallas_call(
        kern,
        grid=(b, l // tm),
        in_specs=in_specs,
        out_specs=pl.BlockSpec((1, tm, n), lambda b_, i: (b_, i, 0)),
        out_shape=jax.ShapeDtypeStruct((b, l, n), BF16),
        scratch_shapes=[pltpu.VMEM((tm, d), BF16)],
        compiler_params=_cparams(("parallel", "parallel")),
        name="proj",
    )(*args)


def _outproj_kernel(a_ref, w_ref, res_ref, gate_ref, o_ref, *, tn):
    tm = a_ref.shape[1]
    sub = min(MM_SUB, tm)

    def matmul(u):
        s, j = u
        return jnp.dot(a_ref[0, s * sub:(s + 1) * sub], w_ref[:, j * tn:(j + 1) * tn], preferred_element_type=F32)

    def epilogue(acc, u):
        s, j = u
        rows, cols = slice(s * sub, (s + 1) * sub), slice(j * tn, (j + 1) * tn)
        o_ref[0, rows, cols] = res_ref[0, rows, cols] + gate_ref[0, :, cols] * acc

    units = [(s, j) for s in range(tm // sub) for j in range(w_ref.shape[1] // tn)]
    _pipelined_units(units, matmul, epilogue)


def _outproj(a, w, res, gate, *, tn=512):
    b, l, k = a.shape
    n = w.shape[1]
    tm = min(ROW_TILE, l)
    return pl.pallas_call(
        functools.partial(_outproj_kernel, tn=tn),
        grid=(b, l // tm),
        in_specs=[
            pl.BlockSpec((1, tm, k), lambda b_, i: (b_, i, 0)),
            pl.BlockSpec((k, n), lambda b_, i: (0, 0), pipeline_mode=pl.Buffered(1)),
            pl.BlockSpec((1, tm, n), lambda b_, i: (b_, i, 0)),
            pl.BlockSpec((1, 1, n), _bsel(gate)),
        ],
        out_specs=pl.BlockSpec((1, tm, n), lambda b_, i: (b_, i, 0)),
        out_shape=jax.ShapeDtypeStruct((b, l, n), F32),
        compiler_params=_cparams(("parallel", "parallel")),
        name="outproj",
    )(a, w, res, gate)


FFN_TILE = 256
FFN_HALO = BF16_SUBLANES
FFN_SUB = 256
FFN_ROW_TILE = 512


def _ffn_kernel(x_ref, xp_ref, xn_ref, nw_ref, sh_ref, sc_ref, g_ref, wu_ref, cw_ref, cb_ref, wd_ref, fw_ref,
                o_ref, h_ref, ua_ref, ug_ref, act_ref, *, final_norm):
    i = pl.program_id(1)
    tm = x_ref.shape[1]
    hl = FFN_HALO
    tf = FFN_TILE
    f = wd_ref.shape[0]
    sub = min(FFN_SUB, tm)
    nsub = tm // sub
    chunks = [(0, sub + 2 * hl)] + [(2 * hl + c * sub, 2 * hl + (c + 1) * sub) for c in range(1, nsub)]

    def normalise(lo, hi):
        nw, sh, sc = nw_ref[...], sh_ref[0], sc_ref[0]
        keep_p = jnp.where(i > 0, F32(1.0), F32(0.0))
        keep_n = jnp.where(i < pl.num_programs(1) - 1, F32(1.0), F32(0.0))
        regions = ((0, hl, xp_ref, 0, keep_p), (hl, hl + tm, x_ref, hl, None), (hl + tm, 2 * hl + tm, xn_ref, hl + tm, keep_n))
        for r0, r1, src, off, keep in regions:
            a, b = max(lo, r0), min(hi, r1)
            if a < b:
                v = _norm_mod(src[0, a - off:b - off, :], nw, sh, sc)
                h_ref[a:b] = (v if keep is None else v * keep).astype(BF16)

    def conv(u_ref, slot, col, s):
        base = hl + s * sub
        cw = cw_ref[:, col:col + tf]
        return (u_ref[slot, base - 1:base - 1 + sub] * cw[0:1] + u_ref[slot, base:base + sub] * cw[1:2]
                + u_ref[slot, base + 1:base + 1 + sub] * cw[2:3]) + cb_ref[:, col:col + tf]

    def up(t, c):
        lo, hi = chunks[c]
        if t == 0:
            normalise(lo, hi)
        ua_ref[t % 2, lo:hi] = jnp.dot(h_ref[lo:hi], wu_ref[:, t * tf:(t + 1) * tf], preferred_element_type=F32)
        ug_ref[t % 2, lo:hi] = jnp.dot(h_ref[lo:hi], wu_ref[:, f + t * tf:f + (t + 1) * tf], preferred_element_type=F32)

    units = [(t, c) for t in range(f // tf) for c in range(nsub)]
    up(*units[0])
    for n, (t, s) in enumerate(units):
        if n + 1 < len(units):
            up(*units[n + 1])
        act = _silu(conv(ua_ref, t % 2, t * tf, s)) * conv(ug_ref, t % 2, f + t * tf, s)
        act_ref[s * sub:(s + 1) * sub, t * tf:(t + 1) * tf] = act.astype(BF16)

    for s in range(nsub):
        rows = slice(s * sub, (s + 1) * sub)
        y = x_ref[0, rows] + g_ref[0] * jnp.dot(act_ref[rows], wd_ref[...], preferred_element_type=F32)
        if final_norm:
            y = (y * lax.rsqrt(jnp.mean(y * y, axis=-1, keepdims=True) + NORM_EPS)) * fw_ref[...]
        o_ref[0, rows] = y


def _ffn(x, nw, shift, scale, gate, wu, cw, cb, wd, final_w, *, final_norm):
    b, l, d = x.shape
    f = wd.shape[0]
    tf = FFN_TILE
    assert f % tf == 0, wd.shape
    tm = min(FFN_ROW_TILE, l)
    hl = FFN_HALO
    per = tm // hl
    nblk = l // hl
    vec = lambda arr: pl.BlockSpec((1, 1, d), _bsel(arr))
    whole = lambda arr: pl.BlockSpec(arr.shape, lambda b_, i: (0, 0), pipeline_mode=pl.Buffered(1))
    return pl.pallas_call(
        functools.partial(_ffn_kernel, final_norm=final_norm),
        grid=(b, l // tm),
        in_specs=[
            pl.BlockSpec((1, tm, d), lambda b_, i: (b_, i, 0)),
            pl.BlockSpec((1, hl, d), lambda b_, i: (b_, jnp.maximum(i * per - 1, 0), 0)),
            pl.BlockSpec((1, hl, d), lambda b_, i: (b_, jnp.minimum((i + 1) * per, nblk - 1), 0)),
            pl.BlockSpec((1, d), lambda b_, i: (0, 0)),
            vec(shift), vec(scale), vec(gate),
            whole(wu), whole(cw), whole(cb), whole(wd),
            pl.BlockSpec((1, d), lambda b_, i: (0, 0)),
        ],
        out_specs=pl.BlockSpec((1, tm, d), lambda b_, i: (b_, i, 0)),
        out_shape=jax.ShapeDtypeStruct((b, l, d), F32),
        scratch_shapes=[pltpu.VMEM((tm + 2 * hl, d), BF16), pltpu.VMEM((2, tm + 2 * hl, tf), F32),
                        pltpu.VMEM((2, tm + 2 * hl, tf), F32), pltpu.VMEM((tm, f), BF16)],
        compiler_params=_cparams(("parallel", "parallel")),
        name="ffn",
    )(x, x, x, nw.reshape(1, d), shift, scale, gate, wu, cw, cb, wd, final_w.reshape(1, d))


def _ret_kernel(lg_ref, ql_ref, kl_ref, vl_ref, gl_ref, qc_ref, kc_ref, vc_ref, gc_ref, ol_ref, oc_ref, sb_ref):
    c = RET_CHUNK
    h = pl.program_id(1)
    lgf = lg_ref[0, h]
    lgb = lg_ref[1, h]
    n_ctx = qc_ref.shape[1] // c
    n_lat = ql_ref.shape[1] // c

    pos = lax.broadcasted_iota(jnp.int32, (c, 1), 0).astype(F32)
    kdec_f = jnp.exp(lgf * (c - 1.0 - pos))
    kdec_b = jnp.exp(lgb * pos)
    qdec_f = jnp.exp(lgf * (pos + 1.0))
    qdec_b = jnp.exp(lgb * (c - pos))
    one = jnp.ones((1, 1), F32)
    cdec_f = jnp.exp(lgf * c * one)
    cdec_b = jnp.exp(lgb * c * one)
    diff = (lax.broadcasted_iota(jnp.int32, (c, c), 0) - lax.broadcasted_iota(jnp.int32, (c, c), 1)).astype(F32)
    dmat = jnp.exp(jnp.where(diff >= 0, lgf, lgb) * jnp.abs(diff))

    seq = [(qc_ref, kc_ref, vc_ref, gc_ref, oc_ref, slice(t * c, (t + 1) * c)) for t in range(n_ctx)]
    seq += [(ql_ref, kl_ref, vl_ref, gl_ref, ol_ref, slice(t * c, (t + 1) * c)) for t in range(n_lat)]
    n = len(seq)

    def kv(idx, kdec):
        _, k_ref, v_ref, _, _, sl = seq[idx]
        kd = (k_ref[0, sl, :].astype(F32) * kdec).astype(BF16)
        return lax.dot_general(kd, v_ref[0, sl, :], (((0,), (0,)), ((), ())), preferred_element_type=F32)

    def qk(idx):
        q_ref, k_ref, _, _, _, sl = seq[idx]
        return lax.dot_general(q_ref[0, sl, :], k_ref[0, sl, :], (((1,), (1,)), ((), ())), preferred_element_type=F32)

    bwd_order = list(range(n_ctx - 1, -1, -1)) + list(range(n - 1, n_ctx - 1, -1))
    s = None
    for pos_, idx in enumerate(bwd_order):
        if s is not None:
            sb_ref[idx] = s.astype(BF16)
        if pos_ + 1 < n:
            upd = kv(idx, kdec_b)
            s = upd if s is None else s * cdec_b + upd

    s = None
    scores = qk(0)
    for idx in range(n):
        q_ref, _, v_ref, g_ref, o_ref, sl = seq[idx]
        qf = q_ref[0, sl, :].astype(F32)
        cross = None
        if idx != bwd_order[0]:
            cross = jnp.dot((qf * qdec_b).astype(BF16), sb_ref[idx], preferred_element_type=F32)
        if s is not None:
            cf = jnp.dot((qf * qdec_f).astype(BF16), s.astype(BF16), preferred_element_type=F32)
            cross = cf if cross is None else cross + cf
        upd = kv(idx, kdec_f) if idx + 1 < n else None
        nxt = qk(idx + 1) if idx + 1 < n else None
        o = jnp.dot((scores * dmat).astype(BF16), v_ref[0, sl, :], preferred_element_type=F32)
        if cross is not None:
            o = o + cross
        if upd is not None:
            s = upd if s is None else s * cdec_f + upd
        o = o * lax.rsqrt(jnp.mean(o * o, axis=-1, keepdims=True) + NORM_EPS)
        o_ref[0, sl, :] = (_silu(g_ref[0, sl, :].astype(F32)) * o).astype(BF16)
        scores = nxt


def _retention(lg, pl_, pc):
    b, l, n = pl_.shape
    lc = pc.shape[1]
    hh = RET_HEADS
    dk = n // (6 * hh)
    dv = 2 * dk
    c = RET_CHUNK

    def specs(length):
        return [
            pl.BlockSpec((1, length, dk), lambda b_, h: (b_, 0, h)),
            pl.BlockSpec((1, length, dk), lambda b_, h: (b_, 0, hh + h)),
            pl.BlockSpec((1, length, dv), lambda b_, h: (b_, 0, hh + h)),
            pl.BlockSpec((1, length, dv), lambda b_, h: (b_, 0, 2 * hh + h)),
        ]

    return pl.pallas_call(
        _ret_kernel,
        grid=(b, hh),
        in_specs=[pl.BlockSpec(memory_space=pltpu.SMEM)] + specs(l) + specs(lc),
        out_specs=[
            pl.BlockSpec((1, l, dv), lambda b_, h: (b_, 0, h)),
            pl.BlockSpec((1, lc, dv), lambda b_, h: (b_, 0, h)),
        ],
        out_shape=[jax.ShapeDtypeStruct((b, l, hh * dv), BF16), jax.ShapeDtypeStruct((b, lc, hh * dv), BF16)],
        scratch_shapes=[pltpu.VMEM((lc // c + l // c, dk, dv), BF16)],
        compiler_params=_cparams(("parallel", "arbitrary")),
        name="retention",
    )(lg, pl_, pl_, pl_, pl_, pc, pc, pc, pc)


def _rope_tables(seq, dk):
    quarter = dk // 4
    t = jnp.arange(seq)
    pos_r = (t // GRID_W).astype(F32)
    pos_c = (t % GRID_W).astype(F32)
    inv = ROPE_BASE ** (-jnp.arange(quarter, dtype=F32) / quarter)
    ang_r = pos_r[:, None] * inv[None, :]
    ang_c = pos_c[:, None] * inv[None, :]
    cos = jnp.concatenate([jnp.cos(ang_r)] * 2 + [jnp.cos(ang_c)] * 2, axis=-1)
    sin = jnp.concatenate([-jnp.sin(ang_r), jnp.sin(ang_r), -jnp.sin(ang_c), jnp.sin(ang_c)], axis=-1)
    return cos, sin


def _na_window(rows):
    kh = min(NA_MAX_KH, rows)
    return kh, NA_QROWS + kh - 1


def _na_kernel(m_ref, q_ref, k_ref, v_ref, kc_ref, vc_ref, o_ref, bias_ref, sl_ref, sc_ref, pl_ref, pc_ref, *, rows):
    b = pl.program_id(1)
    rb = pl.program_id(2)
    nrb = pl.num_programs(2)
    kh, wrows = _na_window(rows)
    qr_n = NA_QROWS
    w = GRID_W
    hd = q_ref.shape[2] // 2

    @pl.when((b == 0) & (rb == 0))
    def _():
        qc = lax.broadcasted_iota(jnp.int32, (w, LANES), 0)
        kc = lax.broadcasted_iota(jnp.int32, (w, LANES), 1)
        cs = jnp.clip(qc - NA_KW // 2, 0, w - NA_KW)
        col_valid = (kc >= cs) & (kc < cs + NA_KW)
        toe = [[jnp.where(col_valid,
                          pltpu.roll(jnp.broadcast_to(m_ref[e, dr:dr + 1, :], (w, LANES)), 0, 1, stride=1, stride_axis=0),
                          MASK_VALUE)[:, :w]
                for dr in range(2 * kh - 1)] for e in range(2)]
        for variant in range(3):
            r0 = (0, qr_n, rows - qr_n)[variant]
            ws = min(max(r0 - kh // 2, 0), rows - wrows)
            for qr in range(qr_n):
                r = r0 + qr
                rs = min(max(r - kh // 2, 0), rows - kh)
                for krr in range(wrows):
                    kr = ws + krr
                    for e in range(2):
                        dst = (e, variant, slice(qr * w, (qr + 1) * w), slice(krr * w, (krr + 1) * w))
                        if rs <= kr < rs + kh:
                            bias_ref[dst] = toe[e][kr - r + kh - 1]
                        else:
                            bias_ref[dst] = jnp.full((w, w), MASK_VALUE, F32)

    kc = kc_ref[0]
    vc = vc_ref[0]
    nq = qr_n * w
    lane = lax.broadcasted_iota(jnp.int32, (nq, q_ref.shape[2]), 1)
    nblk = NA_BLOCKS_PER_STEP
    trans_b = (((1,), (1,)), ((), ()))

    chains = []
    for k in range(nblk):
        blk = rb * nblk + k
        variant = jnp.where(blk == 0, 0, jnp.where(blk == nrb * nblk - 1, 2, 1))
        ws = jnp.clip(blk * qr_n - kh // 2, 0, rows - wrows)
        win = pl.ds(pl.multiple_of(ws * w, w), wrows * w)
        kwin = k_ref[0, win, :]
        q = q_ref[0, k * nq:(k + 1) * nq, :]
        for e in range(2):
            qe = jnp.where((lane >= e * hd) & (lane < (e + 1) * hd), q, jnp.zeros_like(q))
            ci = len(chains)
            sl_ref[ci] = lax.dot_general(qe, kwin, trans_b, preferred_element_type=F32) + bias_ref[e, variant]
            sc_ref[ci] = lax.dot_general(qe, kc, trans_b, preferred_element_type=F32)
            chains.append((k, e, win, ci))

    def head_values(v, e):
        vlane = lax.broadcasted_iota(jnp.int32, v.shape, 1)
        return jnp.where((vlane >= e * hd) & (vlane < (e + 1) * hd), v, jnp.ones_like(v))

    vc_e = [head_values(vc, e) for e in range(2)]
    outs = {}
    for k, e, win, ci in chains:
        m = jnp.maximum(jnp.max(sl_ref[ci], axis=-1, keepdims=True), jnp.max(sc_ref[ci], axis=-1, keepdims=True))
        pl_ref[ci] = jnp.exp2(sl_ref[ci] - m).astype(BF16)
        pc_ref[ci] = jnp.exp2(sc_ref[ci] - m).astype(BF16)
        o = jnp.dot(pl_ref[ci], head_values(v_ref[0, win, :], e), preferred_element_type=F32)
        o += jnp.dot(pc_ref[ci], vc_e[e], preferred_element_type=F32)
        outs[k, e] = o / pltpu.roll(o, hd, 1)
        if e == 1:
            o_ref[0, k * nq:(k + 1) * nq, :] = jnp.where(lane < hd, outs[k, 0], outs[k, 1]).astype(o_ref.dtype)


def _na_bias_table(rpb):
    padded = jnp.pad(rpb * LOG2E, ((0, 0), (0, 0), (0, LANES - rpb.shape[2])))
    return jnp.roll(padded, -(NA_KW - 1), axis=2)


def _na_attention(qkv, kvc, rpb):
    b, s, d3 = qkv.shape
    d = d3 // 3
    lc = kvc.shape[1]
    rows = s // GRID_W
    pairs = d // LANES
    kh, wrows = _na_window(rows)
    qrows = NA_QROWS * GRID_W
    srows = NA_BLOCKS_PER_STEP * qrows
    assert kh == 2 * NA_QROWS and rows % (NA_QROWS * NA_BLOCKS_PER_STEP) == 0 and rows >= wrows + NA_QROWS, rows
    mtab = _na_bias_table(rpb)
    return pl.pallas_call(
        functools.partial(_na_kernel, rows=rows),
        grid=(pairs, b, s // srows),
        in_specs=[
            pl.BlockSpec((2,) + mtab.shape[1:], lambda p, b_, r: (p, 0, 0)),
            pl.BlockSpec((1, srows, LANES), lambda p, b_, r: (b_, r, p)),
            pl.BlockSpec((1, s, LANES), lambda p, b_, r: (b_, 0, pairs + p)),
            pl.BlockSpec((1, s, LANES), lambda p, b_, r: (b_, 0, 2 * pairs + p)),
            pl.BlockSpec((1, lc, LANES), lambda p, b_, r: (b_, 0, p)),
            pl.BlockSpec((1, lc, LANES), lambda p, b_, r: (b_, 0, pairs + p)),
        ],
        out_specs=pl.BlockSpec((1, srows, LANES), lambda p, b_, r: (b_, r, p)),
        out_shape=jax.ShapeDtypeStruct((b, s, d), BF16),
        scratch_shapes=[pltpu.VMEM((2, 3, qrows, wrows * GRID_W), F32),
                        pltpu.VMEM((2 * NA_BLOCKS_PER_STEP, qrows, wrows * GRID_W), F32),
                        pltpu.VMEM((2 * NA_BLOCKS_PER_STEP, qrows, lc), F32),
                        pltpu.VMEM((2 * NA_BLOCKS_PER_STEP, qrows, wrows * GRID_W), BF16),
                        pltpu.VMEM((2 * NA_BLOCKS_PER_STEP, qrows, lc), BF16)],
        compiler_params=_cparams(("arbitrary", "arbitrary", "arbitrary")),
        name="na_attention",
    )(mtab, qkv, qkv, qkv, kvc, kvc)


def kernel(x, c, ctx, c_ctx, ada_w, ada_b, norm1_w, norm2_w, ret_w_in, ret_decay_fwd, ret_decay_bwd, ret_w_out,
           na_w_qkv, na_rpb, na_w_out, ffn_w_up, ffn_conv_w, ffn_conv_b, ffn_w_down, final_norm_w):
    b, seq, d = x.shape
    depth = ada_w.shape[0]
    n_mixers = 2

    pad = BF16_SUBLANES - (b + 1) % BF16_SUBLANES
    cc = jnp.concatenate([c, c_ctx[None, :], jnp.zeros((pad, d), F32)], axis=0)
    mods = _ada(cc, ada_w, ada_b).reshape(depth, cc.shape[0], 6, d)

    def mod(i, k):
        return mods[i, :b, k][:, None, :], mods[i, b:b + 1, k][:, None, :]

    for i in range(depth):
        last = i == depth - 1
        j = i // n_mixers
        (sh1, csh1), (sc1, csc1), (g1, cg1) = mod(i, 0), mod(i, 1), mod(i, 2)
        (sh2, csh2), (sc2, csc2), (g2, cg2) = mod(i, 3), mod(i, 4), mod(i, 5)
        if i % n_mixers == 0:
            hk = ret_w_in.shape[2] // 6
            dk = hk // RET_HEADS
            w_in = ret_w_in[j].astype(BF16)
            kw = dict(scale_cols=(hk, 2 * hk), scale_val=dk ** -0.5)
            pl_ = _proj(x, norm1_w[i], sh1, sc1, w_in, rope=_rope_tables(seq, dk), rope_cols=2 * hk, **kw)
            pc = _proj(ctx, norm1_w[i], csh1, csc1, w_in, **kw)
            lg = jnp.stack([-jnp.exp(ret_decay_fwd[j].astype(F32)), -jnp.exp(ret_decay_bwd[j].astype(F32))])
            y, yc = _retention(lg, pl_, pc)
            w_out = ret_w_out[j].astype(BF16)
        else:
            w_qkv = na_w_qkv[j].astype(BF16)
            qkv = _proj(x, norm1_w[i], sh1, sc1, w_qkv, scale_cols=(0, d), scale_val=(d // NA_HEADS) ** -0.5 * LOG2E)
            kvc = _proj(ctx, norm1_w[i], csh1, csc1, w_qkv[:, d:])
            if not last:
                raise NotImplementedError("context output of the attention mixer is only needed for depth > 2")
            y, yc = _na_attention(qkv, kvc, na_rpb[j]), None
            w_out = na_w_out[j].astype(BF16)
        ffn_w = (ffn_w_up[i].astype(BF16), ffn_conv_w[i], ffn_conv_b[i][None, :], ffn_w_down[i].astype(BF16))
        x = _outproj(y, w_out, x, g1)
        x = _ffn(x, norm2_w[i], sh2, sc2, g2, *ffn_w, final_norm_w, final_norm=last)
        if not last:
            ctx = _outproj(yc, w_out, ctx, cg1)
            ctx = _ffn(ctx, norm2_w[i], csh2, csc2, cg2, *ffn_w, final_norm_w, final_norm=False)
    return x
```

```python
import functools

import jax
import jax.numpy as jnp
import numpy as np
from jax import lax
from jax.experimental import pallas as pl
from jax.experimental.pallas import tpu as pltpu

F32 = jnp.float32
BF16 = jnp.bfloat16

GRID_W = 64
RET_HEADS = 4
ROPE_BASE = 10000.0
NA_HEADS = 16
NA_MAX_KH = 8
NA_KW = 16
CONV_W = 3
NORM_EPS = 1e-6

LANES = 128
BF16_SUBLANES = 16
VMEM_LIMIT_BYTES = 56 * 1024 * 1024

ROW_TILE = 1024
PROJ_ROW_TILE = 512
MM_SUB = 256
RET_CHUNK = 256
NA_QROWS = 4
MASK_VALUE = -1e30
LOG2E = 1.4426950408889634


def _cparams(sem):
    return pltpu.CompilerParams(dimension_semantics=sem, vmem_limit_bytes=VMEM_LIMIT_BYTES)


def _bsel(arr):
    if arr.shape[0] == 1:
        return lambda b, *_: (0, 0, 0)
    return lambda b, *_: (b, 0, 0)


def _norm_mod(x, nw, shift, scale):
    y = x * lax.rsqrt(jnp.mean(x * x, axis=-1, keepdims=True) + NORM_EPS)
    return (y * nw) * (1.0 + scale) + shift


def _silu(x):
    return x / (1.0 + jnp.exp(-x))


def _ada_kernel(c_ref, w_ref, b_ref, o_ref):
    s = _silu(c_ref[...])
    o_ref[0] = jnp.dot(s, w_ref[0], preferred_element_type=F32) + b_ref[0]


def _ada(cc, ada_w, ada_b):
    depth, d, n = ada_w.shape
    tn = 1536
    return pl.pallas_call(
        _ada_kernel,
        grid=(depth, n // tn),
        in_specs=[
            pl.BlockSpec(cc.shape, lambda i, j: (0, 0)),
            pl.BlockSpec((1, d, tn), lambda i, j: (i, 0, j)),
            pl.BlockSpec((1, 1, tn), lambda i, j: (i, 0, j)),
        ],
        out_specs=pl.BlockSpec((1, cc.shape[0], tn), lambda i, j: (i, 0, j)),
        out_shape=jax.ShapeDtypeStruct((depth, cc.shape[0], n), F32),
        compiler_params=_cparams(("parallel", "parallel")),
        name="ada",
    )(cc, ada_w, ada_b.reshape(depth, 1, n))


def _pipelined_units(units, matmul, epilogue):
    acc = matmul(units[0])
    for n, u in enumerate(units):
        nxt = matmul(units[n + 1]) if n + 1 < len(units) else None
        epilogue(acc, u)
        acc = nxt


def _proj_kernel(*refs, tn, rope_cols, scale_cols, scale_val, silu_cols):
    if rope_cols:
        x_ref, nw_ref, sh_ref, sc_ref, w_ref, cos_ref, sin_ref, o_ref, h_ref = refs
    else:
        x_ref, nw_ref, sh_ref, sc_ref, w_ref, o_ref, h_ref = refs
    tm = x_ref.shape[1]
    sub = min(MM_SUB, tm)

    def matmul(u):
        s, j = u
        rows = slice(s * sub, (s + 1) * sub)
        if j == 0:
            h_ref[rows] = _norm_mod(x_ref[0, rows], nw_ref[...], sh_ref[0], sc_ref[0]).astype(BF16)
        return jnp.dot(h_ref[rows], w_ref[:, j * tn:(j + 1) * tn], preferred_element_type=F32)

    def epilogue(acc, u):
        s, j = u
        rows = slice(s * sub, (s + 1) * sub)
        col = j * tn
        if scale_cols[0] <= col < scale_cols[1]:
            acc = acc * scale_val
        if silu_cols[0] <= col < silu_cols[1]:
            acc = _silu(acc)
        if col >= rope_cols:
            o_ref[0, rows, col:col + tn] = acc.astype(o_ref.dtype)
            return
        for g in range(tn // LANES):
            part = acc[:, g * LANES:(g + 1) * LANES]
            t = (g % 2) * LANES
            rot = part * cos_ref[rows, t:t + LANES] + pltpu.roll(part, LANES // 2, 1) * sin_ref[rows, t:t + LANES]
            o_ref[0, rows, col + g * LANES:col + (g + 1) * LANES] = rot.astype(o_ref.dtype)

    units = [(s, j) for s in range(tm // sub) for j in range(w_ref.shape[1] // tn)]
    _pipelined_units(units, matmul, epilogue)


def _proj(x, nw, shift, scale, w, *, tn=512, rope=None, rope_cols=0, scale_cols=(0, 0), scale_val=1.0, silu_cols=(0, 0)):
    b, l, d = x.shape
    n = w.shape[1]
    tm = min(PROJ_ROW_TILE, l)
    assert all(v % tn == 0 for v in (n, rope_cols) + tuple(scale_cols) + tuple(silu_cols))
    in_specs = [
        pl.BlockSpec((1, tm, d), lambda b_, i: (b_, i, 0)),
        pl.BlockSpec((1, d), lambda b_, i: (0, 0)),
        pl.BlockSpec((1, 1, d), _bsel(shift)),
        pl.BlockSpec((1, 1, d), _bsel(scale)),
        pl.BlockSpec((d, n), lambda b_, i: (0, 0), pipeline_mode=pl.Buffered(1)),
    ]
    args = [x, nw.reshape(1, d), shift, scale, w]
    if rope is not None:
        cos, sin = rope
        in_specs += [pl.BlockSpec((tm, cos.shape[1]), lambda b_, i: (i, 0))] * 2
        args += [cos, sin]
    kern = functools.partial(_proj_kernel, tn=tn, rope_cols=(rope_cols if rope is not None else 0),
                             scale_cols=scale_cols, scale_val=scale_val, silu_cols=silu_cols)
    return pl.pallas_call(
        kern,
        grid=(b, l // tm),
        in_specs=in_specs,
        out_specs=pl.BlockSpec((1, tm, n), lambda b_, i: (b_, i, 0)),
        out_shape=jax.ShapeDtypeStruct((b, l, n), BF16),
        scratch_shapes=[pltpu.VMEM((tm, d), BF16)],
        compiler_params=_cparams(("parallel", "parallel")),
        name="proj",
    )(*args)


def _outproj_kernel(a_ref, w_ref, res_ref, gate_ref, o_ref, *, tn):
    tm = a_ref.shape[1]
    sub = min(MM_SUB, tm)

    def matmul(u):
        s, j = u
        return jnp.dot(a_ref[0, s * sub:(s + 1) * sub], w_ref[:, j * tn:(j + 1) * tn], preferred_element_type=F32)

    def epilogue(acc, u):
        s, j = u
        rows, cols = slice(s * sub, (s + 1) * sub), slice(j * tn, (j + 1) * tn)
        o_ref[0, rows, cols] = res_ref[0, rows, cols] + gate_ref[0, :, cols] * acc

    units = [(s, j) for s in range(tm // sub) for j in range(w_ref.shape[1] // tn)]
    _pipelined_units(units, matmul, epilogue)


def _outproj(a, w, res, gate, *, tn=512):
    b, l, k = a.shape
    n = w.shape[1]
    tm = min(ROW_TILE, l)
    return pl.pallas_call(
        functools.partial(_outproj_kernel, tn=tn),
        grid=(b, l // tm),
        in_specs=[
            pl.BlockSpec((1, tm, k), lambda b_, i: (b_, i, 0)),
            pl.BlockSpec((k, n), lambda b_, i: (0, 0), pipeline_mode=pl.Buffered(1)),
            pl.BlockSpec((1, tm, n), lambda b_, i: (b_, i, 0)),
            pl.BlockSpec((1, 1, n), _bsel(gate)),
        ],
        out_specs=pl.BlockSpec((1, tm, n), lambda b_, i: (b_, i, 0)),
        out_shape=jax.ShapeDtypeStruct((b, l, n), F32),
        compiler_params=_cparams(("parallel", "parallel")),
        name="outproj",
    )(a, w, res, gate)


FFN_TILE = 256
FFN_HALO = BF16_SUBLANES
FFN_SUB = 256
FFN_ROW_TILE = 512


def _ffn_kernel(x_ref, xp_ref, xn_ref, nw_ref, sh_ref, sc_ref, g_ref, wu_ref, cw_ref, cb_ref, wd_ref, fw_ref,
                o_ref, h_ref, ua_ref, ug_ref, act_ref, *, final_norm):
    i = pl.program_id(1)
    tm = x_ref.shape[1]
    hl = FFN_HALO
    tf = FFN_TILE
    f = wd_ref.shape[0]
    sub = min(FFN_SUB, tm)
    nsub = tm // sub
    chunks = [(0, sub + 2 * hl)] + [(2 * hl + c * sub, 2 * hl + (c + 1) * sub) for c in range(1, nsub)]

    def normalise(lo, hi):
        nw, sh, sc = nw_ref[...], sh_ref[0], sc_ref[0]
        keep_p = jnp.where(i > 0, F32(1.0), F32(0.0))
        keep_n = jnp.where(i < pl.num_programs(1) - 1, F32(1.0), F32(0.0))
        regions = ((0, hl, xp_ref, 0, keep_p), (hl, hl + tm, x_ref, hl, None), (hl + tm, 2 * hl + tm, xn_ref, hl + tm, keep_n))
        for r0, r1, src, off, keep in regions:
            a, b = max(lo, r0), min(hi, r1)
            if a < b:
                v = _norm_mod(src[0, a - off:b - off, :], nw, sh, sc)
                h_ref[a:b] = (v if keep is None else v * keep).astype(BF16)

    def conv(u_ref, slot, col, s):
        base = hl + s * sub
        pad = 8
        cw = cw_ref[:, col:col + tf]
        blk = u_ref[slot, base - pad:base + sub + pad]
        prev = pltpu.roll(blk, 1, 0)[pad:pad + sub]
        nxt = pltpu.roll(blk, sub + 2 * pad - 1, 0)[pad:pad + sub]
        return (prev * cw[0:1] + blk[pad:pad + sub] * cw[1:2] + nxt * cw[2:3]) + cb_ref[:, col:col + tf]

    def up(t, c):
        lo, hi = chunks[c]
        if t == 0:
            normalise(lo, hi)
        ua_ref[t % 2, lo:hi] = jnp.dot(h_ref[lo:hi], wu_ref[:, t * tf:(t + 1) * tf], preferred_element_type=F32)
        ug_ref[t % 2, lo:hi] = jnp.dot(h_ref[lo:hi], wu_ref[:, f + t * tf:f + (t + 1) * tf], preferred_element_type=F32)

    units = [(t, c) for t in range(f // tf) for c in range(nsub)]
    up(*units[0])
    for n, (t, s) in enumerate(units):
        if n + 1 < len(units):
            up(*units[n + 1])
        act = _silu(conv(ua_ref, t % 2, t * tf, s)) * conv(ug_ref, t % 2, f + t * tf, s)
        act_ref[s * sub:(s + 1) * sub, t * tf:(t + 1) * tf] = act.astype(BF16)

    for s in range(nsub):
        rows = slice(s * sub, (s + 1) * sub)
        y = x_ref[0, rows] + g_ref[0] * jnp.dot(act_ref[rows], wd_ref[...], preferred_element_type=F32)
        if final_norm:
            y = (y * lax.rsqrt(jnp.mean(y * y, axis=-1, keepdims=True) + NORM_EPS)) * fw_ref[...]
        o_ref[0, rows] = y


def _ffn(x, nw, shift, scale, gate, wu, cw, cb, wd, final_w, *, final_norm):
    b, l, d = x.shape
    f = wd.shape[0]
    tf = FFN_TILE
    assert f % tf == 0, wd.shape
    tm = min(FFN_ROW_TILE, l)
    hl = FFN_HALO
    per = tm // hl
    nblk = l // hl
    vec = lambda arr: pl.BlockSpec((1, 1, d), _bsel(arr))
    whole = lambda arr: pl.BlockSpec(arr.shape, lambda b_, i: (0, 0), pipeline_mode=pl.Buffered(1))
    return pl.pallas_call(
        functools.partial(_ffn_kernel, final_norm=final_norm),
        grid=(b, l // tm),
        in_specs=[
            pl.BlockSpec((1, tm, d), lambda b_, i: (b_, i, 0)),
            pl.BlockSpec((1, hl, d), lambda b_, i: (b_, jnp.maximum(i * per - 1, 0), 0)),
            pl.BlockSpec((1, hl, d), lambda b_, i: (b_, jnp.minimum((i + 1) * per, nblk - 1), 0)),
            pl.BlockSpec((1, d), lambda b_, i: (0, 0)),
            vec(shift), vec(scale), vec(gate),
            whole(wu), whole(cw), whole(cb), whole(wd),
            pl.BlockSpec((1, d), lambda b_, i: (0, 0)),
        ],
        out_specs=pl.BlockSpec((1, tm, d), lambda b_, i: (b_, i, 0)),
        out_shape=jax.ShapeDtypeStruct((b, l, d), F32),
        scratch_shapes=[pltpu.VMEM((tm + 2 * hl, d), BF16), pltpu.VMEM((2, tm + 2 * hl, tf), F32),
                        pltpu.VMEM((2, tm + 2 * hl, tf), F32), pltpu.VMEM((tm, f), BF16)],
        compiler_params=_cparams(("parallel", "parallel")),
        name="ffn",
    )(x, x, x, nw.reshape(1, d), shift, scale, gate, wu, cw, cb, wd, final_w.reshape(1, d))


def _ret_kernel(lg_ref, ql_ref, kl_ref, vl_ref, gl_ref, qc_ref, kc_ref, vc_ref, gc_ref, ol_ref, oc_ref, sb_ref):
    c = RET_CHUNK
    h = pl.program_id(1)
    lgf = lg_ref[0, h]
    lgb = lg_ref[1, h]
    n_ctx = qc_ref.shape[1] // c
    n_lat = ql_ref.shape[1] // c

    pos = lax.broadcasted_iota(jnp.int32, (c, 1), 0).astype(F32)
    kdec_f = jnp.exp(lgf * (c - 1.0 - pos))
    kdec_b = jnp.exp(lgb * pos)
    qdec_f = jnp.exp(lgf * (pos + 1.0))
    qdec_b = jnp.exp(lgb * (c - pos))
    one = jnp.ones((1, 1), F32)
    cdec_f = jnp.exp(lgf * c * one)
    cdec_b = jnp.exp(lgb * c * one)
    diff = (lax.broadcasted_iota(jnp.int32, (c, c), 0) - lax.broadcasted_iota(jnp.int32, (c, c), 1)).astype(F32)
    dmat = jnp.exp(jnp.where(diff >= 0, lgf, lgb) * jnp.abs(diff))

    seq = [(qc_ref, kc_ref, vc_ref, gc_ref, oc_ref, slice(t * c, (t + 1) * c)) for t in range(n_ctx)]
    seq += [(ql_ref, kl_ref, vl_ref, gl_ref, ol_ref, slice(t * c, (t + 1) * c)) for t in range(n_lat)]
    n = len(seq)

    def kv(idx, kdec):
        _, k_ref, v_ref, _, _, sl = seq[idx]
        kd = (k_ref[0, sl, :].astype(F32) * kdec).astype(BF16)
        return lax.dot_general(kd, v_ref[0, sl, :], (((0,), (0,)), ((), ())), preferred_element_type=F32)

    def qk(idx):
        q_ref, k_ref, _, _, _, sl = seq[idx]
        return lax.dot_general(q_ref[0, sl, :], k_ref[0, sl, :], (((1,), (1,)), ((), ())), preferred_element_type=F32)

    bwd_order = list(range(n_ctx - 1, -1, -1)) + list(range(n - 1, n_ctx - 1, -1))
    s = None
    for pos_, idx in enumerate(bwd_order):
        if s is not None:
            sb_ref[idx] = s.astype(BF16)
        if pos_ + 1 < n:
            upd = kv(idx, kdec_b)
            s = upd if s is None else s * cdec_b + upd

    s = None
    scores = qk(0)
    for idx in range(n):
        q_ref, _, v_ref, g_ref, o_ref, sl = seq[idx]
        qf = q_ref[0, sl, :].astype(F32)
        cross = None
        if idx != bwd_order[0]:
            cross = jnp.dot((qf * qdec_b).astype(BF16), sb_ref[idx], preferred_element_type=F32)
        if s is not None:
            cf = jnp.dot((qf * qdec_f).astype(BF16), s.astype(BF16), preferred_element_type=F32)
            cross = cf if cross is None else cross + cf
        upd = kv(idx, kdec_f) if idx + 1 < n else None
        nxt = qk(idx + 1) if idx + 1 < n else None
        o = jnp.dot((scores * dmat).astype(BF16), v_ref[0, sl, :], preferred_element_type=F32)
        if cross is not None:
            o = o + cross
        if upd is not None:
            s = upd if s is None else s * cdec_f + upd
        o = o * lax.rsqrt(jnp.mean(o * o, axis=-1, keepdims=True) + NORM_EPS)
        o_ref[0, sl, :] = (g_ref[0, sl, :].astype(F32) * o).astype(BF16)
        scores = nxt


def _retention(lg, pl_, pc):
    b, l, n = pl_.shape
    lc = pc.shape[1]
    hh = RET_HEADS
    dk = n // (6 * hh)
    dv = 2 * dk
    c = RET_CHUNK

    def specs(length):
        return [
            pl.BlockSpec((1, length, dk), lambda b_, h: (b_, 0, h)),
            pl.BlockSpec((1, length, dk), lambda b_, h: (b_, 0, hh + h)),
            pl.BlockSpec((1, length, dv), lambda b_, h: (b_, 0, hh + h)),
            pl.BlockSpec((1, length, dv), lambda b_, h: (b_, 0, 2 * hh + h)),
        ]

    return pl.pallas_call(
        _ret_kernel,
        grid=(b, hh),
        in_specs=[pl.BlockSpec(memory_space=pltpu.SMEM)] + specs(l) + specs(lc),
        out_specs=[
            pl.BlockSpec((1, l, dv), lambda b_, h: (b_, 0, h)),
            pl.BlockSpec((1, lc, dv), lambda b_, h: (b_, 0, h)),
        ],
        out_shape=[jax.ShapeDtypeStruct((b, l, hh * dv), BF16), jax.ShapeDtypeStruct((b, lc, hh * dv), BF16)],
        scratch_shapes=[pltpu.VMEM((lc // c + l // c, dk, dv), BF16)],
        compiler_params=_cparams(("parallel", "arbitrary")),
        name="retention",
    )(lg, pl_, pl_, pl_, pl_, pc, pc, pc, pc)


def _rope_tables(seq, dk):
    quarter = dk // 4
    t = jnp.arange(seq)
    pos_r = (t // GRID_W).astype(F32)
    pos_c = (t % GRID_W).astype(F32)
    inv = ROPE_BASE ** (-jnp.arange(quarter, dtype=F32) / quarter)
    ang_r = pos_r[:, None] * inv[None, :]
    ang_c = pos_c[:, None] * inv[None, :]
    cos = jnp.concatenate([jnp.cos(ang_r)] * 2 + [jnp.cos(ang_c)] * 2, axis=-1)
    sin = jnp.concatenate([-jnp.sin(ang_r), jnp.sin(ang_r), -jnp.sin(ang_c), jnp.sin(ang_c)], axis=-1)
    return cos, sin


def _na_window(rows):
    kh = min(NA_MAX_KH, rows)
    return kh, NA_QROWS + kh


def _na_window_start(blk, rows):
    kh, wrows = _na_window(rows)
    return min(max(blk * NA_QROWS - kh // 2, 0), rows - wrows)


def _na_kernel(m_ref, q_ref, k_ref, v_ref, kc_ref, vc_ref, o_ref, bias_ref, sl_ref, sc_ref, pl_ref, pc_ref, *, rows):
    b = pl.program_id(1)
    kh, wrows = _na_window(rows)
    qr_n = NA_QROWS
    w = GRID_W
    hd = q_ref.shape[2] // 2
    nblk = rows // qr_n

    @pl.when(b == 0)
    def _():
        qc = lax.broadcasted_iota(jnp.int32, (w, LANES), 0)
        kc = lax.broadcasted_iota(jnp.int32, (w, LANES), 1)
        cs = jnp.clip(qc - NA_KW // 2, 0, w - NA_KW)
        col_valid = (kc >= cs) & (kc < cs + NA_KW)
        toe = [[jnp.where(col_valid,
                          pltpu.roll(jnp.broadcast_to(m_ref[e, dr:dr + 1, :], (w, LANES)), 0, 1, stride=1, stride_axis=0),
                          MASK_VALUE)[:, :w]
                for dr in range(2 * kh - 1)] for e in range(2)]
        for variant in range(3):
            r0 = (0, qr_n, rows - qr_n)[variant]
            ws = _na_window_start(r0 // qr_n, rows)
            for qr in range(qr_n):
                r = r0 + qr
                rs = min(max(r - kh // 2, 0), rows - kh)
                for krr in range(wrows):
                    kr = ws + krr
                    for e in range(2):
                        dst = (e, variant, slice(qr * w, (qr + 1) * w), slice(krr * w, (krr + 1) * w))
                        if rs <= kr < rs + kh:
                            bias_ref[dst] = toe[e][kr - r + kh - 1]
                        else:
                            bias_ref[dst] = jnp.full((w, w), MASK_VALUE, F32)

    vc = vc_ref[0]
    nq = qr_n * w
    lane = lax.broadcasted_iota(jnp.int32, (nq, q_ref.shape[2]), 1)

    kc = kc_ref[0]
    trans_b = (((1,), (1,)), ((), ()))

    chains = []
    for k in range(nblk):
        variant = 0 if k == 0 else (2 if k == nblk - 1 else 1)
        ws = _na_window_start(k, rows)
        win = slice(ws * w, (ws + wrows) * w)
        q = q_ref[0, k * nq:(k + 1) * nq, :]
        for e in range(2):
            qe = jnp.where((lane >= e * hd) & (lane < (e + 1) * hd), q, jnp.zeros_like(q))
            ci = len(chains)
            sl_ref[ci] = lax.dot_general(qe, k_ref[0, win, :], trans_b, preferred_element_type=F32) + bias_ref[e, variant]
            sc_ref[ci] = lax.dot_general(qe, kc, trans_b, preferred_element_type=F32)
            chains.append((k, e, win, ci))

    def head_values(v, e):
        vlane = lax.broadcasted_iota(jnp.int32, v.shape, 1)
        return jnp.where((vlane >= e * hd) & (vlane < (e + 1) * hd), v, jnp.ones_like(v))

    vc_e = [head_values(vc, e) for e in range(2)]
    outs = {}
    for k, e, win, ci in chains:
        m = jnp.maximum(jnp.max(sl_ref[ci], axis=-1, keepdims=True), jnp.max(sc_ref[ci], axis=-1, keepdims=True))
        pl_ref[ci] = jnp.exp2(sl_ref[ci] - m).astype(BF16)
        pc_ref[ci] = jnp.exp2(sc_ref[ci] - m).astype(BF16)
        o = jnp.dot(pl_ref[ci], head_values(v_ref[0, win, :], e), preferred_element_type=F32)
        o += jnp.dot(pc_ref[ci], vc_e[e], preferred_element_type=F32)
        outs[k, e] = o / pltpu.roll(o, hd, 1)
        if e == 1:
            o_ref[0, k * nq:(k + 1) * nq, :] = jnp.where(lane < hd, outs[k, 0], outs[k, 1]).astype(o_ref.dtype)


def _na_bias_table(rpb):
    padded = jnp.pad(rpb * LOG2E, ((0, 0), (0, 0), (0, LANES - rpb.shape[2])))
    return jnp.roll(padded, -(NA_KW - 1), axis=2)


def _na_attention(qkv, kvc, rpb):
    b, s, d3 = qkv.shape
    d = d3 // 3
    lc = kvc.shape[1]
    rows = s // GRID_W
    pairs = d // LANES
    kh, wrows = _na_window(rows)
    qrows = NA_QROWS * GRID_W
    nchain = 2 * (rows // NA_QROWS)
    assert kh == 2 * NA_QROWS and rows % NA_QROWS == 0 and rows >= wrows + NA_QROWS, rows
    mtab = _na_bias_table(rpb)
    return pl.pallas_call(
        functools.partial(_na_kernel, rows=rows),
        grid=(pairs, b),
        in_specs=[
            pl.BlockSpec((2,) + mtab.shape[1:], lambda p, b_: (p, 0, 0)),
            pl.BlockSpec((1, s, LANES), lambda p, b_: (b_, 0, p)),
            pl.BlockSpec((1, s, LANES), lambda p, b_: (b_, 0, pairs + p)),
            pl.BlockSpec((1, s, LANES), lambda p, b_: (b_, 0, 2 * pairs + p)),
            pl.BlockSpec((1, lc, LANES), lambda p, b_: (b_, 0, p)),
            pl.BlockSpec((1, lc, LANES), lambda p, b_: (b_, 0, pairs + p)),
        ],
        out_specs=pl.BlockSpec((1, s, LANES), lambda p, b_: (b_, 0, p)),
        out_shape=jax.ShapeDtypeStruct((b, s, d), BF16),
        scratch_shapes=[pltpu.VMEM((2, 3, qrows, wrows * GRID_W), F32),
                        pltpu.VMEM((nchain, qrows, wrows * GRID_W), F32), pltpu.VMEM((nchain, qrows, lc), F32),
                        pltpu.VMEM((nchain, qrows, wrows * GRID_W), BF16), pltpu.VMEM((nchain, qrows, lc), BF16)],
        compiler_params=_cparams(("arbitrary", "arbitrary")),
        name="na_attention",
    )(mtab, qkv, qkv, qkv, kvc, kvc)


def kernel(x, c, ctx, c_ctx, ada_w, ada_b, norm1_w, norm2_w, ret_w_in, ret_decay_fwd, ret_decay_bwd, ret_w_out,
           na_w_qkv, na_rpb, na_w_out, ffn_w_up, ffn_conv_w, ffn_conv_b, ffn_w_down, final_norm_w):
    b, seq, d = x.shape
    depth = ada_w.shape[0]
    n_mixers = 2

    pad = BF16_SUBLANES - (b + 1) % BF16_SUBLANES
    cc = jnp.concatenate([c, c_ctx[None, :], jnp.zeros((pad, d), F32)], axis=0)
    mods = _ada(cc, ada_w, ada_b).reshape(depth, cc.shape[0], 6, d)

    def mod(i, k):
        return mods[i, :b, k][:, None, :], mods[i, b:b + 1, k][:, None, :]

    for i in range(depth):
        last = i == depth - 1
        j = i // n_mixers
        (sh1, csh1), (sc1, csc1), (g1, cg1) = mod(i, 0), mod(i, 1), mod(i, 2)
        (sh2, csh2), (sc2, csc2), (g2, cg2) = mod(i, 3), mod(i, 4), mod(i, 5)
        if i % n_mixers == 0:
            hk = ret_w_in.shape[2] // 6
            dk = hk // RET_HEADS
            w_in = ret_w_in[j].astype(BF16)
            kw = dict(scale_cols=(hk, 2 * hk), scale_val=dk ** -0.5, silu_cols=(4 * hk, 6 * hk))
            pl_ = _proj(x, norm1_w[i], sh1, sc1, w_in, rope=_rope_tables(seq, dk), rope_cols=2 * hk, **kw)
            pc = _proj(ctx, norm1_w[i], csh1, csc1, w_in, **kw)
            lg = jnp.stack([-jnp.exp(ret_decay_fwd[j].astype(F32)), -jnp.exp(ret_decay_bwd[j].astype(F32))])
            y, yc = _retention(lg, pl_, pc)
            w_out = ret_w_out[j].astype(BF16)
        else:
            w_qkv = na_w_qkv[j].astype(BF16)
            qkv = _proj(x, norm1_w[i], sh1, sc1, w_qkv, scale_cols=(0, d), scale_val=(d // NA_HEADS) ** -0.5 * LOG2E)
            kvc = _proj(ctx, norm1_w[i], csh1, csc1, w_qkv[:, d:])
            if not last:
                raise NotImplementedError("context output of the attention mixer is only needed for depth > 2")
            y, yc = _na_attention(qkv, kvc, na_rpb[j]), None
            w_out = na_w_out[j].astype(BF16)
        ffn_w = (ffn_w_up[i].astype(BF16), ffn_conv_w[i], ffn_conv_b[i][None, :], ffn_w_down[i].astype(BF16))
        x = _outproj(y, w_out, x, g1)
        x = _ffn(x, norm2_w[i], sh2, sc2, g2, *ffn_w, final_norm_w, final_norm=last)
        if not last:
            ctx = _outproj(yc, w_out, ctx, cg1)
            ctx = _ffn(ctx, norm2_w[i], csh2, csc2, cg2, *ffn_w, final_norm_w, final_norm=False)
    return x
```

```python
import functools

import jax
import jax.numpy as jnp
import numpy as np
from jax import lax
from jax.experimental import pallas as pl
from jax.experimental.pallas import tpu as pltpu

F32 = jnp.float32
BF16 = jnp.bfloat16

GRID_W = 64
RET_HEADS = 4
ROPE_BASE = 10000.0
NA_HEADS = 16
NA_MAX_KH = 8
NA_KW = 16
CONV_W = 3
NORM_EPS = 1e-6

LANES = 128
SUBLANES = 8
BF16_SUBLANES = 16
VMEM_LIMIT_BYTES = 56 * 1024 * 1024

ROW_TILE = 1024
PROJ_ROW_TILE = 512
MM_SUB = 256
RET_CHUNK = 256
NA_QROWS = 4
MASK_VALUE = -1e30
LOG2E = 1.4426950408889634


def _cparams(sem):
    return pltpu.CompilerParams(dimension_semantics=sem, vmem_limit_bytes=VMEM_LIMIT_BYTES)


def _bsel(arr):
    if arr.shape[0] == 1:
        return lambda b, *_: (0, 0, 0)
    return lambda b, *_: (b, 0, 0)


def _norm_mod(x, nw, shift, scale):
    y = x * lax.rsqrt(jnp.mean(x * x, axis=-1, keepdims=True) + NORM_EPS)
    return (y * nw) * (1.0 + scale) + shift


def _silu(x):
    return x / (1.0 + jnp.exp(-x))


def _ada_kernel(c_ref, w_ref, b_ref, o_ref):
    s = _silu(c_ref[...])
    o_ref[0] = jnp.dot(s, w_ref[0], preferred_element_type=F32) + b_ref[0]


def _ada(cc, ada_w, ada_b):
    depth, d, n = ada_w.shape
    tn = 1536
    return pl.pallas_call(
        _ada_kernel,
        grid=(depth, n // tn),
        in_specs=[
            pl.BlockSpec(cc.shape, lambda i, j: (0, 0)),
            pl.BlockSpec((1, d, tn), lambda i, j: (i, 0, j)),
            pl.BlockSpec((1, 1, tn), lambda i, j: (i, 0, j)),
        ],
        out_specs=pl.BlockSpec((1, cc.shape[0], tn), lambda i, j: (i, 0, j)),
        out_shape=jax.ShapeDtypeStruct((depth, cc.shape[0], n), F32),
        compiler_params=_cparams(("parallel", "parallel")),
        name="ada",
    )(cc, ada_w, ada_b.reshape(depth, 1, n))


def _pipelined_units(units, matmul, epilogue):
    acc = matmul(units[0])
    for n, u in enumerate(units):
        nxt = matmul(units[n + 1]) if n + 1 < len(units) else None
        epilogue(acc, u)
        acc = nxt


def _proj_kernel(*refs, tn, rope_cols, scale_cols, scale_val, silu_cols):
    if rope_cols:
        x_ref, nw_ref, sh_ref, sc_ref, w_ref, cos_ref, sin_ref, o_ref, h_ref = refs
    else:
        x_ref, nw_ref, sh_ref, sc_ref, w_ref, o_ref, h_ref = refs
    tm = x_ref.shape[1]
    sub = min(MM_SUB, tm)

    def matmul(u):
        s, j = u
        rows = slice(s * sub, (s + 1) * sub)
        if j == 0:
            h_ref[rows] = _norm_mod(x_ref[0, rows], nw_ref[...], sh_ref[0], sc_ref[0]).astype(BF16)
        return jnp.dot(h_ref[rows], w_ref[:, j * tn:(j + 1) * tn], preferred_element_type=F32)

    def epilogue(acc, u):
        s, j = u
        rows = slice(s * sub, (s + 1) * sub)
        col = j * tn
        if scale_cols[0] <= col < scale_cols[1]:
            acc = acc * scale_val
        if silu_cols[0] <= col < silu_cols[1]:
            acc = _silu(acc)
        if col >= rope_cols:
            o_ref[0, rows, col:col + tn] = acc.astype(o_ref.dtype)
            return
        for g in range(tn // LANES):
            part = acc[:, g * LANES:(g + 1) * LANES]
            t = (g % 2) * LANES
            rot = part * cos_ref[rows, t:t + LANES] + pltpu.roll(part, LANES // 2, 1) * sin_ref[rows, t:t + LANES]
            o_ref[0, rows, col + g * LANES:col + (g + 1) * LANES] = rot.astype(o_ref.dtype)

    units = [(s, j) for s in range(tm // sub) for j in range(w_ref.shape[1] // tn)]
    _pipelined_units(units, matmul, epilogue)


def _proj(x, nw, shift, scale, w, *, tn=512, rope=None, rope_cols=0, scale_cols=(0, 0), scale_val=1.0, silu_cols=(0, 0)):
    b, l, d = x.shape
    n = w.shape[1]
    tm = min(PROJ_ROW_TILE, l)
    assert all(v % tn == 0 for v in (n, rope_cols) + tuple(scale_cols) + tuple(silu_cols))
    in_specs = [
        pl.BlockSpec((1, tm, d), lambda b_, i: (b_, i, 0)),
        pl.BlockSpec((1, d), lambda b_, i: (0, 0)),
        pl.BlockSpec((1, 1, d), _bsel(shift)),
        pl.BlockSpec((1, 1, d), _bsel(scale)),
        pl.BlockSpec((d, n), lambda b_, i: (0, 0), pipeline_mode=pl.Buffered(1)),
    ]
    args = [x, nw.reshape(1, d), shift, scale, w]
    if rope is not None:
        cos, sin = rope
        in_specs += [pl.BlockSpec((tm, cos.shape[1]), lambda b_, i: (i, 0))] * 2
        args += [cos, sin]
    kern = functools.partial(_proj_kernel, tn=tn, rope_cols=(rope_cols if rope is not None else 0),
                             scale_cols=scale_cols, scale_val=scale_val, silu_cols=silu_cols)
    return pl.pallas_call(
        kern,
        grid=(b, l // tm),
        in_specs=in_specs,
        out_specs=pl.BlockSpec((1, tm, n), lambda b_, i: (b_, i, 0)),
        out_shape=jax.ShapeDtypeStruct((b, l, n), BF16),
        scratch_shapes=[pltpu.VMEM((tm, d), BF16)],
        compiler_params=_cparams(("parallel", "parallel")),
        name="proj",
    )(*args)


def _interleave_rows(v):
    sub, d = v.shape
    return v.reshape(SUBLANES, sub // SUBLANES, d).swapaxes(0, 1).reshape(sub, d)


def _deinterleave_rows(v):
    sub, d = v.shape
    return v.reshape(sub // SUBLANES, SUBLANES, d).swapaxes(0, 1).reshape(sub, d)


def _outproj_kernel(a_ref, w_ref, res_ref, gate_ref, nw_ref, sh_ref, sc_ref, o_ref, h_ref, *, tn):
    tm = a_ref.shape[1]
    sub = min(FFN_SUB, tm)
    nj = w_ref.shape[1] // tn

    def matmul(u):
        s, j = u
        return jnp.dot(a_ref[0, s * sub:(s + 1) * sub], w_ref[:, j * tn:(j + 1) * tn], preferred_element_type=F32)

    def epilogue(acc, u):
        s, j = u
        rows, cols = slice(s * sub, (s + 1) * sub), slice(j * tn, (j + 1) * tn)
        o_ref[0, rows, cols] = res_ref[0, rows, cols] + gate_ref[0, :, cols] * acc
        if j == nj - 1:
            h_ref[0, rows, :] = _norm_mod(o_ref[0, rows, :], nw_ref[...], sh_ref[0], sc_ref[0]).astype(h_ref.dtype)

    units = [(s, j) for s in range(tm // sub) for j in range(nj)]
    _pipelined_units(units, matmul, epilogue)


def _outproj(a, w, res, gate, nw, shift, scale, *, tn=512):
    b, l, k = a.shape
    n = w.shape[1]
    tm = min(ROW_TILE, l)
    assert tm % min(FFN_SUB, tm) == 0 and n % tn == 0
    vec = lambda arr: pl.BlockSpec((1, 1, n), _bsel(arr))
    return pl.pallas_call(
        functools.partial(_outproj_kernel, tn=tn),
        grid=(b, l // tm),
        in_specs=[
            pl.BlockSpec((1, tm, k), lambda b_, i: (b_, i, 0)),
            pl.BlockSpec((k, n), lambda b_, i: (0, 0), pipeline_mode=pl.Buffered(1)),
            pl.BlockSpec((1, tm, n), lambda b_, i: (b_, i, 0)),
            vec(gate),
            pl.BlockSpec((1, n), lambda b_, i: (0, 0)),
            vec(shift), vec(scale),
        ],
        out_specs=[pl.BlockSpec((1, tm, n), lambda b_, i: (b_, i, 0))] * 2,
        out_shape=[jax.ShapeDtypeStruct((b, l, n), F32), jax.ShapeDtypeStruct((b, l, n), BF16)],
        compiler_params=_cparams(("parallel", "parallel")),
        name="outproj",
    )(a, w, res, gate, nw.reshape(1, n), shift, scale)


FFN_TILE = 256
FFN_HALO = BF16_SUBLANES
FFN_SUB = 256
FFN_ROW_TILE = 1024


def _ffn_kernel(x_ref, hm_ref, hp_ref, hn_ref, g_ref, wu_ref, cw_ref, cb_ref, wd_ref, fw_ref,
                o_ref, h_ref, ua_ref, ug_ref, act_ref, *, final_norm):
    i = pl.program_id(1)
    tm = x_ref.shape[1]
    hl = FFN_HALO
    tf = FFN_TILE
    f = wd_ref.shape[0]
    sub = min(FFN_SUB, tm)
    nsub = tm // sub
    chunks = [(hl * (c > 0) + c * sub, hl + (c + 1) * sub + hl * (c == nsub - 1)) for c in range(nsub)]

    def stage(c):
        if c == 0:
            h_ref[0:hl] = jnp.where(i > 0, hp_ref[0], jnp.zeros_like(hp_ref[0]))
        if c == nsub - 1:
            h_ref[hl + tm:2 * hl + tm] = jnp.where(i < pl.num_programs(1) - 1, hn_ref[0], jnp.zeros_like(hn_ref[0]))
        h_ref[hl + c * sub:hl + (c + 1) * sub] = _interleave_rows(hm_ref[0, c * sub:(c + 1) * sub].astype(F32)).astype(BF16)

    def conv(u_ref, slot, col, s):
        base = hl + s * sub
        st = SUBLANES
        cw = cw_ref[:, col:col + tf]
        blk = u_ref[slot, base - st:base + sub + st]
        sublane = lax.broadcasted_iota(jnp.int32, (st, tf), 0)
        first_prev = jnp.where(sublane == 0, pltpu.roll(blk[0:st], 1, 0), pltpu.roll(blk[sub:sub + st], 1, 0))
        last_next = jnp.where(sublane == st - 1, pltpu.roll(blk[sub + st:sub + 2 * st], st - 1, 0),
                              pltpu.roll(blk[st:2 * st], st - 1, 0))
        prev = jnp.concatenate([first_prev, blk[st:sub]], axis=0)
        nxt = jnp.concatenate([blk[2 * st:sub + st], last_next], axis=0)
        return (prev * cw[0:1] + blk[st:sub + st] * cw[1:2] + nxt * cw[2:3]) + cb_ref[:, col:col + tf]

    def up(t, c):
        lo, hi = chunks[c]
        if t == 0:
            stage(c)
        ua_ref[t % 2, lo:hi] = jnp.dot(h_ref[lo:hi], wu_ref[:, t * tf:(t + 1) * tf], preferred_element_type=F32)
        ug_ref[t % 2, lo:hi] = jnp.dot(h_ref[lo:hi], wu_ref[:, f + t * tf:f + (t + 1) * tf], preferred_element_type=F32)

    units = [(t, c) for t in range(f // tf) for c in range(nsub)]
    up(*units[0])
    for n, (t, s) in enumerate(units):
        if n + 1 < len(units):
            up(*units[n + 1])
        act = _silu(conv(ua_ref, t % 2, t * tf, s)) * conv(ug_ref, t % 2, f + t * tf, s)
        act_ref[s * sub:(s + 1) * sub, t * tf:(t + 1) * tf] = act.astype(BF16)

    for s in range(nsub):
        rows = slice(s * sub, (s + 1) * sub)
        down = _deinterleave_rows(jnp.dot(act_ref[rows], wd_ref[...], preferred_element_type=F32))
        y = x_ref[0, rows] + g_ref[0] * down
        if final_norm:
            y = (y * lax.rsqrt(jnp.mean(y * y, axis=-1, keepdims=True) + NORM_EPS)) * fw_ref[...]
        o_ref[0, rows] = y


def _ffn(x, h, gate, wu, cw, cb, wd, final_w, *, final_norm):
    b, l, d = x.shape
    f = wd.shape[0]
    tf = FFN_TILE
    assert f % tf == 0, wd.shape
    tm = min(FFN_ROW_TILE, l)
    hl = FFN_HALO
    per = tm // hl
    nblk = l // hl
    whole = lambda arr: pl.BlockSpec(arr.shape, lambda b_, i: (0, 0), pipeline_mode=pl.Buffered(1))
    return pl.pallas_call(
        functools.partial(_ffn_kernel, final_norm=final_norm),
        grid=(b, l // tm),
        in_specs=[
            pl.BlockSpec((1, tm, d), lambda b_, i: (b_, i, 0)),
            pl.BlockSpec((1, tm, d), lambda b_, i: (b_, i, 0)),
            pl.BlockSpec((1, hl, d), lambda b_, i: (b_, jnp.maximum(i * per - 1, 0), 0)),
            pl.BlockSpec((1, hl, d), lambda b_, i: (b_, jnp.minimum((i + 1) * per, nblk - 1), 0)),
            pl.BlockSpec((1, 1, d), _bsel(gate)),
            whole(wu), whole(cw), whole(cb), whole(wd),
            pl.BlockSpec((1, d), lambda b_, i: (0, 0)),
        ],
        out_specs=pl.BlockSpec((1, tm, d), lambda b_, i: (b_, i, 0)),
        out_shape=jax.ShapeDtypeStruct((b, l, d), F32),
        scratch_shapes=[pltpu.VMEM((tm + 2 * hl, d), BF16), pltpu.VMEM((2, tm + 2 * hl, tf), F32),
                        pltpu.VMEM((2, tm + 2 * hl, tf), F32), pltpu.VMEM((tm, f), BF16)],
        compiler_params=_cparams(("parallel", "parallel")),
        name="ffn",
    )(x, h, h, h, gate, wu, cw, cb, wd, final_w.reshape(1, d))


def _ret_kernel(lg_ref, ql_ref, kl_ref, vl_ref, gl_ref, qc_ref, kc_ref, vc_ref, gc_ref, ol_ref, oc_ref, sb_ref):
    c = RET_CHUNK
    h = pl.program_id(1)
    lgf = lg_ref[0, h]
    lgb = lg_ref[1, h]
    n_ctx = qc_ref.shape[1] // c
    n_lat = ql_ref.shape[1] // c

    pos = lax.broadcasted_iota(jnp.int32, (c, 1), 0).astype(F32)
    kdec_f = jnp.exp(lgf * (c - 1.0 - pos))
    kdec_b = jnp.exp(lgb * pos)
    qdec_f = jnp.exp(lgf * (pos + 1.0))
    qdec_b = jnp.exp(lgb * (c - pos))
    one = jnp.ones((1, 1), F32)
    cdec_f = jnp.exp(lgf * c * one)
    cdec_b = jnp.exp(lgb * c * one)
    diff = (lax.broadcasted_iota(jnp.int32, (c, c), 0) - lax.broadcasted_iota(jnp.int32, (c, c), 1)).astype(F32)
    dmat = jnp.exp(jnp.where(diff >= 0, lgf, lgb) * jnp.abs(diff))

    seq = [(qc_ref, kc_ref, vc_ref, gc_ref, oc_ref, slice(t * c, (t + 1) * c)) for t in range(n_ctx)]
    seq += [(ql_ref, kl_ref, vl_ref, gl_ref, ol_ref, slice(t * c, (t + 1) * c)) for t in range(n_lat)]
    n = len(seq)

    def kv(idx, kdec):
        _, k_ref, v_ref, _, _, sl = seq[idx]
        kd = (k_ref[0, sl, :].astype(F32) * kdec).astype(BF16)
        return lax.dot_general(kd, v_ref[0, sl, :], (((0,), (0,)), ((), ())), preferred_element_type=F32)

    def qk(idx):
        q_ref, k_ref, _, _, _, sl = seq[idx]
        return lax.dot_general(q_ref[0, sl, :], k_ref[0, sl, :], (((1,), (1,)), ((), ())), preferred_element_type=F32)

    bwd_order = list(range(n_ctx - 1, -1, -1)) + list(range(n - 1, n_ctx - 1, -1))
    s = None
    for pos_, idx in enumerate(bwd_order):
        if s is not None:
            sb_ref[idx] = s.astype(BF16)
        if pos_ + 1 < n:
            upd = kv(idx, kdec_b)
            s = upd if s is None else s * cdec_b + upd

    s = None
    scores = qk(0)
    for idx in range(n):
        q_ref, _, v_ref, g_ref, o_ref, sl = seq[idx]
        qf = q_ref[0, sl, :].astype(F32)
        cross = None
        if idx != bwd_order[0]:
            cross = jnp.dot((qf * qdec_b).astype(BF16), sb_ref[idx], preferred_element_type=F32)
        if s is not None:
            cf = jnp.dot((qf * qdec_f).astype(BF16), s.astype(BF16), preferred_element_type=F32)
            cross = cf if cross is None else cross + cf
        upd = kv(idx, kdec_f) if idx + 1 < n else None
        nxt = qk(idx + 1) if idx + 1 < n else None
        o = jnp.dot((scores * dmat).astype(BF16), v_ref[0, sl, :], preferred_element_type=F32)
        if cross is not None:
            o = o + cross
        if upd is not None:
            s = upd if s is None else s * cdec_f + upd
        o = o * lax.rsqrt(jnp.mean(o * o, axis=-1, keepdims=True) + NORM_EPS)
        o_ref[0, sl, :] = (g_ref[0, sl, :].astype(F32) * o).astype(BF16)
        scores = nxt


def _retention(lg, pl_, pc):
    b, l, n = pl_.shape
    lc = pc.shape[1]
    hh = RET_HEADS
    dk = n // (6 * hh)
    dv = 2 * dk
    c = RET_CHUNK

    def specs(length):
        return [
            pl.BlockSpec((1, length, dk), lambda b_, h: (b_, 0, h)),
            pl.BlockSpec((1, length, dk), lambda b_, h: (b_, 0, hh + h)),
            pl.BlockSpec((1, length, dv), lambda b_, h: (b_, 0, hh + h)),
            pl.BlockSpec((1, length, dv), lambda b_, h: (b_, 0, 2 * hh + h)),
        ]

    return pl.pallas_call(
        _ret_kernel,
        grid=(b, hh),
        in_specs=[pl.BlockSpec(memory_space=pltpu.SMEM)] + specs(l) + specs(lc),
        out_specs=[
            pl.BlockSpec((1, l, dv), lambda b_, h: (b_, 0, h)),
            pl.BlockSpec((1, lc, dv), lambda b_, h: (b_, 0, h)),
        ],
        out_shape=[jax.ShapeDtypeStruct((b, l, hh * dv), BF16), jax.ShapeDtypeStruct((b, lc, hh * dv), BF16)],
        scratch_shapes=[pltpu.VMEM((lc // c + l // c, dk, dv), BF16)],
        compiler_params=_cparams(("parallel", "arbitrary")),
        name="retention",
    )(lg, pl_, pl_, pl_, pl_, pc, pc, pc, pc)


def _rope_tables(seq, dk):
    quarter = dk // 4
    t = jnp.arange(seq)
    pos_r = (t // GRID_W).astype(F32)
    pos_c = (t % GRID_W).astype(F32)
    inv = ROPE_BASE ** (-jnp.arange(quarter, dtype=F32) / quarter)
    ang_r = pos_r[:, None] * inv[None, :]
    ang_c = pos_c[:, None] * inv[None, :]
    cos = jnp.concatenate([jnp.cos(ang_r)] * 2 + [jnp.cos(ang_c)] * 2, axis=-1)
    sin = jnp.concatenate([-jnp.sin(ang_r), jnp.sin(ang_r), -jnp.sin(ang_c), jnp.sin(ang_c)], axis=-1)
    return cos, sin


def _na_window(rows):
    kh = min(NA_MAX_KH, rows)
    return kh, NA_QROWS + kh


def _na_window_start(blk, rows):
    kh, wrows = _na_window(rows)
    return min(max(blk * NA_QROWS - kh // 2, 0), rows - wrows)


def _na_kernel(m_ref, q_ref, k_ref, v_ref, kc_ref, vc_ref, o_ref, bias_ref, sl_ref, sc_ref, pl_ref, pc_ref, *, rows):
    b = pl.program_id(1)
    kh, wrows = _na_window(rows)
    qr_n = NA_QROWS
    w = GRID_W
    hd = q_ref.shape[2] // 2
    nblk = rows // qr_n

    @pl.when(b == 0)
    def _():
        qc = lax.broadcasted_iota(jnp.int32, (w, LANES), 0)
        kc = lax.broadcasted_iota(jnp.int32, (w, LANES), 1)
        cs = jnp.clip(qc - NA_KW // 2, 0, w - NA_KW)
        col_valid = (kc >= cs) & (kc < cs + NA_KW)
        toe = [[jnp.where(col_valid,
                          pltpu.roll(jnp.broadcast_to(m_ref[e, dr:dr + 1, :], (w, LANES)), 0, 1, stride=1, stride_axis=0),
                          MASK_VALUE)[:, :w]
                for dr in range(2 * kh - 1)] for e in range(2)]
        for variant in range(3):
            r0 = (0, qr_n, rows - qr_n)[variant]
            ws = _na_window_start(r0 // qr_n, rows)
            for qr in range(qr_n):
                r = r0 + qr
                rs = min(max(r - kh // 2, 0), rows - kh)
                for krr in range(wrows):
                    kr = ws + krr
                    for e in range(2):
                        dst = (e, variant, slice(qr * w, (qr + 1) * w), slice(krr * w, (krr + 1) * w))
                        if rs <= kr < rs + kh:
                            bias_ref[dst] = toe[e][kr - r + kh - 1]
                        else:
                            bias_ref[dst] = jnp.full((w, w), MASK_VALUE, F32)

    vc = vc_ref[0]
    nq = qr_n * w
    lane = lax.broadcasted_iota(jnp.int32, (nq, q_ref.shape[2]), 1)

    kc = kc_ref[0]
    trans_b = (((1,), (1,)), ((), ()))

    chains = []
    for k in range(nblk):
        variant = 0 if k == 0 else (2 if k == nblk - 1 else 1)
        ws = _na_window_start(k, rows)
        win = slice(ws * w, (ws + wrows) * w)
        q = q_ref[0, k * nq:(k + 1) * nq, :]
        for e in range(2):
            qe = jnp.where((lane >= e * hd) & (lane < (e + 1) * hd), q, jnp.zeros_like(q))
            ci = len(chains)
            sl_ref[ci] = lax.dot_general(qe, k_ref[0, win, :], trans_b, preferred_element_type=F32) + bias_ref[e, variant]
            sc_ref[ci] = lax.dot_general(qe, kc, trans_b, preferred_element_type=F32)
            chains.append((k, e, win, ci))

    def head_values(v, e):
        vlane = lax.broadcasted_iota(jnp.int32, v.shape, 1)
        return jnp.where((vlane >= e * hd) & (vlane < (e + 1) * hd), v, jnp.ones_like(v))

    vc_e = [head_values(vc, e) for e in range(2)]
    outs = {}
    for k, e, win, ci in chains:
        m = jnp.maximum(jnp.max(sl_ref[ci], axis=-1, keepdims=True), jnp.max(sc_ref[ci], axis=-1, keepdims=True))
        pl_ref[ci] = jnp.exp2(sl_ref[ci] - m).astype(BF16)
        pc_ref[ci] = jnp.exp2(sc_ref[ci] - m).astype(BF16)
        o = jnp.dot(pl_ref[ci], head_values(v_ref[0, win, :], e), preferred_element_type=F32)
        o += jnp.dot(pc_ref[ci], vc_e[e], preferred_element_type=F32)
        outs[k, e] = o / pltpu.roll(o, hd, 1)
        if e == 1:
            o_ref[0, k * nq:(k + 1) * nq, :] = jnp.where(lane < hd, outs[k, 0], outs[k, 1]).astype(o_ref.dtype)


def _na_bias_table(rpb):
    padded = jnp.pad(rpb * LOG2E, ((0, 0), (0, 0), (0, LANES - rpb.shape[2])))
    return jnp.roll(padded, -(NA_KW - 1), axis=2)


def _na_attention(qkv, kvc, rpb):
    b, s, d3 = qkv.shape
    d = d3 // 3
    lc = kvc.shape[1]
    rows = s // GRID_W
    pairs = d // LANES
    kh, wrows = _na_window(rows)
    qrows = NA_QROWS * GRID_W
    nchain = 2 * (rows // NA_QROWS)
    assert kh == 2 * NA_QROWS and rows % NA_QROWS == 0 and rows >= wrows + NA_QROWS, rows
    mtab = _na_bias_table(rpb)
    return pl.pallas_call(
        functools.partial(_na_kernel, rows=rows),
        grid=(pairs, b),
        in_specs=[
            pl.BlockSpec((2,) + mtab.shape[1:], lambda p, b_: (p, 0, 0)),
            pl.BlockSpec((1, s, LANES), lambda p, b_: (b_, 0, p)),
            pl.BlockSpec((1, s, LANES), lambda p, b_: (b_, 0, pairs + p)),
            pl.BlockSpec((1, s, LANES), lambda p, b_: (b_, 0, 2 * pairs + p)),
            pl.BlockSpec((1, lc, LANES), lambda p, b_: (b_, 0, p)),
            pl.BlockSpec((1, lc, LANES), lambda p, b_: (b_, 0, pairs + p)),
        ],
        out_specs=pl.BlockSpec((1, s, LANES), lambda p, b_: (b_, 0, p)),
        out_shape=jax.ShapeDtypeStruct((b, s, d), BF16),
        scratch_shapes=[pltpu.VMEM((2, 3, qrows, wrows * GRID_W), F32),
                        pltpu.VMEM((nchain, qrows, wrows * GRID_W), F32), pltpu.VMEM((nchain, qrows, lc), F32),
                        pltpu.VMEM((nchain, qrows, wrows * GRID_W), BF16), pltpu.VMEM((nchain, qrows, lc), BF16)],
        compiler_params=_cparams(("arbitrary", "arbitrary")),
        name="na_attention",
    )(mtab, qkv, qkv, qkv, kvc, kvc)


def kernel(x, c, ctx, c_ctx, ada_w, ada_b, norm1_w, norm2_w, ret_w_in, ret_decay_fwd, ret_decay_bwd, ret_w_out,
           na_w_qkv, na_rpb, na_w_out, ffn_w_up, ffn_conv_w, ffn_conv_b, ffn_w_down, final_norm_w):
    b, seq, d = x.shape
    depth = ada_w.shape[0]
    n_mixers = 2

    pad = BF16_SUBLANES - (b + 1) % BF16_SUBLANES
    cc = jnp.concatenate([c, c_ctx[None, :], jnp.zeros((pad, d), F32)], axis=0)
    mods = _ada(cc, ada_w, ada_b).reshape(depth, cc.shape[0], 6, d)

    def mod(i, k):
        return mods[i, :b, k][:, None, :], mods[i, b:b + 1, k][:, None, :]

    for i in range(depth):
        last = i == depth - 1
        j = i // n_mixers
        (sh1, csh1), (sc1, csc1), (g1, cg1) = mod(i, 0), mod(i, 1), mod(i, 2)
        (sh2, csh2), (sc2, csc2), (g2, cg2) = mod(i, 3), mod(i, 4), mod(i, 5)
        if i % n_mixers == 0:
            hk = ret_w_in.shape[2] // 6
            dk = hk // RET_HEADS
            w_in = ret_w_in[j].astype(BF16)
            kw = dict(scale_cols=(hk, 2 * hk), scale_val=dk ** -0.5, silu_cols=(4 * hk, 6 * hk))
            pl_ = _proj(x, norm1_w[i], sh1, sc1, w_in, rope=_rope_tables(seq, dk), rope_cols=2 * hk, **kw)
            pc = _proj(ctx, norm1_w[i], csh1, csc1, w_in, **kw)
            lg = jnp.stack([-jnp.exp(ret_decay_fwd[j].astype(F32)), -jnp.exp(ret_decay_bwd[j].astype(F32))])
            y, yc = _retention(lg, pl_, pc)
            w_out = ret_w_out[j].astype(BF16)
        else:
            w_qkv = na_w_qkv[j].astype(BF16)
            qkv = _proj(x, norm1_w[i], sh1, sc1, w_qkv, scale_cols=(0, d), scale_val=(d // NA_HEADS) ** -0.5 * LOG2E)
            kvc = _proj(ctx, norm1_w[i], csh1, csc1, w_qkv[:, d:])
            if not last:
                raise NotImplementedError("context output of the attention mixer is only needed for depth > 2")
            y, yc = _na_attention(qkv, kvc, na_rpb[j]), None
            w_out = na_w_out[j].astype(BF16)
        ffn_w = (ffn_w_up[i].astype(BF16), ffn_conv_w[i], ffn_conv_b[i][None, :], ffn_w_down[i].astype(BF16))
        x, h2 = _outproj(y, w_out, x, g1, norm2_w[i], sh2, sc2)
        x = _ffn(x, h2, g2, *ffn_w, final_norm_w, final_norm=last)
        if not last:
            ctx, hc2 = _outproj(yc, w_out, ctx, cg1, norm2_w[i], csh2, csc2)
            ctx = _ffn(ctx, hc2, cg2, *ffn_w, final_norm_w, final_norm=False)
    return x
```

```python
import functools

import jax
import jax.numpy as jnp
import numpy as np
from jax import lax
from jax.experimental import pallas as pl
from jax.experimental.pallas import tpu as pltpu

F32 = jnp.float32
BF16 = jnp.bfloat16

GRID_W = 64
RET_HEADS = 4
ROPE_BASE = 10000.0
NA_HEADS = 16
NA_MAX_KH = 8
NA_KW = 16
CONV_W = 3
NORM_EPS = 1e-6

LANES = 128
SUBLANES = 8
BF16_SUBLANES = 16
VMEM_LIMIT_BYTES = 56 * 1024 * 1024

ROW_TILE = 1024
PROJ_ROW_TILE = 512
MM_SUB = 256
RET_CHUNK = 256
NA_QROWS = 4
MASK_VALUE = -1e30
LOG2E = 1.4426950408889634


def _cparams(sem):
    return pltpu.CompilerParams(dimension_semantics=sem, vmem_limit_bytes=VMEM_LIMIT_BYTES)


def _bsel(arr):
    if arr.shape[0] == 1:
        return lambda b, *_: (0, 0, 0)
    return lambda b, *_: (b, 0, 0)


def _norm_mod(x, nw, shift, scale):
    y = x * lax.rsqrt(jnp.mean(x * x, axis=-1, keepdims=True) + NORM_EPS)
    return (y * nw) * (1.0 + scale) + shift


def _silu(x):
    return x / (1.0 + jnp.exp(-x))


def _ada_kernel(c_ref, w_ref, b_ref, o_ref):
    s = _silu(c_ref[...])
    o_ref[0] = jnp.dot(s, w_ref[0], preferred_element_type=F32) + b_ref[0]


def _ada(cc, ada_w, ada_b):
    depth, d, n = ada_w.shape
    tn = 1536
    return pl.pallas_call(
        _ada_kernel,
        grid=(depth, n // tn),
        in_specs=[
            pl.BlockSpec(cc.shape, lambda i, j: (0, 0)),
            pl.BlockSpec((1, d, tn), lambda i, j: (i, 0, j)),
            pl.BlockSpec((1, 1, tn), lambda i, j: (i, 0, j)),
        ],
        out_specs=pl.BlockSpec((1, cc.shape[0], tn), lambda i, j: (i, 0, j)),
        out_shape=jax.ShapeDtypeStruct((depth, cc.shape[0], n), F32),
        compiler_params=_cparams(("parallel", "parallel")),
        name="ada",
    )(cc, ada_w, ada_b.reshape(depth, 1, n))


def _pipelined_units(units, matmul, epilogue):
    acc = matmul(units[0])
    for n, u in enumerate(units):
        nxt = matmul(units[n + 1]) if n + 1 < len(units) else None
        epilogue(acc, u)
        acc = nxt


def _proj_kernel(*refs, tn, rope_cols, scale_cols, scale_val, silu_cols):
    if rope_cols:
        x_ref, nw_ref, sh_ref, sc_ref, w_ref, cos_ref, sin_ref, o_ref, h_ref = refs
    else:
        x_ref, nw_ref, sh_ref, sc_ref, w_ref, o_ref, h_ref = refs
    tm = x_ref.shape[1]
    sub = min(MM_SUB, tm)

    def matmul(u):
        s, j = u
        rows = slice(s * sub, (s + 1) * sub)
        if j == 0:
            h_ref[rows] = _norm_mod(x_ref[0, rows], nw_ref[...], sh_ref[0], sc_ref[0]).astype(BF16)
        return jnp.dot(h_ref[rows], w_ref[:, j * tn:(j + 1) * tn], preferred_element_type=F32)

    def epilogue(acc, u):
        s, j = u
        rows = slice(s * sub, (s + 1) * sub)
        col = j * tn
        if scale_cols[0] <= col < scale_cols[1]:
            acc = acc * scale_val
        if silu_cols[0] <= col < silu_cols[1]:
            acc = _silu(acc)
        if col >= rope_cols:
            o_ref[0, rows, col:col + tn] = acc.astype(o_ref.dtype)
            return
        for g in range(tn // LANES):
            part = acc[:, g * LANES:(g + 1) * LANES]
            t = (g % 2) * LANES
            rot = part * cos_ref[rows, t:t + LANES] + pltpu.roll(part, LANES // 2, 1) * sin_ref[rows, t:t + LANES]
            o_ref[0, rows, col + g * LANES:col + (g + 1) * LANES] = rot.astype(o_ref.dtype)

    units = [(s, j) for s in range(tm // sub) for j in range(w_ref.shape[1] // tn)]
    _pipelined_units(units, matmul, epilogue)


def _proj(x, nw, shift, scale, w, *, tn=512, rope=None, rope_cols=0, scale_cols=(0, 0), scale_val=1.0, silu_cols=(0, 0)):
    b, l, d = x.shape
    n = w.shape[1]
    tm = min(PROJ_ROW_TILE, l)
    assert all(v % tn == 0 for v in (n, rope_cols) + tuple(scale_cols) + tuple(silu_cols))
    in_specs = [
        pl.BlockSpec((1, tm, d), lambda b_, i: (b_, i, 0)),
        pl.BlockSpec((1, d), lambda b_, i: (0, 0)),
        pl.BlockSpec((1, 1, d), _bsel(shift)),
        pl.BlockSpec((1, 1, d), _bsel(scale)),
        pl.BlockSpec((d, n), lambda b_, i: (0, 0), pipeline_mode=pl.Buffered(1)),
    ]
    args = [x, nw.reshape(1, d), shift, scale, w]
    if rope is not None:
        cos, sin = rope
        in_specs += [pl.BlockSpec((tm, cos.shape[1]), lambda b_, i: (i, 0))] * 2
        args += [cos, sin]
    kern = functools.partial(_proj_kernel, tn=tn, rope_cols=(rope_cols if rope is not None else 0),
                             scale_cols=scale_cols, scale_val=scale_val, silu_cols=silu_cols)
    return pl.pallas_call(
        kern,
        grid=(b, l // tm),
        in_specs=in_specs,
        out_specs=pl.BlockSpec((1, tm, n), lambda b_, i: (b_, i, 0)),
        out_shape=jax.ShapeDtypeStruct((b, l, n), BF16),
        scratch_shapes=[pltpu.VMEM((tm, d), BF16)],
        compiler_params=_cparams(("parallel", "parallel")),
        name="proj",
    )(*args)


def _interleave_rows(v):
    sub, d = v.shape
    return v.reshape(SUBLANES, sub // SUBLANES, d).swapaxes(0, 1).reshape(sub, d)


def _deinterleave_rows(v):
    sub, d = v.shape
    return v.reshape(sub // SUBLANES, SUBLANES, d).swapaxes(0, 1).reshape(sub, d)


def _outproj_kernel(a_ref, w_ref, res_ref, gate_ref, nw_ref, sh_ref, sc_ref, o_ref, h_ref, *, tn):
    tm = a_ref.shape[1]
    sub = min(FFN_SUB, tm)
    nj = w_ref.shape[1] // tn

    def matmul(u):
        s, j = u
        return jnp.dot(a_ref[0, s * sub:(s + 1) * sub], w_ref[:, j * tn:(j + 1) * tn], preferred_element_type=F32)

    def epilogue(acc, u):
        s, j = u
        rows, cols = slice(s * sub, (s + 1) * sub), slice(j * tn, (j + 1) * tn)
        o_ref[0, rows, cols] = res_ref[0, rows, cols] + gate_ref[0, :, cols] * acc
        if j == nj - 1:
            h_ref[0, rows, :] = _norm_mod(o_ref[0, rows, :], nw_ref[...], sh_ref[0], sc_ref[0]).astype(h_ref.dtype)

    units = [(s, j) for s in range(tm // sub) for j in range(nj)]
    _pipelined_units(units, matmul, epilogue)


def _outproj(a, w, res, gate, nw, shift, scale, *, tn=512):
    b, l, k = a.shape
    n = w.shape[1]
    tm = min(ROW_TILE, l)
    assert tm % min(FFN_SUB, tm) == 0 and n % tn == 0
    vec = lambda arr: pl.BlockSpec((1, 1, n), _bsel(arr))
    return pl.pallas_call(
        functools.partial(_outproj_kernel, tn=tn),
        grid=(b, l // tm),
        in_specs=[
            pl.BlockSpec((1, tm, k), lambda b_, i: (b_, i, 0)),
            pl.BlockSpec((k, n), lambda b_, i: (0, 0), pipeline_mode=pl.Buffered(1)),
            pl.BlockSpec((1, tm, n), lambda b_, i: (b_, i, 0)),
            vec(gate),
            pl.BlockSpec((1, n), lambda b_, i: (0, 0)),
            vec(shift), vec(scale),
        ],
        out_specs=[pl.BlockSpec((1, tm, n), lambda b_, i: (b_, i, 0))] * 2,
        out_shape=[jax.ShapeDtypeStruct((b, l, n), F32), jax.ShapeDtypeStruct((b, l, n), BF16)],
        compiler_params=_cparams(("parallel", "parallel")),
        name="outproj",
    )(a, w, res, gate, nw.reshape(1, n), shift, scale)


FFN_TILE = 256
FFN_HALO = BF16_SUBLANES
FFN_SUB = 256
FFN_ROW_TILE = 1024


def _ffn_kernel(x_ref, hm_ref, hp_ref, hn_ref, g_ref, wu_ref, cw_ref, cb_ref, wd_ref, fw_ref,
                o_ref, h_ref, ua_ref, ug_ref, act_ref, *, final_norm):
    i = pl.program_id(1)
    tm = x_ref.shape[1]
    hl = FFN_HALO
    tf = FFN_TILE
    f = wd_ref.shape[0]
    sub = min(FFN_SUB, tm)
    nsub = tm // sub
    chunks = [(hl * (c > 0) + c * sub, hl + (c + 1) * sub + hl * (c == nsub - 1)) for c in range(nsub)]

    def stage(c):
        if c == 0:
            h_ref[0:hl] = jnp.where(i > 0, hp_ref[0], jnp.zeros_like(hp_ref[0]))
        if c == nsub - 1:
            h_ref[hl + tm:2 * hl + tm] = jnp.where(i < pl.num_programs(1) - 1, hn_ref[0], jnp.zeros_like(hn_ref[0]))
        h_ref[hl + c * sub:hl + (c + 1) * sub] = _interleave_rows(hm_ref[0, c * sub:(c + 1) * sub].astype(F32)).astype(BF16)

    def conv(u_ref, slot, col, s):
        base = hl + s * sub
        st = SUBLANES
        cw = cw_ref[:, col:col + tf]
        blk = u_ref[slot, base - st:base + sub + st]
        sublane = lax.broadcasted_iota(jnp.int32, (st, tf), 0)
        first_prev = jnp.where(sublane == 0, pltpu.roll(blk[0:st], 1, 0), pltpu.roll(blk[sub:sub + st], 1, 0))
        last_next = jnp.where(sublane == st - 1, pltpu.roll(blk[sub + st:sub + 2 * st], st - 1, 0),
                              pltpu.roll(blk[st:2 * st], st - 1, 0))
        prev = jnp.concatenate([first_prev, blk[st:sub]], axis=0)
        nxt = jnp.concatenate([blk[2 * st:sub + st], last_next], axis=0)
        return (prev * cw[0:1] + blk[st:sub + st] * cw[1:2] + nxt * cw[2:3]) + cb_ref[:, col:col + tf]

    def up(t, c):
        lo, hi = chunks[c]
        if t == 0:
            stage(c)
        ua_ref[t % 2, lo:hi] = jnp.dot(h_ref[lo:hi], wu_ref[:, t * tf:(t + 1) * tf], preferred_element_type=F32)
        ug_ref[t % 2, lo:hi] = jnp.dot(h_ref[lo:hi], wu_ref[:, f + t * tf:f + (t + 1) * tf], preferred_element_type=F32)

    units = [(t, c) for t in range(f // tf) for c in range(nsub)]
    up(*units[0])
    for n, (t, s) in enumerate(units):
        if n + 1 < len(units):
            up(*units[n + 1])
        act = _silu(conv(ua_ref, t % 2, t * tf, s)) * conv(ug_ref, t % 2, f + t * tf, s)
        act_ref[s * sub:(s + 1) * sub, t * tf:(t + 1) * tf] = act.astype(BF16)

    for s in range(nsub):
        rows = slice(s * sub, (s + 1) * sub)
        down = _deinterleave_rows(jnp.dot(act_ref[rows], wd_ref[...], preferred_element_type=F32))
        y = x_ref[0, rows] + g_ref[0] * down
        if final_norm:
            y = (y * lax.rsqrt(jnp.mean(y * y, axis=-1, keepdims=True) + NORM_EPS)) * fw_ref[...]
        o_ref[0, rows] = y


def _ffn(x, h, gate, wu, cw, cb, wd, final_w, *, layer, final_norm):
    b, l, d = x.shape
    f = wd.shape[1]
    tf = FFN_TILE
    assert f % tf == 0, wd.shape
    tm = min(FFN_ROW_TILE, l)
    hl = FFN_HALO
    per = tm // hl
    nblk = l // hl
    whole = lambda arr: pl.BlockSpec((None,) + arr.shape[1:], lambda b_, i: (layer, 0, 0), pipeline_mode=pl.Buffered(1))
    return pl.pallas_call(
        functools.partial(_ffn_kernel, final_norm=final_norm),
        grid=(b, l // tm),
        in_specs=[
            pl.BlockSpec((1, tm, d), lambda b_, i: (b_, i, 0)),
            pl.BlockSpec((1, tm, d), lambda b_, i: (b_, i, 0)),
            pl.BlockSpec((1, hl, d), lambda b_, i: (b_, jnp.maximum(i * per - 1, 0), 0)),
            pl.BlockSpec((1, hl, d), lambda b_, i: (b_, jnp.minimum((i + 1) * per, nblk - 1), 0)),
            pl.BlockSpec((1, 1, d), _bsel(gate)),
            whole(wu), whole(cw), whole(cb), whole(wd),
            pl.BlockSpec((1, d), lambda b_, i: (0, 0)),
        ],
        out_specs=pl.BlockSpec((1, tm, d), lambda b_, i: (b_, i, 0)),
        out_shape=jax.ShapeDtypeStruct((b, l, d), F32),
        scratch_shapes=[pltpu.VMEM((tm + 2 * hl, d), BF16), pltpu.VMEM((2, tm + 2 * hl, tf), F32),
                        pltpu.VMEM((2, tm + 2 * hl, tf), F32), pltpu.VMEM((tm, f), BF16)],
        compiler_params=_cparams(("parallel", "parallel")),
        name="ffn",
    )(x, h, h, h, gate, wu, cw, cb, wd, final_w.reshape(1, d))


def _ret_kernel(lg_ref, ql_ref, kl_ref, vl_ref, gl_ref, qc_ref, kc_ref, vc_ref, gc_ref, ol_ref, oc_ref, sb_ref):
    c = RET_CHUNK
    h = pl.program_id(1)
    lgf = lg_ref[0, h]
    lgb = lg_ref[1, h]
    n_ctx = qc_ref.shape[1] // c
    n_lat = ql_ref.shape[1] // c

    pos = lax.broadcasted_iota(jnp.int32, (c, 1), 0).astype(F32)
    kdec_f = jnp.exp(lgf * (c - 1.0 - pos))
    kdec_b = jnp.exp(lgb * pos)
    qdec_f = jnp.exp(lgf * (pos + 1.0))
    qdec_b = jnp.exp(lgb * (c - pos))
    one = jnp.ones((1, 1), F32)
    cdec_f = jnp.exp(lgf * c * one)
    cdec_b = jnp.exp(lgb * c * one)
    diff = (lax.broadcasted_iota(jnp.int32, (c, c), 0) - lax.broadcasted_iota(jnp.int32, (c, c), 1)).astype(F32)
    dmat = jnp.exp(jnp.where(diff >= 0, lgf, lgb) * jnp.abs(diff))

    seq = [(qc_ref, kc_ref, vc_ref, gc_ref, oc_ref, slice(t * c, (t + 1) * c)) for t in range(n_ctx)]
    seq += [(ql_ref, kl_ref, vl_ref, gl_ref, ol_ref, slice(t * c, (t + 1) * c)) for t in range(n_lat)]
    n = len(seq)

    def kv(idx, kdec):
        _, k_ref, v_ref, _, _, sl = seq[idx]
        kd = (k_ref[0, sl, :].astype(F32) * kdec).astype(BF16)
        return lax.dot_general(kd, v_ref[0, sl, :], (((0,), (0,)), ((), ())), preferred_element_type=F32)

    def qk(idx):
        q_ref, k_ref, _, _, _, sl = seq[idx]
        return lax.dot_general(q_ref[0, sl, :], k_ref[0, sl, :], (((1,), (1,)), ((), ())), preferred_element_type=F32)

    bwd_order = list(range(n_ctx - 1, -1, -1)) + list(range(n - 1, n_ctx - 1, -1))
    s = None
    for pos_, idx in enumerate(bwd_order):
        if s is not None:
            sb_ref[idx] = s.astype(BF16)
        if pos_ + 1 < n:
            upd = kv(idx, kdec_b)
            s = upd if s is None else s * cdec_b + upd

    s = None
    scores = qk(0)
    for idx in range(n):
        q_ref, _, v_ref, g_ref, o_ref, sl = seq[idx]
        qf = q_ref[0, sl, :].astype(F32)
        cross = None
        if idx != bwd_order[0]:
            cross = jnp.dot((qf * qdec_b).astype(BF16), sb_ref[idx], preferred_element_type=F32)
        if s is not None:
            cf = jnp.dot((qf * qdec_f).astype(BF16), s.astype(BF16), preferred_element_type=F32)
            cross = cf if cross is None else cross + cf
        upd = kv(idx, kdec_f) if idx + 1 < n else None
        nxt = qk(idx + 1) if idx + 1 < n else None
        o = jnp.dot((scores * dmat).astype(BF16), v_ref[0, sl, :], preferred_element_type=F32)
        if cross is not None:
            o = o + cross
        if upd is not None:
            s = upd if s is None else s * cdec_f + upd
        o = o * lax.rsqrt(jnp.mean(o * o, axis=-1, keepdims=True) + NORM_EPS)
        o_ref[0, sl, :] = (g_ref[0, sl, :].astype(F32) * o).astype(BF16)
        scores = nxt


def _retention(lg, pl_, pc):
    b, l, n = pl_.shape
    lc = pc.shape[1]
    hh = RET_HEADS
    dk = n // (6 * hh)
    dv = 2 * dk
    c = RET_CHUNK

    def specs(length):
        return [
            pl.BlockSpec((1, length, dk), lambda b_, h: (b_, 0, h)),
            pl.BlockSpec((1, length, dk), lambda b_, h: (b_, 0, hh + h)),
            pl.BlockSpec((1, length, dv), lambda b_, h: (b_, 0, hh + h)),
            pl.BlockSpec((1, length, dv), lambda b_, h: (b_, 0, 2 * hh + h)),
        ]

    return pl.pallas_call(
        _ret_kernel,
        grid=(b, hh),
        in_specs=[pl.BlockSpec(memory_space=pltpu.SMEM)] + specs(l) + specs(lc),
        out_specs=[
            pl.BlockSpec((1, l, dv), lambda b_, h: (b_, 0, h)),
            pl.BlockSpec((1, lc, dv), lambda b_, h: (b_, 0, h)),
        ],
        out_shape=[jax.ShapeDtypeStruct((b, l, hh * dv), BF16), jax.ShapeDtypeStruct((b, lc, hh * dv), BF16)],
        scratch_shapes=[pltpu.VMEM((lc // c + l // c, dk, dv), BF16)],
        compiler_params=_cparams(("parallel", "arbitrary")),
        name="retention",
    )(lg, pl_, pl_, pl_, pl_, pc, pc, pc, pc)


def _rope_tables(seq, dk):
    quarter = dk // 4
    t = np.arange(seq)
    inv = ROPE_BASE ** (-np.arange(quarter, dtype=np.float64) / quarter)
    ang_r = (t // GRID_W)[:, None] * inv[None, :]
    ang_c = (t % GRID_W)[:, None] * inv[None, :]
    cos = np.concatenate([np.cos(ang_r)] * 2 + [np.cos(ang_c)] * 2, axis=-1)
    sin = np.concatenate([-np.sin(ang_r), np.sin(ang_r), -np.sin(ang_c), np.sin(ang_c)], axis=-1)
    return jnp.asarray(cos, F32), jnp.asarray(sin, F32)


def _na_window(rows):
    kh = min(NA_MAX_KH, rows)
    return kh, NA_QROWS + kh


def _na_window_start(blk, rows):
    kh, wrows = _na_window(rows)
    return min(max(blk * NA_QROWS - kh // 2, 0), rows - wrows)


def _na_kernel(m_ref, q_ref, k_ref, v_ref, kc_ref, vc_ref, o_ref, bias_ref, sl_ref, sc_ref, pl_ref, pc_ref, *, rows):
    b = pl.program_id(1)
    kh, wrows = _na_window(rows)
    qr_n = NA_QROWS
    w = GRID_W
    hd = q_ref.shape[2] // 2
    nblk = rows // qr_n

    @pl.when(b == 0)
    def _():
        qc = lax.broadcasted_iota(jnp.int32, (w, LANES), 0)
        kc = lax.broadcasted_iota(jnp.int32, (w, LANES), 1)
        cs = jnp.clip(qc - NA_KW // 2, 0, w - NA_KW)
        col_valid = (kc >= cs) & (kc < cs + NA_KW)
        toe = [[jnp.where(col_valid,
                          pltpu.roll(jnp.broadcast_to(m_ref[e, dr:dr + 1, :], (w, LANES)), 0, 1, stride=1, stride_axis=0),
                          MASK_VALUE)[:, :w]
                for dr in range(2 * kh - 1)] for e in range(2)]
        for variant in range(3):
            r0 = (0, qr_n, rows - qr_n)[variant]
            ws = _na_window_start(r0 // qr_n, rows)
            for qr in range(qr_n):
                r = r0 + qr
                rs = min(max(r - kh // 2, 0), rows - kh)
                for krr in range(wrows):
                    kr = ws + krr
                    for e in range(2):
                        dst = (e, variant, slice(qr * w, (qr + 1) * w), slice(krr * w, (krr + 1) * w))
                        if rs <= kr < rs + kh:
                            bias_ref[dst] = toe[e][kr - r + kh - 1]
                        else:
                            bias_ref[dst] = jnp.full((w, w), MASK_VALUE, F32)

    vc = vc_ref[0]
    nq = qr_n * w
    lane = lax.broadcasted_iota(jnp.int32, (nq, q_ref.shape[2]), 1)

    kc = kc_ref[0]
    trans_b = (((1,), (1,)), ((), ()))

    chains = []
    for k in range(nblk):
        variant = 0 if k == 0 else (2 if k == nblk - 1 else 1)
        ws = _na_window_start(k, rows)
        win = slice(ws * w, (ws + wrows) * w)
        q = q_ref[0, k * nq:(k + 1) * nq, :]
        for e in range(2):
            qe = jnp.where((lane >= e * hd) & (lane < (e + 1) * hd), q, jnp.zeros_like(q))
            ci = len(chains)
            sl_ref[ci] = lax.dot_general(qe, k_ref[0, win, :], trans_b, preferred_element_type=F32) + bias_ref[e, variant]
            sc_ref[ci] = lax.dot_general(qe, kc, trans_b, preferred_element_type=F32)
            chains.append((k, e, win, ci))

    def head_values(v, e):
        vlane = lax.broadcasted_iota(jnp.int32, v.shape, 1)
        return jnp.where((vlane >= e * hd) & (vlane < (e + 1) * hd), v, jnp.ones_like(v))

    vc_e = [head_values(vc, e) for e in range(2)]
    outs = {}
    for k, e, win, ci in chains:
        m = jnp.maximum(jnp.max(sl_ref[ci], axis=-1, keepdims=True), jnp.max(sc_ref[ci], axis=-1, keepdims=True))
        pl_ref[ci] = jnp.exp2(sl_ref[ci] - m).astype(BF16)
        pc_ref[ci] = jnp.exp2(sc_ref[ci] - m).astype(BF16)
        o = jnp.dot(pl_ref[ci], head_values(v_ref[0, win, :], e), preferred_element_type=F32)
        o += jnp.dot(pc_ref[ci], vc_e[e], preferred_element_type=F32)
        outs[k, e] = o / pltpu.roll(o, hd, 1)
        if e == 1:
            o_ref[0, k * nq:(k + 1) * nq, :] = jnp.where(lane < hd, outs[k, 0], outs[k, 1]).astype(o_ref.dtype)


def _na_bias_table(rpb):
    padded = jnp.pad(rpb * LOG2E, ((0, 0), (0, 0), (0, LANES - rpb.shape[2])))
    return jnp.roll(padded, -(NA_KW - 1), axis=2)


def _na_attention(qkv, kvc, rpb):
    b, s, d3 = qkv.shape
    d = d3 // 3
    lc = kvc.shape[1]
    rows = s // GRID_W
    pairs = d // LANES
    kh, wrows = _na_window(rows)
    qrows = NA_QROWS * GRID_W
    nchain = 2 * (rows // NA_QROWS)
    assert kh == 2 * NA_QROWS and rows % NA_QROWS == 0 and rows >= wrows + NA_QROWS, rows
    mtab = _na_bias_table(rpb)
    return pl.pallas_call(
        functools.partial(_na_kernel, rows=rows),
        grid=(pairs, b),
        in_specs=[
            pl.BlockSpec((2,) + mtab.shape[1:], lambda p, b_: (p, 0, 0)),
            pl.BlockSpec((1, s, LANES), lambda p, b_: (b_, 0, p)),
            pl.BlockSpec((1, s, LANES), lambda p, b_: (b_, 0, pairs + p)),
            pl.BlockSpec((1, s, LANES), lambda p, b_: (b_, 0, 2 * pairs + p)),
            pl.BlockSpec((1, lc, LANES), lambda p, b_: (b_, 0, p)),
            pl.BlockSpec((1, lc, LANES), lambda p, b_: (b_, 0, pairs + p)),
        ],
        out_specs=pl.BlockSpec((1, s, LANES), lambda p, b_: (b_, 0, p)),
        out_shape=jax.ShapeDtypeStruct((b, s, d), BF16),
        scratch_shapes=[pltpu.VMEM((2, 3, qrows, wrows * GRID_W), F32),
                        pltpu.VMEM((nchain, qrows, wrows * GRID_W), F32), pltpu.VMEM((nchain, qrows, lc), F32),
                        pltpu.VMEM((nchain, qrows, wrows * GRID_W), BF16), pltpu.VMEM((nchain, qrows, lc), BF16)],
        compiler_params=_cparams(("arbitrary", "arbitrary")),
        name="na_attention",
    )(mtab, qkv, qkv, qkv, kvc, kvc)


def kernel(x, c, ctx, c_ctx, ada_w, ada_b, norm1_w, norm2_w, ret_w_in, ret_decay_fwd, ret_decay_bwd, ret_w_out,
           na_w_qkv, na_rpb, na_w_out, ffn_w_up, ffn_conv_w, ffn_conv_b, ffn_w_down, final_norm_w):
    b, seq, d = x.shape
    depth = ada_w.shape[0]
    n_mixers = 2

    pad = BF16_SUBLANES - (b + 1) % BF16_SUBLANES
    cc = jnp.concatenate([c, c_ctx[None, :], jnp.zeros((pad, d), F32)], axis=0)
    mods = _ada(cc, ada_w, ada_b).reshape(depth, cc.shape[0], 6, d)

    def mod(i, k):
        return mods[i, :b, k][:, None, :], mods[i, b:b + 1, k][:, None, :]

    ffn_w = (ffn_w_up.astype(BF16), ffn_conv_w, ffn_conv_b[:, None, :], ffn_w_down.astype(BF16))
    for i in range(depth):
        last = i == depth - 1
        j = i // n_mixers
        (sh1, csh1), (sc1, csc1), (g1, cg1) = mod(i, 0), mod(i, 1), mod(i, 2)
        (sh2, csh2), (sc2, csc2), (g2, cg2) = mod(i, 3), mod(i, 4), mod(i, 5)
        if i % n_mixers == 0:
            hk = ret_w_in.shape[2] // 6
            dk = hk // RET_HEADS
            w_in = ret_w_in[j].astype(BF16)
            kw = dict(scale_cols=(hk, 2 * hk), scale_val=dk ** -0.5, silu_cols=(4 * hk, 6 * hk))
            pl_ = _proj(x, norm1_w[i], sh1, sc1, w_in, rope=_rope_tables(seq, dk), rope_cols=2 * hk, **kw)
            pc = _proj(ctx, norm1_w[i], csh1, csc1, w_in, **kw)
            lg = jnp.stack([-jnp.exp(ret_decay_fwd[j].astype(F32)), -jnp.exp(ret_decay_bwd[j].astype(F32))])
            y, yc = _retention(lg, pl_, pc)
            w_out = ret_w_out[j].astype(BF16)
        else:
            w_qkv = na_w_qkv[j].astype(BF16)
            qkv = _proj(x, norm1_w[i], sh1, sc1, w_qkv, scale_cols=(0, d), scale_val=(d // NA_HEADS) ** -0.5 * LOG2E)
            kvc = _proj(ctx, norm1_w[i], csh1, csc1, w_qkv[:, d:])
            if not last:
                raise NotImplementedError("context output of the attention mixer is only needed for depth > 2")
            y, yc = _na_attention(qkv, kvc, na_rpb[j]), None
            w_out = na_w_out[j].astype(BF16)
        x, h2 = _outproj(y, w_out, x, g1, norm2_w[i], sh2, sc2)
        x = _ffn(x, h2, g2, *ffn_w, final_norm_w, layer=i, final_norm=last)
        if not last:
            ctx, hc2 = _outproj(yc, w_out, ctx, cg1, norm2_w[i], csh2, csc2)
            ctx = _ffn(ctx, hc2, cg2, *ffn_w, final_norm_w, layer=i, final_norm=False)
    return x
```

```python
import functools

import jax
import jax.numpy as jnp
import numpy as np
from jax import lax
from jax.experimental import pallas as pl
from jax.experimental.pallas import tpu as pltpu

F32 = jnp.float32
BF16 = jnp.bfloat16

GRID_W = 64
RET_HEADS = 4
ROPE_BASE = 10000.0
NA_HEADS = 16
NA_MAX_KH = 8
NA_KW = 16
CONV_W = 3
NORM_EPS = 1e-6

LANES = 128
SUBLANES = 8
BF16_SUBLANES = 16
VMEM_LIMIT_BYTES = 56 * 1024 * 1024

ADA_COL_TILE = 3072
ROW_TILE = 1024
PROJ_ROW_TILES = (1024, 512, 256)
PROJ_VMEM_BUDGET = 40 * 1024 * 1024
MM_SUB = 256
RET_CHUNK = 256
NA_QROWS = 4
MASK_VALUE = -1e30
LOG2E = 1.4426950408889634


def _cparams(sem):
    return pltpu.CompilerParams(dimension_semantics=sem, vmem_limit_bytes=VMEM_LIMIT_BYTES)


def _bsel(arr):
    if arr.shape[0] == 1:
        return lambda b, *_: (0, 0, 0)
    return lambda b, *_: (b, 0, 0)


def _norm_mod(x, nw, shift, scale):
    y = x * lax.rsqrt(jnp.mean(x * x, axis=-1, keepdims=True) + NORM_EPS)
    return (y * nw) * (1.0 + scale) + shift


def _silu(x):
    return x / (1.0 + jnp.exp(-x))


def _ada_kernel(c_ref, w_ref, b_ref, o_ref):
    s = _silu(c_ref[...])
    o_ref[0] = jnp.dot(s, w_ref[0], preferred_element_type=F32) + b_ref[0]


def _ada(cc, ada_w, ada_b):
    depth, d, n = ada_w.shape
    tn = ADA_COL_TILE
    return pl.pallas_call(
        _ada_kernel,
        grid=(depth, n // tn),
        in_specs=[
            pl.BlockSpec(cc.shape, lambda i, j: (0, 0)),
            pl.BlockSpec((1, d, tn), lambda i, j: (i, 0, j)),
            pl.BlockSpec((1, 1, tn), lambda i, j: (i, 0, j)),
        ],
        out_specs=pl.BlockSpec((1, cc.shape[0], tn), lambda i, j: (i, 0, j)),
        out_shape=jax.ShapeDtypeStruct((depth, cc.shape[0], n), F32),
        compiler_params=_cparams(("parallel", "parallel")),
        name="ada",
    )(cc, ada_w, ada_b.reshape(depth, 1, n))


def _pipelined_units(units, matmul, epilogue):
    acc = matmul(units[0])
    for n, u in enumerate(units):
        nxt = matmul(units[n + 1]) if n + 1 < len(units) else None
        epilogue(acc, u)
        acc = nxt


def _proj_kernel(*refs, tn, rope_cols, scale_cols, scale_val, silu_cols):
    if rope_cols:
        x_ref, nw_ref, sh_ref, sc_ref, w_ref, cos_ref, sin_ref, o_ref, h_ref = refs
    else:
        x_ref, nw_ref, sh_ref, sc_ref, w_ref, o_ref, h_ref = refs
    tm = x_ref.shape[1]
    sub = min(MM_SUB, tm)

    def matmul(u):
        s, j = u
        rows = slice(s * sub, (s + 1) * sub)
        if j == 0:
            h_ref[rows] = _norm_mod(x_ref[0, rows], nw_ref[...], sh_ref[0], sc_ref[0]).astype(BF16)
        return jnp.dot(h_ref[rows], w_ref[:, j * tn:(j + 1) * tn], preferred_element_type=F32)

    def epilogue(acc, u):
        s, j = u
        rows = slice(s * sub, (s + 1) * sub)
        col = j * tn
        if scale_cols[0] <= col < scale_cols[1]:
            acc = acc * scale_val
        if silu_cols[0] <= col < silu_cols[1]:
            acc = _silu(acc)
        if col >= rope_cols:
            o_ref[0, rows, col:col + tn] = acc.astype(o_ref.dtype)
            return
        for g in range(tn // LANES):
            part = acc[:, g * LANES:(g + 1) * LANES]
            t = (g % 2) * LANES
            rot = part * cos_ref[rows, t:t + LANES] + pltpu.roll(part, LANES // 2, 1) * sin_ref[rows, t:t + LANES]
            o_ref[0, rows, col + g * LANES:col + (g + 1) * LANES] = rot.astype(o_ref.dtype)

    units = [(s, j) for s in range(tm // sub) for j in range(w_ref.shape[1] // tn)]
    _pipelined_units(units, matmul, epilogue)


def _proj_row_tile(l, d, n, rope_width):
    for tm in PROJ_ROW_TILES:
        if tm > l or l % tm:
            continue
        blocks = 2 * tm * d * 4 + 2 * tm * n * 2 + 4 * tm * rope_width * 4
        if d * n * 2 + blocks + tm * d * 2 <= PROJ_VMEM_BUDGET:
            return tm
    raise ValueError((l, d, n))


def _proj(x, nw, shift, scale, w, *, tn=512, rope=None, rope_cols=0, scale_cols=(0, 0), scale_val=1.0, silu_cols=(0, 0)):
    b, l, d = x.shape
    n = w.shape[1]
    tm = _proj_row_tile(l, d, n, rope[0].shape[1] if rope is not None else 0)
    assert all(v % tn == 0 for v in (n, rope_cols) + tuple(scale_cols) + tuple(silu_cols))
    in_specs = [
        pl.BlockSpec((1, tm, d), lambda b_, i: (b_, i, 0)),
        pl.BlockSpec((1, d), lambda b_, i: (0, 0)),
        pl.BlockSpec((1, 1, d), _bsel(shift)),
        pl.BlockSpec((1, 1, d), _bsel(scale)),
        pl.BlockSpec((d, n), lambda b_, i: (0, 0), pipeline_mode=pl.Buffered(1)),
    ]
    args = [x, nw.reshape(1, d), shift, scale, w]
    if rope is not None:
        cos, sin = rope
        in_specs += [pl.BlockSpec((tm, cos.shape[1]), lambda b_, i: (i, 0))] * 2
        args += [cos, sin]
    kern = functools.partial(_proj_kernel, tn=tn, rope_cols=(rope_cols if rope is not None else 0),
                             scale_cols=scale_cols, scale_val=scale_val, silu_cols=silu_cols)
    return pl.pallas_call(
        kern,
        grid=(b, l // tm),
        in_specs=in_specs,
        out_specs=pl.BlockSpec((1, tm, n), lambda b_, i: (b_, i, 0)),
        out_shape=jax.ShapeDtypeStruct((b, l, n), BF16),
        scratch_shapes=[pltpu.VMEM((tm, d), BF16)],
        compiler_params=_cparams(("parallel", "parallel")),
        name="proj",
    )(*args)


def _interleave_rows(v):
    sub, d = v.shape
    return v.reshape(SUBLANES, sub // SUBLANES, d).swapaxes(0, 1).reshape(sub, d)


def _deinterleave_rows(v):
    sub, d = v.shape
    return v.reshape(sub // SUBLANES, SUBLANES, d).swapaxes(0, 1).reshape(sub, d)


def _outproj_kernel(a_ref, w_ref, res_ref, gate_ref, nw_ref, sh_ref, sc_ref, o_ref, h_ref, *, tn):
    tm = a_ref.shape[1]
    sub = min(FFN_SUB, tm)
    nj = w_ref.shape[1] // tn

    def matmul(u):
        s, j = u
        return jnp.dot(a_ref[0, s * sub:(s + 1) * sub], w_ref[:, j * tn:(j + 1) * tn], preferred_element_type=F32)

    def epilogue(acc, u):
        s, j = u
        rows, cols = slice(s * sub, (s + 1) * sub), slice(j * tn, (j + 1) * tn)
        o_ref[0, rows, cols] = res_ref[0, rows, cols] + gate_ref[0, :, cols] * acc
        if j == nj - 1:
            h_ref[0, rows, :] = _norm_mod(o_ref[0, rows, :], nw_ref[...], sh_ref[0], sc_ref[0]).astype(h_ref.dtype)

    units = [(s, j) for s in range(tm // sub) for j in range(nj)]
    _pipelined_units(units, matmul, epilogue)


def _outproj(a, w, res, gate, nw, shift, scale, *, tn=512):
    b, l, k = a.shape
    n = w.shape[1]
    tm = min(ROW_TILE, l)
    assert tm % min(FFN_SUB, tm) == 0 and n % tn == 0
    vec = lambda arr: pl.BlockSpec((1, 1, n), _bsel(arr))
    return pl.pallas_call(
        functools.partial(_outproj_kernel, tn=tn),
        grid=(b, l // tm),
        in_specs=[
            pl.BlockSpec((1, tm, k), lambda b_, i: (b_, i, 0)),
            pl.BlockSpec((k, n), lambda b_, i: (0, 0), pipeline_mode=pl.Buffered(1)),
            pl.BlockSpec((1, tm, n), lambda b_, i: (b_, i, 0)),
            vec(gate),
            pl.BlockSpec((1, n), lambda b_, i: (0, 0)),
            vec(shift), vec(scale),
        ],
        out_specs=[pl.BlockSpec((1, tm, n), lambda b_, i: (b_, i, 0))] * 2,
        out_shape=[jax.ShapeDtypeStruct((b, l, n), F32), jax.ShapeDtypeStruct((b, l, n), BF16)],
        compiler_params=_cparams(("parallel", "parallel")),
        name="outproj",
    )(a, w, res, gate, nw.reshape(1, n), shift, scale)


FFN_TILE = 256
FFN_HALO = BF16_SUBLANES
FFN_SUB = 256
FFN_ROW_TILE = 1024


def _ffn_kernel(x_ref, hm_ref, hp_ref, hn_ref, g_ref, wu_ref, cw_ref, cb_ref, wd_ref, fw_ref,
                o_ref, h_ref, ua_ref, ug_ref, act_ref, *, final_norm):
    i = pl.program_id(1)
    tm = x_ref.shape[1]
    hl = FFN_HALO
    tf = FFN_TILE
    f = wd_ref.shape[0]
    sub = min(FFN_SUB, tm)
    nsub = tm // sub
    chunks = [(hl * (c > 0) + c * sub, hl + (c + 1) * sub + hl * (c == nsub - 1)) for c in range(nsub)]

    def stage(c):
        if c == 0:
            h_ref[0:hl] = jnp.where(i > 0, hp_ref[0], jnp.zeros_like(hp_ref[0]))
        if c == nsub - 1:
            h_ref[hl + tm:2 * hl + tm] = jnp.where(i < pl.num_programs(1) - 1, hn_ref[0], jnp.zeros_like(hn_ref[0]))
        h_ref[hl + c * sub:hl + (c + 1) * sub] = _interleave_rows(hm_ref[0, c * sub:(c + 1) * sub].astype(F32)).astype(BF16)

    def conv(u_ref, slot, col, s):
        base = hl + s * sub
        st = SUBLANES
        cw = cw_ref[:, col:col + tf]
        blk = u_ref[slot, base - st:base + sub + st]
        sublane = lax.broadcasted_iota(jnp.int32, (st, tf), 0)
        first_prev = jnp.where(sublane == 0, pltpu.roll(blk[0:st], 1, 0), pltpu.roll(blk[sub:sub + st], 1, 0))
        last_next = jnp.where(sublane == st - 1, pltpu.roll(blk[sub + st:sub + 2 * st], st - 1, 0),
                              pltpu.roll(blk[st:2 * st], st - 1, 0))
        prev = jnp.concatenate([first_prev, blk[st:sub]], axis=0)
        nxt = jnp.concatenate([blk[2 * st:sub + st], last_next], axis=0)
        return (prev * cw[0:1] + blk[st:sub + st] * cw[1:2] + nxt * cw[2:3]) + cb_ref[:, col:col + tf]

    def up(t, c):
        lo, hi = chunks[c]
        if t == 0:
            stage(c)
        ua_ref[t % 2, lo:hi] = jnp.dot(h_ref[lo:hi], wu_ref[:, t * tf:(t + 1) * tf], preferred_element_type=F32)
        ug_ref[t % 2, lo:hi] = jnp.dot(h_ref[lo:hi], wu_ref[:, f + t * tf:f + (t + 1) * tf], preferred_element_type=F32)

    units = [(t, c) for t in range(f // tf) for c in range(nsub)]
    up(*units[0])
    for n, (t, s) in enumerate(units):
        if n + 1 < len(units):
            up(*units[n + 1])
        act = _silu(conv(ua_ref, t % 2, t * tf, s)) * conv(ug_ref, t % 2, f + t * tf, s)
        act_ref[s * sub:(s + 1) * sub, t * tf:(t + 1) * tf] = act.astype(BF16)

    for s in range(nsub):
        rows = slice(s * sub, (s + 1) * sub)
        down = _deinterleave_rows(jnp.dot(act_ref[rows], wd_ref[...], preferred_element_type=F32))
        y = x_ref[0, rows] + g_ref[0] * down
        if final_norm:
            y = (y * lax.rsqrt(jnp.mean(y * y, axis=-1, keepdims=True) + NORM_EPS)) * fw_ref[...]
        o_ref[0, rows] = y


def _ffn(x, h, gate, wu, cw, cb, wd, final_w, *, layer, final_norm):
    b, l, d = x.shape
    f = wd.shape[1]
    tf = FFN_TILE
    assert f % tf == 0, wd.shape
    tm = min(FFN_ROW_TILE, l)
    hl = FFN_HALO
    per = tm // hl
    nblk = l // hl
    whole = lambda arr: pl.BlockSpec((None,) + arr.shape[1:], lambda b_, i: (layer, 0, 0), pipeline_mode=pl.Buffered(1))
    return pl.pallas_call(
        functools.partial(_ffn_kernel, final_norm=final_norm),
        grid=(b, l // tm),
        in_specs=[
            pl.BlockSpec((1, tm, d), lambda b_, i: (b_, i, 0)),
            pl.BlockSpec((1, tm, d), lambda b_, i: (b_, i, 0)),
            pl.BlockSpec((1, hl, d), lambda b_, i: (b_, jnp.maximum(i * per - 1, 0), 0)),
            pl.BlockSpec((1, hl, d), lambda b_, i: (b_, jnp.minimum((i + 1) * per, nblk - 1), 0)),
            pl.BlockSpec((1, 1, d), _bsel(gate)),
            whole(wu), whole(cw), whole(cb), whole(wd),
            pl.BlockSpec((1, d), lambda b_, i: (0, 0)),
        ],
        out_specs=pl.BlockSpec((1, tm, d), lambda b_, i: (b_, i, 0)),
        out_shape=jax.ShapeDtypeStruct((b, l, d), F32),
        scratch_shapes=[pltpu.VMEM((tm + 2 * hl, d), BF16), pltpu.VMEM((2, tm + 2 * hl, tf), F32),
                        pltpu.VMEM((2, tm + 2 * hl, tf), F32), pltpu.VMEM((tm, f), BF16)],
        compiler_params=_cparams(("parallel", "parallel")),
        name="ffn",
    )(x, h, h, h, gate, wu, cw, cb, wd, final_w.reshape(1, d))


def _ret_kernel(lg_ref, ql_ref, kl_ref, vl_ref, gl_ref, qc_ref, kc_ref, vc_ref, gc_ref, ol_ref, oc_ref, sb_ref):
    c = RET_CHUNK
    h = pl.program_id(1)
    lgf = lg_ref[0, h]
    lgb = lg_ref[1, h]
    n_ctx = qc_ref.shape[1] // c
    n_lat = ql_ref.shape[1] // c

    pos = lax.broadcasted_iota(jnp.int32, (c, 1), 0).astype(F32)
    kdec_f = jnp.exp(lgf * (c - 1.0 - pos))
    kdec_b = jnp.exp(lgb * pos)
    qdec_f = jnp.exp(lgf * (pos + 1.0))
    qdec_b = jnp.exp(lgb * (c - pos))
    one = jnp.ones((1, 1), F32)
    cdec_f = jnp.exp(lgf * c * one)
    cdec_b = jnp.exp(lgb * c * one)
    diff = (lax.broadcasted_iota(jnp.int32, (c, c), 0) - lax.broadcasted_iota(jnp.int32, (c, c), 1)).astype(F32)
    dmat = jnp.exp(jnp.where(diff >= 0, lgf, lgb) * jnp.abs(diff))

    seq = [(qc_ref, kc_ref, vc_ref, gc_ref, oc_ref, slice(t * c, (t + 1) * c)) for t in range(n_ctx)]
    seq += [(ql_ref, kl_ref, vl_ref, gl_ref, ol_ref, slice(t * c, (t + 1) * c)) for t in range(n_lat)]
    n = len(seq)

    def kv(idx, kdec):
        _, k_ref, v_ref, _, _, sl = seq[idx]
        kd = (k_ref[0, sl, :].astype(F32) * kdec).astype(BF16)
        return lax.dot_general(kd, v_ref[0, sl, :], (((0,), (0,)), ((), ())), preferred_element_type=F32)

    def qk(idx):
        q_ref, k_ref, _, _, _, sl = seq[idx]
        return lax.dot_general(q_ref[0, sl, :], k_ref[0, sl, :], (((1,), (1,)), ((), ())), preferred_element_type=F32)

    bwd_order = list(range(n_ctx - 1, -1, -1)) + list(range(n - 1, n_ctx - 1, -1))
    s = None
    for pos_, idx in enumerate(bwd_order):
        if s is not None:
            sb_ref[idx] = s.astype(BF16)
        if pos_ + 1 < n:
            upd = kv(idx, kdec_b)
            s = upd if s is None else s * cdec_b + upd

    s = None
    scores = qk(0)
    for idx in range(n):
        q_ref, _, v_ref, g_ref, o_ref, sl = seq[idx]
        qf = q_ref[0, sl, :].astype(F32)
        cross = None
        if idx != bwd_order[0]:
            cross = jnp.dot((qf * qdec_b).astype(BF16), sb_ref[idx], preferred_element_type=F32)
        if s is not None:
            cf = jnp.dot((qf * qdec_f).astype(BF16), s.astype(BF16), preferred_element_type=F32)
            cross = cf if cross is None else cross + cf
        upd = kv(idx, kdec_f) if idx + 1 < n else None
        nxt = qk(idx + 1) if idx + 1 < n else None
        o = jnp.dot((scores * dmat).astype(BF16), v_ref[0, sl, :], preferred_element_type=F32)
        if cross is not None:
            o = o + cross
        if upd is not None:
            s = upd if s is None else s * cdec_f + upd
        o = o * lax.rsqrt(jnp.mean(o * o, axis=-1, keepdims=True) + NORM_EPS)
        o_ref[0, sl, :] = (g_ref[0, sl, :].astype(F32) * o).astype(BF16)
        scores = nxt


def _retention(lg, pl_, pc):
    b, l, n = pl_.shape
    lc = pc.shape[1]
    hh = RET_HEADS
    dk = n // (6 * hh)
    dv = 2 * dk
    c = RET_CHUNK

    def specs(length):
        return [
            pl.BlockSpec((1, length, dk), lambda b_, h: (b_, 0, h)),
            pl.BlockSpec((1, length, dk), lambda b_, h: (b_, 0, hh + h)),
            pl.BlockSpec((1, length, dv), lambda b_, h: (b_, 0, hh + h)),
            pl.BlockSpec((1, length, dv), lambda b_, h: (b_, 0, 2 * hh + h)),
        ]

    return pl.pallas_call(
        _ret_kernel,
        grid=(b, hh),
        in_specs=[pl.BlockSpec(memory_space=pltpu.SMEM)] + specs(l) + specs(lc),
        out_specs=[
            pl.BlockSpec((1, l, dv), lambda b_, h: (b_, 0, h)),
            pl.BlockSpec((1, lc, dv), lambda b_, h: (b_, 0, h)),
        ],
        out_shape=[jax.ShapeDtypeStruct((b, l, hh * dv), BF16), jax.ShapeDtypeStruct((b, lc, hh * dv), BF16)],
        scratch_shapes=[pltpu.VMEM((lc // c + l // c, dk, dv), BF16)],
        compiler_params=_cparams(("parallel", "arbitrary")),
        name="retention",
    )(lg, pl_, pl_, pl_, pl_, pc, pc, pc, pc)


def _rope_tables(seq, dk):
    quarter = dk // 4
    t = np.arange(seq)
    inv = ROPE_BASE ** (-np.arange(quarter, dtype=np.float64) / quarter)
    ang_r = (t // GRID_W)[:, None] * inv[None, :]
    ang_c = (t % GRID_W)[:, None] * inv[None, :]
    cos = np.concatenate([np.cos(ang_r)] * 2 + [np.cos(ang_c)] * 2, axis=-1)
    sin = np.concatenate([-np.sin(ang_r), np.sin(ang_r), -np.sin(ang_c), np.sin(ang_c)], axis=-1)
    return jnp.asarray(cos, F32), jnp.asarray(sin, F32)


def _na_window(rows):
    kh = min(NA_MAX_KH, rows)
    return kh, NA_QROWS + kh


def _na_window_start(blk, rows):
    kh, wrows = _na_window(rows)
    return min(max(blk * NA_QROWS - kh // 2, 0), rows - wrows)


def _na_kernel(m_ref, q_ref, k_ref, v_ref, kc_ref, vc_ref, o_ref, bias_ref, sl_ref, sc_ref, pl_ref, pc_ref, *, rows):
    b = pl.program_id(1)
    kh, wrows = _na_window(rows)
    qr_n = NA_QROWS
    w = GRID_W
    hd = q_ref.shape[2] // 2
    nblk = rows // qr_n

    @pl.when(b == 0)
    def _():
        qc = lax.broadcasted_iota(jnp.int32, (w, LANES), 0)
        kc = lax.broadcasted_iota(jnp.int32, (w, LANES), 1)
        cs = jnp.clip(qc - NA_KW // 2, 0, w - NA_KW)
        col_valid = (kc >= cs) & (kc < cs + NA_KW)
        toe = [[jnp.where(col_valid,
                          pltpu.roll(jnp.broadcast_to(m_ref[e, dr:dr + 1, :], (w, LANES)), 0, 1, stride=1, stride_axis=0),
                          MASK_VALUE)[:, :w]
                for dr in range(2 * kh - 1)] for e in range(2)]
        for variant in range(3):
            r0 = (0, qr_n, rows - qr_n)[variant]
            ws = _na_window_start(r0 // qr_n, rows)
            for qr in range(qr_n):
                r = r0 + qr
                rs = min(max(r - kh // 2, 0), rows - kh)
                for krr in range(wrows):
                    kr = ws + krr
                    for e in range(2):
                        dst = (e, variant, slice(qr * w, (qr + 1) * w), slice(krr * w, (krr + 1) * w))
                        if rs <= kr < rs + kh:
                            bias_ref[dst] = toe[e][kr - r + kh - 1]
                        else:
                            bias_ref[dst] = jnp.full((w, w), MASK_VALUE, F32)

    vc = vc_ref[0]
    nq = qr_n * w
    lane = lax.broadcasted_iota(jnp.int32, (nq, q_ref.shape[2]), 1)

    kc = kc_ref[0]
    trans_b = (((1,), (1,)), ((), ()))

    chains = []
    for k in range(nblk):
        variant = 0 if k == 0 else (2 if k == nblk - 1 else 1)
        ws = _na_window_start(k, rows)
        win = slice(ws * w, (ws + wrows) * w)
        q = q_ref[0, k * nq:(k + 1) * nq, :]
        for e in range(2):
            qe = jnp.where((lane >= e * hd) & (lane < (e + 1) * hd), q, jnp.zeros_like(q))
            ci = len(chains)
            sl_ref[ci] = lax.dot_general(qe, k_ref[0, win, :], trans_b, preferred_element_type=F32) + bias_ref[e, variant]
            sc_ref[ci] = lax.dot_general(qe, kc, trans_b, preferred_element_type=F32)
            chains.append((k, e, win, ci))

    def head_values(v, e):
        vlane = lax.broadcasted_iota(jnp.int32, v.shape, 1)
        return jnp.where((vlane >= e * hd) & (vlane < (e + 1) * hd), v, jnp.ones_like(v))

    vc_e = [head_values(vc, e) for e in range(2)]
    outs = {}
    for k, e, win, ci in chains:
        m = jnp.maximum(jnp.max(sl_ref[ci], axis=-1, keepdims=True), jnp.max(sc_ref[ci], axis=-1, keepdims=True))
        pl_ref[ci] = jnp.exp2(sl_ref[ci] - m).astype(BF16)
        pc_ref[ci] = jnp.exp2(sc_ref[ci] - m).astype(BF16)
        o = jnp.dot(pl_ref[ci], head_values(v_ref[0, win, :], e), preferred_element_type=F32)
        o += jnp.dot(pc_ref[ci], vc_e[e], preferred_element_type=F32)
        outs[k, e] = o / pltpu.roll(o, hd, 1)
        if e == 1:
            o_ref[0, k * nq:(k + 1) * nq, :] = jnp.where(lane < hd, outs[k, 0], outs[k, 1]).astype(o_ref.dtype)


def _na_bias_table(rpb):
    padded = jnp.pad(rpb * LOG2E, ((0, 0), (0, 0), (0, LANES - rpb.shape[2])))
    return jnp.roll(padded, -(NA_KW - 1), axis=2)


def _na_attention(qkv, kvc, rpb):
    b, s, d3 = qkv.shape
    d = d3 // 3
    lc = kvc.shape[1]
    rows = s // GRID_W
    pairs = d // LANES
    kh, wrows = _na_window(rows)
    qrows = NA_QROWS * GRID_W
    nchain = 2 * (rows // NA_QROWS)
    assert kh == 2 * NA_QROWS and rows % NA_QROWS == 0 and rows >= wrows + NA_QROWS, rows
    mtab = _na_bias_table(rpb)
    return pl.pallas_call(
        functools.partial(_na_kernel, rows=rows),
        grid=(pairs, b),
        in_specs=[
            pl.BlockSpec((2,) + mtab.shape[1:], lambda p, b_: (p, 0, 0)),
            pl.BlockSpec((1, s, LANES), lambda p, b_: (b_, 0, p)),
            pl.BlockSpec((1, s, LANES), lambda p, b_: (b_, 0, pairs + p)),
            pl.BlockSpec((1, s, LANES), lambda p, b_: (b_, 0, 2 * pairs + p)),
            pl.BlockSpec((1, lc, LANES), lambda p, b_: (b_, 0, p)),
            pl.BlockSpec((1, lc, LANES), lambda p, b_: (b_, 0, pairs + p)),
        ],
        out_specs=pl.BlockSpec((1, s, LANES), lambda p, b_: (b_, 0, p)),
        out_shape=jax.ShapeDtypeStruct((b, s, d), BF16),
        scratch_shapes=[pltpu.VMEM((2, 3, qrows, wrows * GRID_W), F32),
                        pltpu.VMEM((nchain, qrows, wrows * GRID_W), F32), pltpu.VMEM((nchain, qrows, lc), F32),
                        pltpu.VMEM((nchain, qrows, wrows * GRID_W), BF16), pltpu.VMEM((nchain, qrows, lc), BF16)],
        compiler_params=_cparams(("arbitrary", "arbitrary")),
        name="na_attention",
    )(mtab, qkv, qkv, qkv, kvc, kvc)


def kernel(x, c, ctx, c_ctx, ada_w, ada_b, norm1_w, norm2_w, ret_w_in, ret_decay_fwd, ret_decay_bwd, ret_w_out,
           na_w_qkv, na_rpb, na_w_out, ffn_w_up, ffn_conv_w, ffn_conv_b, ffn_w_down, final_norm_w):
    b, seq, d = x.shape
    depth = ada_w.shape[0]
    n_mixers = 2

    pad = BF16_SUBLANES - (b + 1) % BF16_SUBLANES
    cc = jnp.concatenate([c, c_ctx[None, :], jnp.zeros((pad, d), F32)], axis=0)
    mods = _ada(cc, ada_w, ada_b).reshape(depth, cc.shape[0], 6, d)

    def mod(i, k):
        return mods[i, :b, k][:, None, :], mods[i, b:b + 1, k][:, None, :]

    ffn_w = (ffn_w_up.astype(BF16), ffn_conv_w, ffn_conv_b[:, None, :], ffn_w_down.astype(BF16))
    for i in range(depth):
        last = i == depth - 1
        j = i // n_mixers
        (sh1, csh1), (sc1, csc1), (g1, cg1) = mod(i, 0), mod(i, 1), mod(i, 2)
        (sh2, csh2), (sc2, csc2), (g2, cg2) = mod(i, 3), mod(i, 4), mod(i, 5)
        if i % n_mixers == 0:
            hk = ret_w_in.shape[2] // 6
            dk = hk // RET_HEADS
            w_in = ret_w_in[j].astype(BF16)
            kw = dict(scale_cols=(hk, 2 * hk), scale_val=dk ** -0.5, silu_cols=(4 * hk, 6 * hk))
            pl_ = _proj(x, norm1_w[i], sh1, sc1, w_in, rope=_rope_tables(seq, dk), rope_cols=2 * hk, **kw)
            pc = _proj(ctx, norm1_w[i], csh1, csc1, w_in, **kw)
            lg = jnp.stack([-jnp.exp(ret_decay_fwd[j].astype(F32)), -jnp.exp(ret_decay_bwd[j].astype(F32))])
            y, yc = _retention(lg, pl_, pc)
            w_out = ret_w_out[j].astype(BF16)
        else:
            w_qkv = na_w_qkv[j].astype(BF16)
            qkv = _proj(x, norm1_w[i], sh1, sc1, w_qkv, scale_cols=(0, d), scale_val=(d // NA_HEADS) ** -0.5 * LOG2E)
            kvc = _proj(ctx, norm1_w[i], csh1, csc1, w_qkv[:, d:])
            if not last:
                raise NotImplementedError("context output of the attention mixer is only needed for depth > 2")
            y, yc = _na_attention(qkv, kvc, na_rpb[j]), None
            w_out = na_w_out[j].astype(BF16)
        x, h2 = _outproj(y, w_out, x, g1, norm2_w[i], sh2, sc2)
        x = _ffn(x, h2, g2, *ffn_w, final_norm_w, layer=i, final_norm=last)
        if not last:
            ctx, hc2 = _outproj(yc, w_out, ctx, cg1, norm2_w[i], csh2, csc2)
            ctx = _ffn(ctx, hc2, cg2, *ffn_w, final_norm_w, layer=i, final_norm=False)
    return x
```

```python
import functools

import jax
import jax.numpy as jnp
import numpy as np
from jax import lax
from jax.experimental import pallas as pl
from jax.experimental.pallas import tpu as pltpu

F32 = jnp.float32
BF16 = jnp.bfloat16

GRID_W = 64
RET_HEADS = 4
ROPE_BASE = 10000.0
NA_HEADS = 16
NA_MAX_KH = 8
NA_KW = 16
CONV_W = 3
NORM_EPS = 1e-6

LANES = 128
SUBLANES = 8
BF16_SUBLANES = 16
VMEM_LIMIT_BYTES = 56 * 1024 * 1024

ADA_COL_TILE = 3072
ROW_TILE = 1024
PROJ_ROW_TILES = (1024, 512, 256)
PROJ_VMEM_BUDGET = 40 * 1024 * 1024
MM_SUB = 256
RET_CHUNK = 256
NA_QROWS = 4
MASK_VALUE = -1e30
LOG2E = 1.4426950408889634


def _cparams(sem):
    return pltpu.CompilerParams(dimension_semantics=sem, vmem_limit_bytes=VMEM_LIMIT_BYTES)


def _bsel(arr):
    if arr.shape[0] == 1:
        return lambda b, *_: (0, 0, 0)
    return lambda b, *_: (b, 0, 0)


def _norm_mod(x, nw, shift, scale):
    y = x * lax.rsqrt(jnp.mean(x * x, axis=-1, keepdims=True) + NORM_EPS)
    return (y * nw) * (1.0 + scale) + shift


def _silu(x):
    return x / (1.0 + jnp.exp(-x))


def _ada_kernel(c_ref, w_ref, b_ref, ci_ref, o_ref, co_ref):
    s = _silu(c_ref[...])
    o_ref[0] = jnp.dot(s, w_ref[0], preferred_element_type=F32) + b_ref[0]
    co_ref[...] = ci_ref[...].astype(co_ref.dtype)


def _ada(cc, ada_w, ada_b, cast):
    depth, d, n = ada_w.shape
    tn = ADA_COL_TILE
    nj = n // tn
    cast2d = cast.reshape(-1, cast.shape[-1])
    assert cast2d.shape[0] % (depth * nj * BF16_SUBLANES) == 0, cast.shape
    slab = pl.BlockSpec((cast2d.shape[0] // (depth * nj), cast2d.shape[1]), lambda i, j: (i * nj + j, 0))
    mods, converted = pl.pallas_call(
        _ada_kernel,
        grid=(depth, nj),
        in_specs=[
            pl.BlockSpec(cc.shape, lambda i, j: (0, 0)),
            pl.BlockSpec((1, d, tn), lambda i, j: (i, 0, j)),
            pl.BlockSpec((1, 1, tn), lambda i, j: (i, 0, j)),
            slab,
        ],
        out_specs=[pl.BlockSpec((1, cc.shape[0], tn), lambda i, j: (i, 0, j)), slab],
        out_shape=[jax.ShapeDtypeStruct((depth, cc.shape[0], n), F32), jax.ShapeDtypeStruct(cast2d.shape, BF16)],
        compiler_params=_cparams(("parallel", "parallel")),
        name="ada",
    )(cc, ada_w, ada_b.reshape(depth, 1, n), cast2d)
    return mods, converted.reshape(cast.shape)


def _pipelined_units(units, matmul, epilogue):
    acc = matmul(units[0])
    for n, u in enumerate(units):
        nxt = matmul(units[n + 1]) if n + 1 < len(units) else None
        epilogue(acc, u)
        acc = nxt


def _proj_kernel(*refs, tn, rope_cols, scale_cols, scale_val, silu_cols, n_cast):
    n_in = 7 if rope_cols else 5
    cast_in, cast_out = refs[n_in:n_in + n_cast], refs[n_in + n_cast + 1:n_in + 2 * n_cast + 1]
    refs = refs[:n_in] + (refs[n_in + n_cast],) + refs[n_in + 2 * n_cast + 1:]
    if rope_cols:
        x_ref, nw_ref, sh_ref, sc_ref, w_ref, cos_ref, sin_ref, o_ref, h_ref = refs
    else:
        x_ref, nw_ref, sh_ref, sc_ref, w_ref, o_ref, h_ref = refs
    tm = x_ref.shape[1]
    sub = min(MM_SUB, tm)

    def matmul(u):
        s, j = u
        rows = slice(s * sub, (s + 1) * sub)
        if j == 0:
            h_ref[rows] = _norm_mod(x_ref[0, rows], nw_ref[...], sh_ref[0], sc_ref[0]).astype(BF16)
        return jnp.dot(h_ref[rows], w_ref[:, j * tn:(j + 1) * tn], preferred_element_type=F32)

    def epilogue(acc, u):
        s, j = u
        if u == (0, 1):
            for src, dst in zip(cast_in, cast_out):
                dst[...] = src[...].astype(dst.dtype)
        rows = slice(s * sub, (s + 1) * sub)
        col = j * tn
        if scale_cols[0] <= col < scale_cols[1]:
            acc = acc * scale_val
        if silu_cols[0] <= col < silu_cols[1]:
            acc = _silu(acc)
        if col >= rope_cols:
            o_ref[0, rows, col:col + tn] = acc.astype(o_ref.dtype)
            return
        for g in range(tn // LANES):
            part = acc[:, g * LANES:(g + 1) * LANES]
            t = (g % 2) * LANES
            rot = part * cos_ref[rows, t:t + LANES] + pltpu.roll(part, LANES // 2, 1) * sin_ref[rows, t:t + LANES]
            o_ref[0, rows, col + g * LANES:col + (g + 1) * LANES] = rot.astype(o_ref.dtype)

    units = [(s, j) for s in range(tm // sub) for j in range(w_ref.shape[1] // tn)]
    _pipelined_units(units, matmul, epilogue)


def _proj_row_tile(l, d, n, rope_width):
    for tm in PROJ_ROW_TILES:
        if tm > l or l % tm:
            continue
        blocks = 2 * tm * d * 4 + 2 * tm * n * 2 + 4 * tm * rope_width * 4
        if d * n * 2 + blocks + tm * d * 2 <= PROJ_VMEM_BUDGET:
            return tm
    raise ValueError((l, d, n))


def _proj(x, nw, shift, scale, w, *, tn=512, rope=None, rope_cols=0, scale_cols=(0, 0), scale_val=1.0, silu_cols=(0, 0),
          cast=()):
    b, l, d = x.shape
    n = w.shape[1]
    tm = _proj_row_tile(l, d, n, rope[0].shape[1] if rope is not None else 0)
    assert all(v % tn == 0 for v in (n, rope_cols) + tuple(scale_cols) + tuple(silu_cols))
    steps = b * (l // tm)
    cast2d = [a.reshape(-1, a.shape[-1]) for a in cast]
    assert all(a.shape[0] % (steps * BF16_SUBLANES) == 0 for a in cast2d), [a.shape for a in cast2d]
    assert not cast or n // tn >= 2
    slab = lambda a: pl.BlockSpec((a.shape[0] // steps, a.shape[1]), lambda b_, i: (b_ * (l // tm) + i, 0))
    in_specs = [
        pl.BlockSpec((1, tm, d), lambda b_, i: (b_, i, 0)),
        pl.BlockSpec((1, d), lambda b_, i: (0, 0)),
        pl.BlockSpec((1, 1, d), _bsel(shift)),
        pl.BlockSpec((1, 1, d), _bsel(scale)),
        pl.BlockSpec((d, n), lambda b_, i: (0, 0), pipeline_mode=pl.Buffered(1)),
    ]
    args = [x, nw.reshape(1, d), shift, scale, w]
    if rope is not None:
        cos, sin = rope
        in_specs += [pl.BlockSpec((tm, cos.shape[1]), lambda b_, i: (i, 0))] * 2
        args += [cos, sin]
    kern = functools.partial(_proj_kernel, tn=tn, rope_cols=(rope_cols if rope is not None else 0),
                             scale_cols=scale_cols, scale_val=scale_val, silu_cols=silu_cols, n_cast=len(cast))
    out, *converted = pl.pallas_call(
        kern,
        grid=(b, l // tm),
        in_specs=in_specs + [slab(a) for a in cast2d],
        out_specs=[pl.BlockSpec((1, tm, n), lambda b_, i: (b_, i, 0))] + [slab(a) for a in cast2d],
        out_shape=[jax.ShapeDtypeStruct((b, l, n), BF16)] + [jax.ShapeDtypeStruct(a.shape, BF16) for a in cast2d],
        scratch_shapes=[pltpu.VMEM((tm, d), BF16)],
        compiler_params=_cparams(("parallel", "parallel")),
        name="proj",
    )(*args, *cast2d)
    if not cast:
        return out
    return out, [c.reshape(a.shape) for c, a in zip(converted, cast)]


def _interleave_rows(v):
    sub, d = v.shape
    return v.reshape(SUBLANES, sub // SUBLANES, d).swapaxes(0, 1).reshape(sub, d)


def _deinterleave_rows(v):
    sub, d = v.shape
    return v.reshape(sub // SUBLANES, SUBLANES, d).swapaxes(0, 1).reshape(sub, d)


def _outproj_kernel(a_ref, w_ref, res_ref, gate_ref, nw_ref, sh_ref, sc_ref, o_ref, h_ref, *, tn):
    tm = a_ref.shape[1]
    sub = min(FFN_SUB, tm)
    nj = w_ref.shape[1] // tn

    def matmul(u):
        s, j = u
        return jnp.dot(a_ref[0, s * sub:(s + 1) * sub], w_ref[:, j * tn:(j + 1) * tn], preferred_element_type=F32)

    def epilogue(acc, u):
        s, j = u
        rows, cols = slice(s * sub, (s + 1) * sub), slice(j * tn, (j + 1) * tn)
        o_ref[0, rows, cols] = res_ref[0, rows, cols] + gate_ref[0, :, cols] * acc
        if j == nj - 1:
            h_ref[0, rows, :] = _norm_mod(o_ref[0, rows, :], nw_ref[...], sh_ref[0], sc_ref[0]).astype(h_ref.dtype)

    units = [(s, j) for s in range(tm // sub) for j in range(nj)]
    _pipelined_units(units, matmul, epilogue)


def _outproj(a, w, res, gate, nw, shift, scale, *, tn=512):
    b, l, k = a.shape
    n = w.shape[1]
    tm = min(ROW_TILE, l)
    assert tm % min(FFN_SUB, tm) == 0 and n % tn == 0
    vec = lambda arr: pl.BlockSpec((1, 1, n), _bsel(arr))
    return pl.pallas_call(
        functools.partial(_outproj_kernel, tn=tn),
        grid=(b, l // tm),
        in_specs=[
            pl.BlockSpec((1, tm, k), lambda b_, i: (b_, i, 0)),
            pl.BlockSpec((k, n), lambda b_, i: (0, 0), pipeline_mode=pl.Buffered(1)),
            pl.BlockSpec((1, tm, n), lambda b_, i: (b_, i, 0)),
            vec(gate),
            pl.BlockSpec((1, n), lambda b_, i: (0, 0)),
            vec(shift), vec(scale),
        ],
        out_specs=[pl.BlockSpec((1, tm, n), lambda b_, i: (b_, i, 0))] * 2,
        out_shape=[jax.ShapeDtypeStruct((b, l, n), F32), jax.ShapeDtypeStruct((b, l, n), BF16)],
        compiler_params=_cparams(("parallel", "parallel")),
        name="outproj",
    )(a, w, res, gate, nw.reshape(1, n), shift, scale)


FFN_TILE = 256
FFN_HALO = BF16_SUBLANES
FFN_SUB = 256
FFN_ROW_TILE = 1024


def _ffn_kernel(x_ref, hm_ref, hp_ref, hn_ref, g_ref, wu_ref, cw_ref, cb_ref, wd_ref, fw_ref,
                o_ref, h_ref, ua_ref, ug_ref, act_ref, *, final_norm):
    i = pl.program_id(1)
    tm = x_ref.shape[1]
    hl = FFN_HALO
    tf = FFN_TILE
    f = wd_ref.shape[0]
    sub = min(FFN_SUB, tm)
    nsub = tm // sub
    chunks = [(hl * (c > 0) + c * sub, hl + (c + 1) * sub + hl * (c == nsub - 1)) for c in range(nsub)]

    def stage(c):
        if c == 0:
            h_ref[0:hl] = jnp.where(i > 0, hp_ref[0], jnp.zeros_like(hp_ref[0]))
        if c == nsub - 1:
            h_ref[hl + tm:2 * hl + tm] = jnp.where(i < pl.num_programs(1) - 1, hn_ref[0], jnp.zeros_like(hn_ref[0]))
        h_ref[hl + c * sub:hl + (c + 1) * sub] = _interleave_rows(hm_ref[0, c * sub:(c + 1) * sub].astype(F32)).astype(BF16)

    def conv(u_ref, slot, col, s):
        base = hl + s * sub
        st = SUBLANES
        cw = cw_ref[:, col:col + tf]
        blk = u_ref[slot, base - st:base + sub + st]
        sublane = lax.broadcasted_iota(jnp.int32, (st, tf), 0)
        first_prev = jnp.where(sublane == 0, pltpu.roll(blk[0:st], 1, 0), pltpu.roll(blk[sub:sub + st], 1, 0))
        last_next = jnp.where(sublane == st - 1, pltpu.roll(blk[sub + st:sub + 2 * st], st - 1, 0),
                              pltpu.roll(blk[st:2 * st], st - 1, 0))
        prev = jnp.concatenate([first_prev, blk[st:sub]], axis=0)
        nxt = jnp.concatenate([blk[2 * st:sub + st], last_next], axis=0)
        return (prev * cw[0:1] + blk[st:sub + st] * cw[1:2] + nxt * cw[2:3]) + cb_ref[:, col:col + tf]

    def up(t, c):
        lo, hi = chunks[c]
        if t == 0:
            stage(c)
        ua_ref[t % 2, lo:hi] = jnp.dot(h_ref[lo:hi], wu_ref[:, t * tf:(t + 1) * tf], preferred_element_type=F32)
        ug_ref[t % 2, lo:hi] = jnp.dot(h_ref[lo:hi], wu_ref[:, f + t * tf:f + (t + 1) * tf], preferred_element_type=F32)

    units = [(t, c) for t in range(f // tf) for c in range(nsub)]
    up(*units[0])
    for n, (t, s) in enumerate(units):
        if n + 1 < len(units):
            up(*units[n + 1])
        act = _silu(conv(ua_ref, t % 2, t * tf, s)) * conv(ug_ref, t % 2, f + t * tf, s)
        act_ref[s * sub:(s + 1) * sub, t * tf:(t + 1) * tf] = act.astype(BF16)

    for s in range(nsub):
        rows = slice(s * sub, (s + 1) * sub)
        down = _deinterleave_rows(jnp.dot(act_ref[rows], wd_ref[...], preferred_element_type=F32))
        y = x_ref[0, rows] + g_ref[0] * down
        if final_norm:
            y = (y * lax.rsqrt(jnp.mean(y * y, axis=-1, keepdims=True) + NORM_EPS)) * fw_ref[...]
        o_ref[0, rows] = y


def _ffn(x, h, gate, wu, cw, cb, wd, final_w, *, layer, final_norm):
    b, l, d = x.shape
    f = wd.shape[1]
    tf = FFN_TILE
    assert f % tf == 0, wd.shape
    tm = min(FFN_ROW_TILE, l)
    hl = FFN_HALO
    per = tm // hl
    nblk = l // hl
    whole = lambda arr: pl.BlockSpec((None,) + arr.shape[1:], lambda b_, i: (layer, 0, 0), pipeline_mode=pl.Buffered(1))
    return pl.pallas_call(
        functools.partial(_ffn_kernel, final_norm=final_norm),
        grid=(b, l // tm),
        in_specs=[
            pl.BlockSpec((1, tm, d), lambda b_, i: (b_, i, 0)),
            pl.BlockSpec((1, tm, d), lambda b_, i: (b_, i, 0)),
            pl.BlockSpec((1, hl, d), lambda b_, i: (b_, jnp.maximum(i * per - 1, 0), 0)),
            pl.BlockSpec((1, hl, d), lambda b_, i: (b_, jnp.minimum((i + 1) * per, nblk - 1), 0)),
            pl.BlockSpec((1, 1, d), _bsel(gate)),
            whole(wu), whole(cw), whole(cb), whole(wd),
            pl.BlockSpec((1, d), lambda b_, i: (0, 0)),
        ],
        out_specs=pl.BlockSpec((1, tm, d), lambda b_, i: (b_, i, 0)),
        out_shape=jax.ShapeDtypeStruct((b, l, d), F32),
        scratch_shapes=[pltpu.VMEM((tm + 2 * hl, d), BF16), pltpu.VMEM((2, tm + 2 * hl, tf), F32),
                        pltpu.VMEM((2, tm + 2 * hl, tf), F32), pltpu.VMEM((tm, f), BF16)],
        compiler_params=_cparams(("parallel", "parallel")),
        name="ffn",
    )(x, h, h, h, gate, wu, cw, cb, wd, final_w.reshape(1, d))


def _ret_kernel(lg_ref, ql_ref, kl_ref, vl_ref, gl_ref, qc_ref, kc_ref, vc_ref, gc_ref, ol_ref, oc_ref, sb_ref):
    c = RET_CHUNK
    h = pl.program_id(1)
    lgf = lg_ref[0, h]
    lgb = lg_ref[1, h]
    n_ctx = qc_ref.shape[1] // c
    n_lat = ql_ref.shape[1] // c

    pos = lax.broadcasted_iota(jnp.int32, (c, 1), 0).astype(F32)
    kdec_f = jnp.exp(lgf * (c - 1.0 - pos))
    kdec_b = jnp.exp(lgb * pos)
    qdec_f = jnp.exp(lgf * (pos + 1.0))
    qdec_b = jnp.exp(lgb * (c - pos))
    one = jnp.ones((1, 1), F32)
    cdec_f = jnp.exp(lgf * c * one)
    cdec_b = jnp.exp(lgb * c * one)
    diff = (lax.broadcasted_iota(jnp.int32, (c, c), 0) - lax.broadcasted_iota(jnp.int32, (c, c), 1)).astype(F32)
    dmat = jnp.exp(jnp.where(diff >= 0, lgf, lgb) * jnp.abs(diff))

    seq = [(qc_ref, kc_ref, vc_ref, gc_ref, oc_ref, slice(t * c, (t + 1) * c)) for t in range(n_ctx)]
    seq += [(ql_ref, kl_ref, vl_ref, gl_ref, ol_ref, slice(t * c, (t + 1) * c)) for t in range(n_lat)]
    n = len(seq)

    def kv(idx, kdec):
        _, k_ref, v_ref, _, _, sl = seq[idx]
        kd = (k_ref[0, sl, :].astype(F32) * kdec).astype(BF16)
        return lax.dot_general(kd, v_ref[0, sl, :], (((0,), (0,)), ((), ())), preferred_element_type=F32)

    def qk(idx):
        q_ref, k_ref, _, _, _, sl = seq[idx]
        return lax.dot_general(q_ref[0, sl, :], k_ref[0, sl, :], (((1,), (1,)), ((), ())), preferred_element_type=F32)

    bwd_order = list(range(n_ctx - 1, -1, -1)) + list(range(n - 1, n_ctx - 1, -1))
    s = None
    for pos_, idx in enumerate(bwd_order):
        if s is not None:
            sb_ref[idx] = s.astype(BF16)
        if pos_ + 1 < n:
            upd = kv(idx, kdec_b)
            s = upd if s is None else s * cdec_b + upd

    s = None
    scores = qk(0)
    for idx in range(n):
        q_ref, _, v_ref, g_ref, o_ref, sl = seq[idx]
        qf = q_ref[0, sl, :].astype(F32)
        cross = None
        if idx != bwd_order[0]:
            cross = jnp.dot((qf * qdec_b).astype(BF16), sb_ref[idx], preferred_element_type=F32)
        if s is not None:
            cf = jnp.dot((qf * qdec_f).astype(BF16), s.astype(BF16), preferred_element_type=F32)
            cross = cf if cross is None else cross + cf
        upd = kv(idx, kdec_f) if idx + 1 < n else None
        nxt = qk(idx + 1) if idx + 1 < n else None
        o = jnp.dot((scores * dmat).astype(BF16), v_ref[0, sl, :], preferred_element_type=F32)
        if cross is not None:
            o = o + cross
        if upd is not None:
            s = upd if s is None else s * cdec_f + upd
        o = o * lax.rsqrt(jnp.mean(o * o, axis=-1, keepdims=True) + NORM_EPS)
        o_ref[0, sl, :] = (g_ref[0, sl, :].astype(F32) * o).astype(BF16)
        scores = nxt


def _retention(lg, pl_, pc):
    b, l, n = pl_.shape
    lc = pc.shape[1]
    hh = RET_HEADS
    dk = n // (6 * hh)
    dv = 2 * dk
    c = RET_CHUNK

    def specs(length):
        return [
            pl.BlockSpec((1, length, dk), lambda b_, h: (b_, 0, h)),
            pl.BlockSpec((1, length, dk), lambda b_, h: (b_, 0, hh + h)),
            pl.BlockSpec((1, length, dv), lambda b_, h: (b_, 0, hh + h)),
            pl.BlockSpec((1, length, dv), lambda b_, h: (b_, 0, 2 * hh + h)),
        ]

    return pl.pallas_call(
        _ret_kernel,
        grid=(b, hh),
        in_specs=[pl.BlockSpec(memory_space=pltpu.SMEM)] + specs(l) + specs(lc),
        out_specs=[
            pl.BlockSpec((1, l, dv), lambda b_, h: (b_, 0, h)),
            pl.BlockSpec((1, lc, dv), lambda b_, h: (b_, 0, h)),
        ],
        out_shape=[jax.ShapeDtypeStruct((b, l, hh * dv), BF16), jax.ShapeDtypeStruct((b, lc, hh * dv), BF16)],
        scratch_shapes=[pltpu.VMEM((lc // c + l // c, dk, dv), BF16)],
        compiler_params=_cparams(("parallel", "arbitrary")),
        name="retention",
    )(lg, pl_, pl_, pl_, pl_, pc, pc, pc, pc)


def _rope_tables(seq, dk):
    quarter = dk // 4
    t = np.arange(seq)
    inv = ROPE_BASE ** (-np.arange(quarter, dtype=np.float64) / quarter)
    ang_r = (t // GRID_W)[:, None] * inv[None, :]
    ang_c = (t % GRID_W)[:, None] * inv[None, :]
    cos = np.concatenate([np.cos(ang_r)] * 2 + [np.cos(ang_c)] * 2, axis=-1)
    sin = np.concatenate([-np.sin(ang_r), np.sin(ang_r), -np.sin(ang_c), np.sin(ang_c)], axis=-1)
    return jnp.asarray(cos, F32), jnp.asarray(sin, F32)


def _na_window(rows):
    kh = min(NA_MAX_KH, rows)
    return kh, NA_QROWS + kh


def _na_window_start(blk, rows):
    kh, wrows = _na_window(rows)
    return min(max(blk * NA_QROWS - kh // 2, 0), rows - wrows)


def _na_kernel(m_ref, q_ref, k_ref, v_ref, kc_ref, vc_ref, o_ref, bias_ref, sl_ref, sc_ref, pl_ref, pc_ref, *, rows):
    b = pl.program_id(1)
    kh, wrows = _na_window(rows)
    qr_n = NA_QROWS
    w = GRID_W
    hd = q_ref.shape[2] // 2
    nblk = rows // qr_n

    @pl.when(b == 0)
    def _():
        qc = lax.broadcasted_iota(jnp.int32, (w, LANES), 0)
        kc = lax.broadcasted_iota(jnp.int32, (w, LANES), 1)
        cs = jnp.clip(qc - NA_KW // 2, 0, w - NA_KW)
        col_valid = (kc >= cs) & (kc < cs + NA_KW)
        toe = [[jnp.where(col_valid,
                          pltpu.roll(jnp.broadcast_to(m_ref[e, dr:dr + 1, :], (w, LANES)), 0, 1, stride=1, stride_axis=0),
                          MASK_VALUE)[:, :w]
                for dr in range(2 * kh - 1)] for e in range(2)]
        for variant in range(3):
            r0 = (0, qr_n, rows - qr_n)[variant]
            ws = _na_window_start(r0 // qr_n, rows)
            for qr in range(qr_n):
                r = r0 + qr
                rs = min(max(r - kh // 2, 0), rows - kh)
                for krr in range(wrows):
                    kr = ws + krr
                    for e in range(2):
                        dst = (e, variant, slice(qr * w, (qr + 1) * w), slice(krr * w, (krr + 1) * w))
                        if rs <= kr < rs + kh:
                            bias_ref[dst] = toe[e][kr - r + kh - 1]
                        else:
                            bias_ref[dst] = jnp.full((w, w), MASK_VALUE, F32)

    vc = vc_ref[0]
    nq = qr_n * w
    lane = lax.broadcasted_iota(jnp.int32, (nq, q_ref.shape[2]), 1)

    kc = kc_ref[0]
    trans_b = (((1,), (1,)), ((), ()))

    chains = []
    for k in range(nblk):
        variant = 0 if k == 0 else (2 if k == nblk - 1 else 1)
        ws = _na_window_start(k, rows)
        win = slice(ws * w, (ws + wrows) * w)
        q = q_ref[0, k * nq:(k + 1) * nq, :]
        for e in range(2):
            qe = jnp.where((lane >= e * hd) & (lane < (e + 1) * hd), q, jnp.zeros_like(q))
            ci = len(chains)
            sl_ref[ci] = lax.dot_general(qe, k_ref[0, win, :], trans_b, preferred_element_type=F32) + bias_ref[e, variant]
            sc_ref[ci] = lax.dot_general(qe, kc, trans_b, preferred_element_type=F32)
            chains.append((k, e, win, ci))

    def head_values(v, e):
        vlane = lax.broadcasted_iota(jnp.int32, v.shape, 1)
        return jnp.where((vlane >= e * hd) & (vlane < (e + 1) * hd), v, jnp.ones_like(v))

    vc_e = [head_values(vc, e) for e in range(2)]
    outs = {}
    for k, e, win, ci in chains:
        m = jnp.maximum(jnp.max(sl_ref[ci], axis=-1, keepdims=True), jnp.max(sc_ref[ci], axis=-1, keepdims=True))
        pl_ref[ci] = jnp.exp2(sl_ref[ci] - m).astype(BF16)
        pc_ref[ci] = jnp.exp2(sc_ref[ci] - m).astype(BF16)
        o = jnp.dot(pl_ref[ci], head_values(v_ref[0, win, :], e), preferred_element_type=F32)
        o += jnp.dot(pc_ref[ci], vc_e[e], preferred_element_type=F32)
        outs[k, e] = o / pltpu.roll(o, hd, 1)
        if e == 1:
            o_ref[0, k * nq:(k + 1) * nq, :] = jnp.where(lane < hd, outs[k, 0], outs[k, 1]).astype(o_ref.dtype)


def _na_bias_table(rpb):
    padded = jnp.pad(rpb * LOG2E, ((0, 0), (0, 0), (0, LANES - rpb.shape[2])))
    return jnp.roll(padded, -(NA_KW - 1), axis=2)


def _na_attention(qkv, kvc, rpb):
    b, s, d3 = qkv.shape
    d = d3 // 3
    lc = kvc.shape[1]
    rows = s // GRID_W
    pairs = d // LANES
    kh, wrows = _na_window(rows)
    qrows = NA_QROWS * GRID_W
    nchain = 2 * (rows // NA_QROWS)
    assert kh == 2 * NA_QROWS and rows % NA_QROWS == 0 and rows >= wrows + NA_QROWS, rows
    mtab = _na_bias_table(rpb)
    return pl.pallas_call(
        functools.partial(_na_kernel, rows=rows),
        grid=(pairs, b),
        in_specs=[
            pl.BlockSpec((2,) + mtab.shape[1:], lambda p, b_: (p, 0, 0)),
            pl.BlockSpec((1, s, LANES), lambda p, b_: (b_, 0, p)),
            pl.BlockSpec((1, s, LANES), lambda p, b_: (b_, 0, pairs + p)),
            pl.BlockSpec((1, s, LANES), lambda p, b_: (b_, 0, 2 * pairs + p)),
            pl.BlockSpec((1, lc, LANES), lambda p, b_: (b_, 0, p)),
            pl.BlockSpec((1, lc, LANES), lambda p, b_: (b_, 0, pairs + p)),
        ],
        out_specs=pl.BlockSpec((1, s, LANES), lambda p, b_: (b_, 0, p)),
        out_shape=jax.ShapeDtypeStruct((b, s, d), BF16),
        scratch_shapes=[pltpu.VMEM((2, 3, qrows, wrows * GRID_W), F32),
                        pltpu.VMEM((nchain, qrows, wrows * GRID_W), F32), pltpu.VMEM((nchain, qrows, lc), F32),
                        pltpu.VMEM((nchain, qrows, wrows * GRID_W), BF16), pltpu.VMEM((nchain, qrows, lc), BF16)],
        compiler_params=_cparams(("arbitrary", "arbitrary")),
        name="na_attention",
    )(mtab, qkv, qkv, qkv, kvc, kvc)


def kernel(x, c, ctx, c_ctx, ada_w, ada_b, norm1_w, norm2_w, ret_w_in, ret_decay_fwd, ret_decay_bwd, ret_w_out,
           na_w_qkv, na_rpb, na_w_out, ffn_w_up, ffn_conv_w, ffn_conv_b, ffn_w_down, final_norm_w):
    b, seq, d = x.shape
    depth = ada_w.shape[0]
    n_mixers = 2

    pad = BF16_SUBLANES - (b + 1) % BF16_SUBLANES
    cc = jnp.concatenate([c, c_ctx[None, :], jnp.zeros((pad, d), F32)], axis=0)
    mods, ret_w_in_bf16 = _ada(cc, ada_w, ada_b, ret_w_in)
    mods = mods.reshape(depth, cc.shape[0], 6, d)

    def mod(i, k):
        return mods[i, :b, k][:, None, :], mods[i, b:b + 1, k][:, None, :]

    bf16_w = None
    for i in range(depth):
        last = i == depth - 1
        j = i // n_mixers
        (sh1, csh1), (sc1, csc1), (g1, cg1) = mod(i, 0), mod(i, 1), mod(i, 2)
        (sh2, csh2), (sc2, csc2), (g2, cg2) = mod(i, 3), mod(i, 4), mod(i, 5)
        if i % n_mixers == 0:
            hk = ret_w_in.shape[2] // 6
            dk = hk // RET_HEADS
            w_in = ret_w_in_bf16[j]
            kw = dict(scale_cols=(hk, 2 * hk), scale_val=dk ** -0.5, silu_cols=(4 * hk, 6 * hk))
            rope = dict(rope=_rope_tables(seq, dk), rope_cols=2 * hk)
            if bf16_w is None:
                pl_, bf16_w = _proj(x, norm1_w[i], sh1, sc1, w_in, **rope, **kw,
                                    cast=(ffn_w_up, ffn_w_down, ret_w_out, na_w_qkv, na_w_out))
                ffn_w = (bf16_w[0], ffn_conv_w, ffn_conv_b[:, None, :], bf16_w[1])
            else:
                pl_ = _proj(x, norm1_w[i], sh1, sc1, w_in, **rope, **kw)
            pc = _proj(ctx, norm1_w[i], csh1, csc1, w_in, **kw)
            lg = jnp.stack([-jnp.exp(ret_decay_fwd[j].astype(F32)), -jnp.exp(ret_decay_bwd[j].astype(F32))])
            y, yc = _retention(lg, pl_, pc)
            w_out = bf16_w[2][j]
        else:
            w_qkv = bf16_w[3][j]
            qkv = _proj(x, norm1_w[i], sh1, sc1, w_qkv, scale_cols=(0, d), scale_val=(d // NA_HEADS) ** -0.5 * LOG2E)
            kvc = _proj(ctx, norm1_w[i], csh1, csc1, w_qkv[:, d:])
            if not last:
                raise NotImplementedError("context output of the attention mixer is only needed for depth > 2")
            y, yc = _na_attention(qkv, kvc, na_rpb[j]), None
            w_out = bf16_w[4][j]
        x, h2 = _outproj(y, w_out, x, g1, norm2_w[i], sh2, sc2)
        x = _ffn(x, h2, g2, *ffn_w, final_norm_w, layer=i, final_norm=last)
        if not last:
            ctx, hc2 = _outproj(yc, w_out, ctx, cg1, norm2_w[i], csh2, csc2)
            ctx = _ffn(ctx, hc2, cg2, *ffn_w, final_norm_w, layer=i, final_norm=False)
    return x
```

```python
import functools

import jax
import jax.numpy as jnp
import numpy as np
from jax import lax
from jax.experimental import pallas as pl
from jax.experimental.pallas import tpu as pltpu

F32 = jnp.float32
BF16 = jnp.bfloat16

GRID_W = 64
RET_HEADS = 4
ROPE_BASE = 10000.0
NA_HEADS = 16
NA_MAX_KH = 8
NA_KW = 16
CONV_W = 3
NORM_EPS = 1e-6

LANES = 128
SUBLANES = 8
BF16_SUBLANES = 16
VMEM_LIMIT_BYTES = 56 * 1024 * 1024

ADA_COL_TILE = 3072
ROW_TILE = 1024
PROJ_ROW_TILES = (1024, 512, 256)
PROJ_VMEM_BUDGET = 40 * 1024 * 1024
MM_SUB = 256
RET_CHUNK = 256
NA_QROWS = 4
MASK_VALUE = -1e30
LOG2E = 1.4426950408889634


def _cparams(sem):
    return pltpu.CompilerParams(dimension_semantics=sem, vmem_limit_bytes=VMEM_LIMIT_BYTES)


def _bsel(arr):
    if arr.shape[0] == 1:
        return lambda b, *_: (0, 0, 0)
    return lambda b, *_: (b, 0, 0)


def _norm_mod(x, nw, shift, scale):
    y = x * lax.rsqrt(jnp.mean(x * x, axis=-1, keepdims=True) + NORM_EPS)
    return (y * nw) * (1.0 + scale) + shift


def _silu(x):
    return x / (1.0 + jnp.exp(-x))


def _ada_kernel(c_ref, w_ref, b_ref, ci_ref, o_ref, co_ref):
    s = _silu(c_ref[...])
    o_ref[0] = jnp.dot(s, w_ref[0], preferred_element_type=F32) + b_ref[0]
    co_ref[...] = ci_ref[...].astype(co_ref.dtype)


def _ada(cc, ada_w, ada_b, cast):
    depth, d, n = ada_w.shape
    tn = ADA_COL_TILE
    nj = n // tn
    cast2d = cast.reshape(-1, cast.shape[-1])
    assert cast2d.shape[0] % (depth * nj * BF16_SUBLANES) == 0, cast.shape
    slab = pl.BlockSpec((cast2d.shape[0] // (depth * nj), cast2d.shape[1]), lambda i, j: (i * nj + j, 0))
    mods, converted = pl.pallas_call(
        _ada_kernel,
        grid=(depth, nj),
        in_specs=[
            pl.BlockSpec(cc.shape, lambda i, j: (0, 0)),
            pl.BlockSpec((1, d, tn), lambda i, j: (i, 0, j)),
            pl.BlockSpec((1, 1, tn), lambda i, j: (i, 0, j)),
            slab,
        ],
        out_specs=[pl.BlockSpec((1, cc.shape[0], tn), lambda i, j: (i, 0, j)), slab],
        out_shape=[jax.ShapeDtypeStruct((depth, cc.shape[0], n), F32), jax.ShapeDtypeStruct(cast2d.shape, BF16)],
        compiler_params=_cparams(("parallel", "parallel")),
        name="ada",
    )(cc, ada_w, ada_b.reshape(depth, 1, n), cast2d)
    return mods, converted.reshape(cast.shape)


def _pipelined_units(units, matmul, epilogue):
    acc = matmul(units[0])
    for n, u in enumerate(units):
        nxt = matmul(units[n + 1]) if n + 1 < len(units) else None
        epilogue(acc, u)
        acc = nxt


def _proj_kernel(*refs, tn, rope_cols, scale_cols, scale_val, silu_cols, n_cast):
    n_in = 7 if rope_cols else 5
    cast_in, cast_out = refs[n_in:n_in + n_cast], refs[n_in + n_cast + 1:n_in + 2 * n_cast + 1]
    refs = refs[:n_in] + (refs[n_in + n_cast],) + refs[n_in + 2 * n_cast + 1:]
    if rope_cols:
        x_ref, nw_ref, sh_ref, sc_ref, w_ref, cos_ref, sin_ref, o_ref, h_ref = refs
    else:
        x_ref, nw_ref, sh_ref, sc_ref, w_ref, o_ref, h_ref = refs
    tm = x_ref.shape[1]
    sub = min(MM_SUB, tm)

    def matmul(u):
        s, j = u
        rows = slice(s * sub, (s + 1) * sub)
        if j == 0:
            h_ref[rows] = _norm_mod(x_ref[0, rows], nw_ref[...], sh_ref[0], sc_ref[0]).astype(BF16)
        return jnp.dot(h_ref[rows], w_ref[:, j * tn:(j + 1) * tn], preferred_element_type=F32)

    def epilogue(acc, u):
        s, j = u
        if u == (0, 1):
            for src, dst in zip(cast_in, cast_out):
                dst[...] = src[...].astype(dst.dtype)
        rows = slice(s * sub, (s + 1) * sub)
        col = j * tn
        if scale_cols[0] <= col < scale_cols[1]:
            acc = acc * scale_val
        if silu_cols[0] <= col < silu_cols[1]:
            acc = _silu(acc)
        if col >= rope_cols:
            o_ref[0, rows, col:col + tn] = acc.astype(o_ref.dtype)
            return
        for g in range(tn // LANES):
            part = acc[:, g * LANES:(g + 1) * LANES]
            t = (g % 2) * LANES
            rot = part * cos_ref[rows, t:t + LANES] + pltpu.roll(part, LANES // 2, 1) * sin_ref[rows, t:t + LANES]
            o_ref[0, rows, col + g * LANES:col + (g + 1) * LANES] = rot.astype(o_ref.dtype)

    units = [(s, j) for s in range(tm // sub) for j in range(w_ref.shape[1] // tn)]
    _pipelined_units(units, matmul, epilogue)


def _proj_row_tile(l, d, n, rope_width):
    for tm in PROJ_ROW_TILES:
        if tm > l or l % tm:
            continue
        blocks = 2 * tm * d * 4 + 2 * tm * n * 2 + 4 * tm * rope_width * 4
        if d * n * 2 + blocks + tm * d * 2 <= PROJ_VMEM_BUDGET:
            return tm
    raise ValueError((l, d, n))


def _proj(x, nw, shift, scale, w, *, tn=512, rope=None, rope_cols=0, scale_cols=(0, 0), scale_val=1.0, silu_cols=(0, 0),
          cast=()):
    b, l, d = x.shape
    n = w.shape[1]
    tm = _proj_row_tile(l, d, n, rope[0].shape[1] if rope is not None else 0)
    assert all(v % tn == 0 for v in (n, rope_cols) + tuple(scale_cols) + tuple(silu_cols))
    steps = b * (l // tm)
    cast2d = [a.reshape(-1, a.shape[-1]) for a in cast]
    assert all(a.shape[0] % (steps * BF16_SUBLANES) == 0 for a in cast2d), [a.shape for a in cast2d]
    assert not cast or n // tn >= 2
    slab = lambda a: pl.BlockSpec((a.shape[0] // steps, a.shape[1]), lambda b_, i: (b_ * (l // tm) + i, 0))
    in_specs = [
        pl.BlockSpec((1, tm, d), lambda b_, i: (b_, i, 0)),
        pl.BlockSpec((1, d), lambda b_, i: (0, 0)),
        pl.BlockSpec((1, 1, d), _bsel(shift)),
        pl.BlockSpec((1, 1, d), _bsel(scale)),
        pl.BlockSpec((d, n), lambda b_, i: (0, 0), pipeline_mode=pl.Buffered(1)),
    ]
    args = [x, nw.reshape(1, d), shift, scale, w]
    if rope is not None:
        cos, sin = rope
        in_specs += [pl.BlockSpec((tm, cos.shape[1]), lambda b_, i: (i, 0))] * 2
        args += [cos, sin]
    kern = functools.partial(_proj_kernel, tn=tn, rope_cols=(rope_cols if rope is not None else 0),
                             scale_cols=scale_cols, scale_val=scale_val, silu_cols=silu_cols, n_cast=len(cast))
    out, *converted = pl.pallas_call(
        kern,
        grid=(b, l // tm),
        in_specs=in_specs + [slab(a) for a in cast2d],
        out_specs=[pl.BlockSpec((1, tm, n), lambda b_, i: (b_, i, 0))] + [slab(a) for a in cast2d],
        out_shape=[jax.ShapeDtypeStruct((b, l, n), BF16)] + [jax.ShapeDtypeStruct(a.shape, BF16) for a in cast2d],
        scratch_shapes=[pltpu.VMEM((tm, d), BF16)],
        compiler_params=_cparams(("parallel", "parallel")),
        name="proj",
    )(*args, *cast2d)
    if not cast:
        return out
    return out, [c.reshape(a.shape) for c, a in zip(converted, cast)]


def _interleave_rows(v):
    sub, d = v.shape
    return v.reshape(SUBLANES, sub // SUBLANES, d).swapaxes(0, 1).reshape(sub, d)


def _deinterleave_rows(v):
    sub, d = v.shape
    return v.reshape(sub // SUBLANES, SUBLANES, d).swapaxes(0, 1).reshape(sub, d)


def _outproj_kernel(a_ref, w_ref, res_ref, gate_ref, nw_ref, sh_ref, sc_ref, o_ref, h_ref, *, tn):
    tm = a_ref.shape[1]
    sub = min(MM_SUB, tm)
    nj = w_ref.shape[1] // tn

    def matmul(u):
        s, j = u
        return jnp.dot(a_ref[0, s * sub:(s + 1) * sub], w_ref[:, j * tn:(j + 1) * tn], preferred_element_type=F32)

    def epilogue(acc, u):
        s, j = u
        rows, cols = slice(s * sub, (s + 1) * sub), slice(j * tn, (j + 1) * tn)
        o_ref[0, rows, cols] = res_ref[0, rows, cols] + gate_ref[0, :, cols] * acc
        if j == nj - 1:
            h_ref[0, rows, :] = _norm_mod(o_ref[0, rows, :], nw_ref[...], sh_ref[0], sc_ref[0]).astype(h_ref.dtype)

    units = [(s, j) for s in range(tm // sub) for j in range(nj)]
    _pipelined_units(units, matmul, epilogue)


def _outproj(a, w, res, gate, nw, shift, scale, *, tn=512):
    b, l, k = a.shape
    n = w.shape[1]
    tm = min(ROW_TILE, l)
    assert tm % min(MM_SUB, tm) == 0 and n % tn == 0
    vec = lambda arr: pl.BlockSpec((1, 1, n), _bsel(arr))
    return pl.pallas_call(
        functools.partial(_outproj_kernel, tn=tn),
        grid=(b, l // tm),
        in_specs=[
            pl.BlockSpec((1, tm, k), lambda b_, i: (b_, i, 0)),
            pl.BlockSpec((k, n), lambda b_, i: (0, 0), pipeline_mode=pl.Buffered(1)),
            pl.BlockSpec((1, tm, n), lambda b_, i: (b_, i, 0)),
            vec(gate),
            pl.BlockSpec((1, n), lambda b_, i: (0, 0)),
            vec(shift), vec(scale),
        ],
        out_specs=[pl.BlockSpec((1, tm, n), lambda b_, i: (b_, i, 0))] * 2,
        out_shape=[jax.ShapeDtypeStruct((b, l, n), F32), jax.ShapeDtypeStruct((b, l, n), BF16)],
        compiler_params=_cparams(("parallel", "parallel")),
        name="outproj",
    )(a, w, res, gate, nw.reshape(1, n), shift, scale)


FFN_TILE = 256
FFN_HALO = BF16_SUBLANES
FFN_SUB = 512
FFN_ROW_TILE = 1024


def _ffn_kernel(x_ref, hm_ref, hp_ref, hn_ref, g_ref, wu_ref, cw_ref, cb_ref, wd_ref, fw_ref,
                o_ref, h_ref, ua_ref, ug_ref, act_ref, *, final_norm):
    i = pl.program_id(1)
    tm = x_ref.shape[1]
    hl = FFN_HALO
    tf = FFN_TILE
    f = wd_ref.shape[0]
    sub = min(FFN_SUB, tm)
    nsub = tm // sub
    chunks = [(hl * (c > 0) + c * sub, hl + (c + 1) * sub + hl * (c == nsub - 1)) for c in range(nsub)]

    def stage(c):
        if c == 0:
            h_ref[0:hl] = jnp.where(i > 0, hp_ref[0], jnp.zeros_like(hp_ref[0]))
        if c == nsub - 1:
            h_ref[hl + tm:2 * hl + tm] = jnp.where(i < pl.num_programs(1) - 1, hn_ref[0], jnp.zeros_like(hn_ref[0]))
        h_ref[hl + c * sub:hl + (c + 1) * sub] = _interleave_rows(hm_ref[0, c * sub:(c + 1) * sub].astype(F32)).astype(BF16)

    def conv(u_ref, slot, col, s):
        base = hl + s * sub
        st = SUBLANES
        cw = cw_ref[:, col:col + tf]
        blk = u_ref[slot, base - st:base + sub + st]
        sublane = lax.broadcasted_iota(jnp.int32, (st, tf), 0)
        first_prev = jnp.where(sublane == 0, pltpu.roll(blk[0:st], 1, 0), pltpu.roll(blk[sub:sub + st], 1, 0))
        last_next = jnp.where(sublane == st - 1, pltpu.roll(blk[sub + st:sub + 2 * st], st - 1, 0),
                              pltpu.roll(blk[st:2 * st], st - 1, 0))
        prev = jnp.concatenate([first_prev, blk[st:sub]], axis=0)
        nxt = jnp.concatenate([blk[2 * st:sub + st], last_next], axis=0)
        return (prev * cw[0:1] + blk[st:sub + st] * cw[1:2] + nxt * cw[2:3]) + cb_ref[:, col:col + tf]

    def up(t, c):
        lo, hi = chunks[c]
        if t == 0:
            stage(c)
        ua_ref[t % 2, lo:hi] = jnp.dot(h_ref[lo:hi], wu_ref[:, t * tf:(t + 1) * tf], preferred_element_type=F32)
        ug_ref[t % 2, lo:hi] = jnp.dot(h_ref[lo:hi], wu_ref[:, f + t * tf:f + (t + 1) * tf], preferred_element_type=F32)

    units = [(t, c) for t in range(f // tf) for c in range(nsub)]
    up(*units[0])
    for n, (t, s) in enumerate(units):
        if n + 1 < len(units):
            up(*units[n + 1])
        act = _silu(conv(ua_ref, t % 2, t * tf, s)) * conv(ug_ref, t % 2, f + t * tf, s)
        act_ref[s * sub:(s + 1) * sub, t * tf:(t + 1) * tf] = act.astype(BF16)

    for s in range(nsub):
        rows = slice(s * sub, (s + 1) * sub)
        down = _deinterleave_rows(jnp.dot(act_ref[rows], wd_ref[...], preferred_element_type=F32))
        y = x_ref[0, rows] + g_ref[0] * down
        if final_norm:
            y = (y * lax.rsqrt(jnp.mean(y * y, axis=-1, keepdims=True) + NORM_EPS)) * fw_ref[...]
        o_ref[0, rows] = y


def _ffn(x, h, gate, wu, cw, cb, wd, final_w, *, layer, final_norm):
    b, l, d = x.shape
    f = wd.shape[1]
    tf = FFN_TILE
    assert f % tf == 0, wd.shape
    tm = min(FFN_ROW_TILE, l)
    hl = FFN_HALO
    per = tm // hl
    nblk = l // hl
    whole = lambda arr: pl.BlockSpec((None,) + arr.shape[1:], lambda b_, i: (layer, 0, 0), pipeline_mode=pl.Buffered(1))
    return pl.pallas_call(
        functools.partial(_ffn_kernel, final_norm=final_norm),
        grid=(b, l // tm),
        in_specs=[
            pl.BlockSpec((1, tm, d), lambda b_, i: (b_, i, 0)),
            pl.BlockSpec((1, tm, d), lambda b_, i: (b_, i, 0)),
            pl.BlockSpec((1, hl, d), lambda b_, i: (b_, jnp.maximum(i * per - 1, 0), 0)),
            pl.BlockSpec((1, hl, d), lambda b_, i: (b_, jnp.minimum((i + 1) * per, nblk - 1), 0)),
            pl.BlockSpec((1, 1, d), _bsel(gate)),
            whole(wu), whole(cw), whole(cb), whole(wd),
            pl.BlockSpec((1, d), lambda b_, i: (0, 0)),
        ],
        out_specs=pl.BlockSpec((1, tm, d), lambda b_, i: (b_, i, 0)),
        out_shape=jax.ShapeDtypeStruct((b, l, d), F32),
        scratch_shapes=[pltpu.VMEM((tm + 2 * hl, d), BF16), pltpu.VMEM((2, tm + 2 * hl, tf), F32),
                        pltpu.VMEM((2, tm + 2 * hl, tf), F32), pltpu.VMEM((tm, f), BF16)],
        compiler_params=_cparams(("parallel", "parallel")),
        name="ffn",
    )(x, h, h, h, gate, wu, cw, cb, wd, final_w.reshape(1, d))


def _ret_kernel(lg_ref, ql_ref, kl_ref, vl_ref, gl_ref, qc_ref, kc_ref, vc_ref, gc_ref, ol_ref, oc_ref, sb_ref):
    c = RET_CHUNK
    h = pl.program_id(1)
    lgf = lg_ref[0, h]
    lgb = lg_ref[1, h]
    n_ctx = qc_ref.shape[1] // c
    n_lat = ql_ref.shape[1] // c

    pos = lax.broadcasted_iota(jnp.int32, (c, 1), 0).astype(F32)
    kdec_f = jnp.exp(lgf * (c - 1.0 - pos))
    kdec_b = jnp.exp(lgb * pos)
    qdec_f = jnp.exp(lgf * (pos + 1.0))
    qdec_b = jnp.exp(lgb * (c - pos))
    one = jnp.ones((1, 1), F32)
    cdec_f = jnp.exp(lgf * c * one)
    cdec_b = jnp.exp(lgb * c * one)
    diff = (lax.broadcasted_iota(jnp.int32, (c, c), 0) - lax.broadcasted_iota(jnp.int32, (c, c), 1)).astype(F32)
    dmat = jnp.exp(jnp.where(diff >= 0, lgf, lgb) * jnp.abs(diff))

    seq = [(qc_ref, kc_ref, vc_ref, gc_ref, oc_ref, slice(t * c, (t + 1) * c)) for t in range(n_ctx)]
    seq += [(ql_ref, kl_ref, vl_ref, gl_ref, ol_ref, slice(t * c, (t + 1) * c)) for t in range(n_lat)]
    n = len(seq)

    def kv(idx, kdec):
        _, k_ref, v_ref, _, _, sl = seq[idx]
        kd = (k_ref[0, sl, :].astype(F32) * kdec).astype(BF16)
        return lax.dot_general(kd, v_ref[0, sl, :], (((0,), (0,)), ((), ())), preferred_element_type=F32)

    def qk(idx):
        q_ref, k_ref, _, _, _, sl = seq[idx]
        return lax.dot_general(q_ref[0, sl, :], k_ref[0, sl, :], (((1,), (1,)), ((), ())), preferred_element_type=F32)

    bwd_order = list(range(n_ctx - 1, -1, -1)) + list(range(n - 1, n_ctx - 1, -1))
    s = None
    for pos_, idx in enumerate(bwd_order):
        if s is not None:
            sb_ref[idx] = s.astype(BF16)
        if pos_ + 1 < n:
            upd = kv(idx, kdec_b)
            s = upd if s is None else s * cdec_b + upd

    s = None
    scores = qk(0)
    for idx in range(n):
        q_ref, _, v_ref, g_ref, o_ref, sl = seq[idx]
        qf = q_ref[0, sl, :].astype(F32)
        cross = None
        if idx != bwd_order[0]:
            cross = jnp.dot((qf * qdec_b).astype(BF16), sb_ref[idx], preferred_element_type=F32)
        if s is not None:
            cf = jnp.dot((qf * qdec_f).astype(BF16), s.astype(BF16), preferred_element_type=F32)
            cross = cf if cross is None else cross + cf
        upd = kv(idx, kdec_f) if idx + 1 < n else None
        nxt = qk(idx + 1) if idx + 1 < n else None
        o = jnp.dot((scores * dmat).astype(BF16), v_ref[0, sl, :], preferred_element_type=F32)
        if cross is not None:
            o = o + cross
        if upd is not None:
            s = upd if s is None else s * cdec_f + upd
        o = o * lax.rsqrt(jnp.mean(o * o, axis=-1, keepdims=True) + NORM_EPS)
        o_ref[0, sl, :] = (g_ref[0, sl, :].astype(F32) * o).astype(BF16)
        scores = nxt


def _retention(lg, pl_, pc):
    b, l, n = pl_.shape
    lc = pc.shape[1]
    hh = RET_HEADS
    dk = n // (6 * hh)
    dv = 2 * dk
    c = RET_CHUNK

    def specs(length):
        return [
            pl.BlockSpec((1, length, dk), lambda b_, h: (b_, 0, h)),
            pl.BlockSpec((1, length, dk), lambda b_, h: (b_, 0, hh + h)),
            pl.BlockSpec((1, length, dv), lambda b_, h: (b_, 0, hh + h)),
            pl.BlockSpec((1, length, dv), lambda b_, h: (b_, 0, 2 * hh + h)),
        ]

    return pl.pallas_call(
        _ret_kernel,
        grid=(b, hh),
        in_specs=[pl.BlockSpec(memory_space=pltpu.SMEM)] + specs(l) + specs(lc),
        out_specs=[
            pl.BlockSpec((1, l, dv), lambda b_, h: (b_, 0, h)),
            pl.BlockSpec((1, lc, dv), lambda b_, h: (b_, 0, h)),
        ],
        out_shape=[jax.ShapeDtypeStruct((b, l, hh * dv), BF16), jax.ShapeDtypeStruct((b, lc, hh * dv), BF16)],
        scratch_shapes=[pltpu.VMEM((lc // c + l // c, dk, dv), BF16)],
        compiler_params=_cparams(("parallel", "arbitrary")),
        name="retention",
    )(lg, pl_, pl_, pl_, pl_, pc, pc, pc, pc)


def _rope_tables(seq, dk):
    quarter = dk // 4
    t = np.arange(seq)
    inv = ROPE_BASE ** (-np.arange(quarter, dtype=np.float64) / quarter)
    ang_r = (t // GRID_W)[:, None] * inv[None, :]
    ang_c = (t % GRID_W)[:, None] * inv[None, :]
    cos = np.concatenate([np.cos(ang_r)] * 2 + [np.cos(ang_c)] * 2, axis=-1)
    sin = np.concatenate([-np.sin(ang_r), np.sin(ang_r), -np.sin(ang_c), np.sin(ang_c)], axis=-1)
    return jnp.asarray(cos, F32), jnp.asarray(sin, F32)


def _na_window(rows):
    kh = min(NA_MAX_KH, rows)
    return kh, NA_QROWS + kh


def _na_window_start(blk, rows):
    kh, wrows = _na_window(rows)
    return min(max(blk * NA_QROWS - kh // 2, 0), rows - wrows)


def _na_kernel(m_ref, q_ref, k_ref, v_ref, kc_ref, vc_ref, o_ref, bias_ref, sl_ref, sc_ref, pl_ref, pc_ref, *, rows):
    b = pl.program_id(1)
    kh, wrows = _na_window(rows)
    qr_n = NA_QROWS
    w = GRID_W
    hd = q_ref.shape[2] // 2
    nblk = rows // qr_n

    @pl.when(b == 0)
    def _():
        qc = lax.broadcasted_iota(jnp.int32, (w, LANES), 0)
        kc = lax.broadcasted_iota(jnp.int32, (w, LANES), 1)
        cs = jnp.clip(qc - NA_KW // 2, 0, w - NA_KW)
        col_valid = (kc >= cs) & (kc < cs + NA_KW)
        toe = [[jnp.where(col_valid,
                          pltpu.roll(jnp.broadcast_to(m_ref[e, dr:dr + 1, :], (w, LANES)), 0, 1, stride=1, stride_axis=0),
                          MASK_VALUE)[:, :w]
                for dr in range(2 * kh - 1)] for e in range(2)]
        for variant in range(3):
            r0 = (0, qr_n, rows - qr_n)[variant]
            ws = _na_window_start(r0 // qr_n, rows)
            for qr in range(qr_n):
                r = r0 + qr
                rs = min(max(r - kh // 2, 0), rows - kh)
                for krr in range(wrows):
                    kr = ws + krr
                    for e in range(2):
                        dst = (e, variant, slice(qr * w, (qr + 1) * w), slice(krr * w, (krr + 1) * w))
                        if rs <= kr < rs + kh:
                            bias_ref[dst] = toe[e][kr - r + kh - 1]
                        else:
                            bias_ref[dst] = jnp.full((w, w), MASK_VALUE, F32)

    vc = vc_ref[0]
    nq = qr_n * w
    lane = lax.broadcasted_iota(jnp.int32, (nq, q_ref.shape[2]), 1)

    kc = kc_ref[0]
    trans_b = (((1,), (1,)), ((), ()))

    chains = []
    for k in range(nblk):
        variant = 0 if k == 0 else (2 if k == nblk - 1 else 1)
        ws = _na_window_start(k, rows)
        win = slice(ws * w, (ws + wrows) * w)
        q = q_ref[0, k * nq:(k + 1) * nq, :]
        for e in range(2):
            qe = jnp.where((lane >= e * hd) & (lane < (e + 1) * hd), q, jnp.zeros_like(q))
            ci = len(chains)
            sl_ref[ci] = lax.dot_general(qe, k_ref[0, win, :], trans_b, preferred_element_type=F32) + bias_ref[e, variant]
            sc_ref[ci] = lax.dot_general(qe, kc, trans_b, preferred_element_type=F32)
            chains.append((k, e, win, ci))

    def head_values(v, e):
        vlane = lax.broadcasted_iota(jnp.int32, v.shape, 1)
        return jnp.where((vlane >= e * hd) & (vlane < (e + 1) * hd), v, jnp.ones_like(v))

    vc_e = [head_values(vc, e) for e in range(2)]
    outs = {}
    for k, e, win, ci in chains:
        m = jnp.maximum(jnp.max(sl_ref[ci], axis=-1, keepdims=True), jnp.max(sc_ref[ci], axis=-1, keepdims=True))
        pl_ref[ci] = jnp.exp2(sl_ref[ci] - m).astype(BF16)
        pc_ref[ci] = jnp.exp2(sc_ref[ci] - m).astype(BF16)
        o = jnp.dot(pl_ref[ci], head_values(v_ref[0, win, :], e), preferred_element_type=F32)
        o += jnp.dot(pc_ref[ci], vc_e[e], preferred_element_type=F32)
        outs[k, e] = o / pltpu.roll(o, hd, 1)
        if e == 1:
            o_ref[0, k * nq:(k + 1) * nq, :] = jnp.where(lane < hd, outs[k, 0], outs[k, 1]).astype(o_ref.dtype)


def _na_bias_table(rpb):
    padded = jnp.pad(rpb * LOG2E, ((0, 0), (0, 0), (0, LANES - rpb.shape[2])))
    return jnp.roll(padded, -(NA_KW - 1), axis=2)


def _na_attention(qkv, kvc, rpb):
    b, s, d3 = qkv.shape
    d = d3 // 3
    lc = kvc.shape[1]
    rows = s // GRID_W
    pairs = d // LANES
    kh, wrows = _na_window(rows)
    qrows = NA_QROWS * GRID_W
    nchain = 2 * (rows // NA_QROWS)
    assert kh == 2 * NA_QROWS and rows % NA_QROWS == 0 and rows >= wrows + NA_QROWS, rows
    mtab = _na_bias_table(rpb)
    return pl.pallas_call(
        functools.partial(_na_kernel, rows=rows),
        grid=(pairs, b),
        in_specs=[
            pl.BlockSpec((2,) + mtab.shape[1:], lambda p, b_: (p, 0, 0)),
            pl.BlockSpec((1, s, LANES), lambda p, b_: (b_, 0, p)),
            pl.BlockSpec((1, s, LANES), lambda p, b_: (b_, 0, pairs + p)),
            pl.BlockSpec((1, s, LANES), lambda p, b_: (b_, 0, 2 * pairs + p)),
            pl.BlockSpec((1, lc, LANES), lambda p, b_: (b_, 0, p)),
            pl.BlockSpec((1, lc, LANES), lambda p, b_: (b_, 0, pairs + p)),
        ],
        out_specs=pl.BlockSpec((1, s, LANES), lambda p, b_: (b_, 0, p)),
        out_shape=jax.ShapeDtypeStruct((b, s, d), BF16),
        scratch_shapes=[pltpu.VMEM((2, 3, qrows, wrows * GRID_W), F32),
                        pltpu.VMEM((nchain, qrows, wrows * GRID_W), F32), pltpu.VMEM((nchain, qrows, lc), F32),
                        pltpu.VMEM((nchain, qrows, wrows * GRID_W), BF16), pltpu.VMEM((nchain, qrows, lc), BF16)],
        compiler_params=_cparams(("arbitrary", "arbitrary")),
        name="na_attention",
    )(mtab, qkv, qkv, qkv, kvc, kvc)


def kernel(x, c, ctx, c_ctx, ada_w, ada_b, norm1_w, norm2_w, ret_w_in, ret_decay_fwd, ret_decay_bwd, ret_w_out,
           na_w_qkv, na_rpb, na_w_out, ffn_w_up, ffn_conv_w, ffn_conv_b, ffn_w_down, final_norm_w):
    b, seq, d = x.shape
    depth = ada_w.shape[0]
    n_mixers = 2

    pad = BF16_SUBLANES - (b + 1) % BF16_SUBLANES
    cc = jnp.concatenate([c, c_ctx[None, :], jnp.zeros((pad, d), F32)], axis=0)
    mods, ret_w_in_bf16 = _ada(cc, ada_w, ada_b, ret_w_in)
    mods = mods.reshape(depth, cc.shape[0], 6, d)

    def mod(i, k):
        return mods[i, :b, k][:, None, :], mods[i, b:b + 1, k][:, None, :]

    bf16_w = None
    for i in range(depth):
        last = i == depth - 1
        j = i // n_mixers
        (sh1, csh1), (sc1, csc1), (g1, cg1) = mod(i, 0), mod(i, 1), mod(i, 2)
        (sh2, csh2), (sc2, csc2), (g2, cg2) = mod(i, 3), mod(i, 4), mod(i, 5)
        if i % n_mixers == 0:
            hk = ret_w_in.shape[2] // 6
            dk = hk // RET_HEADS
            w_in = ret_w_in_bf16[j]
            kw = dict(scale_cols=(hk, 2 * hk), scale_val=dk ** -0.5, silu_cols=(4 * hk, 6 * hk))
            rope = dict(rope=_rope_tables(seq, dk), rope_cols=2 * hk)
            if bf16_w is None:
                pl_, bf16_w = _proj(x, norm1_w[i], sh1, sc1, w_in, **rope, **kw,
                                    cast=(ffn_w_up, ffn_w_down, ret_w_out, na_w_qkv, na_w_out))
                ffn_w = (bf16_w[0], ffn_conv_w, ffn_conv_b[:, None, :], bf16_w[1])
            else:
                pl_ = _proj(x, norm1_w[i], sh1, sc1, w_in, **rope, **kw)
            pc = _proj(ctx, norm1_w[i], csh1, csc1, w_in, **kw)
            lg = jnp.stack([-jnp.exp(ret_decay_fwd[j].astype(F32)), -jnp.exp(ret_decay_bwd[j].astype(F32))])
            y, yc = _retention(lg, pl_, pc)
            w_out = bf16_w[2][j]
        else:
            w_qkv = bf16_w[3][j]
            qkv = _proj(x, norm1_w[i], sh1, sc1, w_qkv, scale_cols=(0, d), scale_val=(d // NA_HEADS) ** -0.5 * LOG2E)
            kvc = _proj(ctx, norm1_w[i], csh1, csc1, w_qkv[:, d:])
            if not last:
                raise NotImplementedError("context output of the attention mixer is only needed for depth > 2")
            y, yc = _na_attention(qkv, kvc, na_rpb[j]), None
            w_out = bf16_w[4][j]
        x, h2 = _outproj(y, w_out, x, g1, norm2_w[i], sh2, sc2)
        x = _ffn(x, h2, g2, *ffn_w, final_norm_w, layer=i, final_norm=last)
        if not last:
            ctx, hc2 = _outproj(yc, w_out, ctx, cg1, norm2_w[i], csh2, csc2)
            ctx = _ffn(ctx, hc2, cg2, *ffn_w, final_norm_w, layer=i, final_norm=False)
    return x
```

```python
import functools

import jax
import jax.numpy as jnp
import numpy as np
from jax import lax
from jax.experimental import pallas as pl
from jax.experimental.pallas import tpu as pltpu

F32 = jnp.float32
BF16 = jnp.bfloat16

GRID_W = 64
RET_HEADS = 4
ROPE_BASE = 10000.0
NA_HEADS = 16
NA_MAX_KH = 8
NA_KW = 16
CONV_W = 3
NORM_EPS = 1e-6

LANES = 128
SUBLANES = 8
BF16_SUBLANES = 16
VMEM_LIMIT_BYTES = 56 * 1024 * 1024

ADA_COL_TILE = 3072
ROW_TILE = 1024
PROJ_ROW_TILES = (1024, 512, 256)
PROJ_VMEM_BUDGET = 40 * 1024 * 1024
MM_SUB = 256
RET_CHUNK = 256
NA_QROWS = 4
MASK_VALUE = -1e30
LOG2E = 1.4426950408889634


def _cparams(sem):
    return pltpu.CompilerParams(dimension_semantics=sem, vmem_limit_bytes=VMEM_LIMIT_BYTES)


def _bsel(arr):
    if arr.shape[0] == 1:
        return lambda b, *_: (0, 0, 0)
    return lambda b, *_: (b, 0, 0)


def _norm_mod(x, nw, shift, scale):
    y = x * lax.rsqrt(jnp.mean(x * x, axis=-1, keepdims=True) + NORM_EPS)
    return (y * nw) * (1.0 + scale) + shift


def _silu(x):
    return x / (1.0 + jnp.exp(-x))


def _ada_kernel(c_ref, w_ref, b_ref, ci_ref, o_ref, co_ref):
    s = _silu(c_ref[...])
    o_ref[0] = jnp.dot(s, w_ref[0], preferred_element_type=F32) + b_ref[0]
    co_ref[...] = ci_ref[...].astype(co_ref.dtype)


def _ada(cc, ada_w, ada_b, cast):
    depth, d, n = ada_w.shape
    tn = ADA_COL_TILE
    nj = n // tn
    cast2d = cast.reshape(-1, cast.shape[-1])
    assert cast2d.shape[0] % (depth * nj * BF16_SUBLANES) == 0, cast.shape
    slab = pl.BlockSpec((cast2d.shape[0] // (depth * nj), cast2d.shape[1]), lambda i, j: (i * nj + j, 0))
    mods, converted = pl.pallas_call(
        _ada_kernel,
        grid=(depth, nj),
        in_specs=[
            pl.BlockSpec(cc.shape, lambda i, j: (0, 0)),
            pl.BlockSpec((1, d, tn), lambda i, j: (i, 0, j)),
            pl.BlockSpec((1, 1, tn), lambda i, j: (i, 0, j)),
            slab,
        ],
        out_specs=[pl.BlockSpec((1, cc.shape[0], tn), lambda i, j: (i, 0, j)), slab],
        out_shape=[jax.ShapeDtypeStruct((depth, cc.shape[0], n), F32), jax.ShapeDtypeStruct(cast2d.shape, BF16)],
        compiler_params=_cparams(("parallel", "parallel")),
        name="ada",
    )(cc, ada_w, ada_b.reshape(depth, 1, n), cast2d)
    return mods, converted.reshape(cast.shape)


def _pipelined_units(units, matmul, epilogue):
    acc = matmul(units[0])
    for n, u in enumerate(units):
        nxt = matmul(units[n + 1]) if n + 1 < len(units) else None
        epilogue(acc, u)
        acc = nxt


def _proj_kernel(*refs, tn, rope_cols, scale_cols, scale_val, silu_cols, n_cast):
    n_in = 7 if rope_cols else 5
    cast_in, cast_out = refs[n_in:n_in + n_cast], refs[n_in + n_cast + 1:n_in + 2 * n_cast + 1]
    refs = refs[:n_in] + (refs[n_in + n_cast],) + refs[n_in + 2 * n_cast + 1:]
    if rope_cols:
        x_ref, nw_ref, sh_ref, sc_ref, w_ref, cos_ref, sin_ref, o_ref, h_ref = refs
    else:
        x_ref, nw_ref, sh_ref, sc_ref, w_ref, o_ref, h_ref = refs
    tm = x_ref.shape[1]
    sub = min(MM_SUB, tm)

    def matmul(u):
        s, j = u
        rows = slice(s * sub, (s + 1) * sub)
        if j == 0:
            h_ref[rows] = _norm_mod(x_ref[0, rows], nw_ref[...], sh_ref[0], sc_ref[0]).astype(BF16)
        return jnp.dot(h_ref[rows], w_ref[:, j * tn:(j + 1) * tn], preferred_element_type=F32)

    def epilogue(acc, u):
        s, j = u
        if u == (0, 1):
            for src, dst in zip(cast_in, cast_out):
                dst[...] = src[...].astype(dst.dtype)
        rows = slice(s * sub, (s + 1) * sub)
        col = j * tn
        if scale_cols[0] <= col < scale_cols[1]:
            acc = acc * scale_val
        if silu_cols[0] <= col < silu_cols[1]:
            acc = _silu(acc)
        if col >= rope_cols:
            o_ref[0, rows, col:col + tn] = acc.astype(o_ref.dtype)
            return
        for g in range(tn // LANES):
            part = acc[:, g * LANES:(g + 1) * LANES]
            t = (g % 2) * LANES
            rot = part * cos_ref[rows, t:t + LANES] + pltpu.roll(part, LANES // 2, 1) * sin_ref[rows, t:t + LANES]
            o_ref[0, rows, col + g * LANES:col + (g + 1) * LANES] = rot.astype(o_ref.dtype)

    units = [(s, j) for s in range(tm // sub) for j in range(w_ref.shape[1] // tn)]
    _pipelined_units(units, matmul, epilogue)


def _proj_row_tile(l, d, n, rope_width):
    for tm in PROJ_ROW_TILES:
        if tm > l or l % tm:
            continue
        blocks = 2 * tm * d * 4 + 2 * tm * n * 2 + 4 * tm * rope_width * 4
        if d * n * 2 + blocks + tm * d * 2 <= PROJ_VMEM_BUDGET:
            return tm
    raise ValueError((l, d, n))


def _proj(x, nw, shift, scale, w, *, tn=512, rope=None, rope_cols=0, scale_cols=(0, 0), scale_val=1.0, silu_cols=(0, 0),
          cast=()):
    b, l, d = x.shape
    n = w.shape[1]
    tm = _proj_row_tile(l, d, n, rope[0].shape[1] if rope is not None else 0)
    assert all(v % tn == 0 for v in (n, rope_cols) + tuple(scale_cols) + tuple(silu_cols))
    steps = b * (l // tm)
    cast2d = [a.reshape(-1, a.shape[-1]) for a in cast]
    assert all(a.shape[0] % (steps * BF16_SUBLANES) == 0 for a in cast2d), [a.shape for a in cast2d]
    assert not cast or n // tn >= 2
    slab = lambda a: pl.BlockSpec((a.shape[0] // steps, a.shape[1]), lambda b_, i: (b_ * (l // tm) + i, 0))
    in_specs = [
        pl.BlockSpec((1, tm, d), lambda b_, i: (b_, i, 0)),
        pl.BlockSpec((1, d), lambda b_, i: (0, 0)),
        pl.BlockSpec((1, 1, d), _bsel(shift)),
        pl.BlockSpec((1, 1, d), _bsel(scale)),
        pl.BlockSpec((d, n), lambda b_, i: (0, 0), pipeline_mode=pl.Buffered(1)),
    ]
    args = [x, nw.reshape(1, d), shift, scale, w]
    if rope is not None:
        cos, sin = rope
        in_specs += [pl.BlockSpec((tm, cos.shape[1]), lambda b_, i: (i, 0))] * 2
        args += [cos, sin]
    kern = functools.partial(_proj_kernel, tn=tn, rope_cols=(rope_cols if rope is not None else 0),
                             scale_cols=scale_cols, scale_val=scale_val, silu_cols=silu_cols, n_cast=len(cast))
    out, *converted = pl.pallas_call(
        kern,
        grid=(b, l // tm),
        in_specs=in_specs + [slab(a) for a in cast2d],
        out_specs=[pl.BlockSpec((1, tm, n), lambda b_, i: (b_, i, 0))] + [slab(a) for a in cast2d],
        out_shape=[jax.ShapeDtypeStruct((b, l, n), BF16)] + [jax.ShapeDtypeStruct(a.shape, BF16) for a in cast2d],
        scratch_shapes=[pltpu.VMEM((tm, d), BF16)],
        compiler_params=_cparams(("parallel", "parallel")),
        name="proj",
    )(*args, *cast2d)
    if not cast:
        return out
    return out, [c.reshape(a.shape) for c, a in zip(converted, cast)]


def _interleave_rows(v):
    sub, d = v.shape
    return v.reshape(SUBLANES, sub // SUBLANES, d).swapaxes(0, 1).reshape(sub, d)


def _deinterleave_rows(v):
    sub, d = v.shape
    return v.reshape(sub // SUBLANES, SUBLANES, d).swapaxes(0, 1).reshape(sub, d)


def _outproj_kernel(a_ref, w_ref, res_ref, gate_ref, nw_ref, sh_ref, sc_ref, o_ref, h_ref, *, tn):
    tm = a_ref.shape[1]
    sub = min(MM_SUB, tm)
    nj = w_ref.shape[1] // tn

    def matmul(u):
        s, j = u
        return jnp.dot(a_ref[0, s * sub:(s + 1) * sub], w_ref[:, j * tn:(j + 1) * tn], preferred_element_type=F32)

    def epilogue(acc, u):
        s, j = u
        rows, cols = slice(s * sub, (s + 1) * sub), slice(j * tn, (j + 1) * tn)
        o_ref[0, rows, cols] = res_ref[0, rows, cols] + gate_ref[0, :, cols] * acc
        if j == nj - 1:
            h_ref[0, rows, :] = _norm_mod(o_ref[0, rows, :], nw_ref[...], sh_ref[0], sc_ref[0]).astype(h_ref.dtype)

    units = [(s, j) for s in range(tm // sub) for j in range(nj)]
    _pipelined_units(units, matmul, epilogue)


def _outproj(a, w, res, gate, nw, shift, scale, *, tn=512):
    b, l, k = a.shape
    n = w.shape[1]
    tm = min(ROW_TILE, l)
    assert tm % min(MM_SUB, tm) == 0 and n % tn == 0
    vec = lambda arr: pl.BlockSpec((1, 1, n), _bsel(arr))
    return pl.pallas_call(
        functools.partial(_outproj_kernel, tn=tn),
        grid=(b, l // tm),
        in_specs=[
            pl.BlockSpec((1, tm, k), lambda b_, i: (b_, i, 0)),
            pl.BlockSpec((k, n), lambda b_, i: (0, 0), pipeline_mode=pl.Buffered(1)),
            pl.BlockSpec((1, tm, n), lambda b_, i: (b_, i, 0)),
            vec(gate),
            pl.BlockSpec((1, n), lambda b_, i: (0, 0)),
            vec(shift), vec(scale),
        ],
        out_specs=[pl.BlockSpec((1, tm, n), lambda b_, i: (b_, i, 0))] * 2,
        out_shape=[jax.ShapeDtypeStruct((b, l, n), F32), jax.ShapeDtypeStruct((b, l, n), BF16)],
        compiler_params=_cparams(("parallel", "parallel")),
        name="outproj",
    )(a, w, res, gate, nw.reshape(1, n), shift, scale)


FFN_TILE = 256
FFN_HALO = BF16_SUBLANES
FFN_SUB = 1024
FFN_ROW_TILE = 1024


def _ffn_kernel(x_ref, hm_ref, hp_ref, hn_ref, g_ref, wu_ref, cw_ref, cb_ref, wd_ref, fw_ref,
                o_ref, h_ref, ua_ref, ug_ref, act_ref, *, final_norm):
    i = pl.program_id(1)
    tm = x_ref.shape[1]
    hl = FFN_HALO
    tf = FFN_TILE
    f = wd_ref.shape[0]
    sub = min(FFN_SUB, tm)
    nsub = tm // sub
    chunks = [(hl * (c > 0) + c * sub, hl + (c + 1) * sub + hl * (c == nsub - 1)) for c in range(nsub)]

    def stage(c):
        if c == 0:
            h_ref[0:hl] = jnp.where(i > 0, hp_ref[0], jnp.zeros_like(hp_ref[0]))
        if c == nsub - 1:
            h_ref[hl + tm:2 * hl + tm] = jnp.where(i < pl.num_programs(1) - 1, hn_ref[0], jnp.zeros_like(hn_ref[0]))
        h_ref[hl + c * sub:hl + (c + 1) * sub] = _interleave_rows(hm_ref[0, c * sub:(c + 1) * sub].astype(F32)).astype(BF16)

    def conv(u_ref, slot, col, s):
        base = hl + s * sub
        st = SUBLANES
        cw = cw_ref[:, col:col + tf]
        blk = u_ref[slot, base - st:base + sub + st]
        sublane = lax.broadcasted_iota(jnp.int32, (st, tf), 0)
        first_prev = jnp.where(sublane == 0, pltpu.roll(blk[0:st], 1, 0), pltpu.roll(blk[sub:sub + st], 1, 0))
        last_next = jnp.where(sublane == st - 1, pltpu.roll(blk[sub + st:sub + 2 * st], st - 1, 0),
                              pltpu.roll(blk[st:2 * st], st - 1, 0))
        prev = jnp.concatenate([first_prev, blk[st:sub]], axis=0)
        nxt = jnp.concatenate([blk[2 * st:sub + st], last_next], axis=0)
        return (prev * cw[0:1] + blk[st:sub + st] * cw[1:2] + nxt * cw[2:3]) + cb_ref[:, col:col + tf]

    def up(t, c):
        lo, hi = chunks[c]
        if t == 0:
            stage(c)
        ua_ref[t % 2, lo:hi] = jnp.dot(h_ref[lo:hi], wu_ref[:, t * tf:(t + 1) * tf], preferred_element_type=F32)
        ug_ref[t % 2, lo:hi] = jnp.dot(h_ref[lo:hi], wu_ref[:, f + t * tf:f + (t + 1) * tf], preferred_element_type=F32)

    units = [(t, c) for t in range(f // tf) for c in range(nsub)]
    up(*units[0])
    for n, (t, s) in enumerate(units):
        if n + 1 < len(units):
            up(*units[n + 1])
        act = _silu(conv(ua_ref, t % 2, t * tf, s)) * conv(ug_ref, t % 2, f + t * tf, s)
        act_ref[s * sub:(s + 1) * sub, t * tf:(t + 1) * tf] = act.astype(BF16)

    for s in range(nsub):
        rows = slice(s * sub, (s + 1) * sub)
        down = _deinterleave_rows(jnp.dot(act_ref[rows], wd_ref[...], preferred_element_type=F32))
        y = x_ref[0, rows] + g_ref[0] * down
        if final_norm:
            y = (y * lax.rsqrt(jnp.mean(y * y, axis=-1, keepdims=True) + NORM_EPS)) * fw_ref[...]
        o_ref[0, rows] = y


def _ffn(x, h, gate, wu, cw, cb, wd, final_w, *, layer, final_norm):
    b, l, d = x.shape
    f = wd.shape[1]
    tf = FFN_TILE
    assert f % tf == 0, wd.shape
    tm = min(FFN_ROW_TILE, l)
    hl = FFN_HALO
    per = tm // hl
    nblk = l // hl
    whole = lambda arr: pl.BlockSpec((None,) + arr.shape[1:], lambda b_, i: (layer, 0, 0), pipeline_mode=pl.Buffered(1))
    return pl.pallas_call(
        functools.partial(_ffn_kernel, final_norm=final_norm),
        grid=(b, l // tm),
        in_specs=[
            pl.BlockSpec((1, tm, d), lambda b_, i: (b_, i, 0)),
            pl.BlockSpec((1, tm, d), lambda b_, i: (b_, i, 0)),
            pl.BlockSpec((1, hl, d), lambda b_, i: (b_, jnp.maximum(i * per - 1, 0), 0)),
            pl.BlockSpec((1, hl, d), lambda b_, i: (b_, jnp.minimum((i + 1) * per, nblk - 1), 0)),
            pl.BlockSpec((1, 1, d), _bsel(gate)),
            whole(wu), whole(cw), whole(cb), whole(wd),
            pl.BlockSpec((1, d), lambda b_, i: (0, 0)),
        ],
        out_specs=pl.BlockSpec((1, tm, d), lambda b_, i: (b_, i, 0)),
        out_shape=jax.ShapeDtypeStruct((b, l, d), F32),
        scratch_shapes=[pltpu.VMEM((tm + 2 * hl, d), BF16), pltpu.VMEM((2, tm + 2 * hl, tf), F32),
                        pltpu.VMEM((2, tm + 2 * hl, tf), F32), pltpu.VMEM((tm, f), BF16)],
        compiler_params=_cparams(("parallel", "parallel")),
        name="ffn",
    )(x, h, h, h, gate, wu, cw, cb, wd, final_w.reshape(1, d))


def _ret_kernel(lg_ref, ql_ref, kl_ref, vl_ref, gl_ref, qc_ref, kc_ref, vc_ref, gc_ref, ol_ref, oc_ref, sb_ref):
    c = RET_CHUNK
    h = pl.program_id(1)
    lgf = lg_ref[0, h]
    lgb = lg_ref[1, h]
    n_ctx = qc_ref.shape[1] // c
    n_lat = ql_ref.shape[1] // c

    pos = lax.broadcasted_iota(jnp.int32, (c, 1), 0).astype(F32)
    kdec_f = jnp.exp(lgf * (c - 1.0 - pos))
    kdec_b = jnp.exp(lgb * pos)
    qdec_f = jnp.exp(lgf * (pos + 1.0))
    qdec_b = jnp.exp(lgb * (c - pos))
    one = jnp.ones((1, 1), F32)
    cdec_f = jnp.exp(lgf * c * one)
    cdec_b = jnp.exp(lgb * c * one)
    diff = (lax.broadcasted_iota(jnp.int32, (c, c), 0) - lax.broadcasted_iota(jnp.int32, (c, c), 1)).astype(F32)
    dmat = jnp.exp(jnp.where(diff >= 0, lgf, lgb) * jnp.abs(diff))

    seq = [(qc_ref, kc_ref, vc_ref, gc_ref, oc_ref, slice(t * c, (t + 1) * c)) for t in range(n_ctx)]
    seq += [(ql_ref, kl_ref, vl_ref, gl_ref, ol_ref, slice(t * c, (t + 1) * c)) for t in range(n_lat)]
    n = len(seq)

    def kv(idx, kdec):
        _, k_ref, v_ref, _, _, sl = seq[idx]
        kd = (k_ref[0, sl, :].astype(F32) * kdec).astype(BF16)
        return lax.dot_general(kd, v_ref[0, sl, :], (((0,), (0,)), ((), ())), preferred_element_type=F32)

    def qk(idx):
        q_ref, k_ref, _, _, _, sl = seq[idx]
        return lax.dot_general(q_ref[0, sl, :], k_ref[0, sl, :], (((1,), (1,)), ((), ())), preferred_element_type=F32)

    bwd_order = list(range(n_ctx - 1, -1, -1)) + list(range(n - 1, n_ctx - 1, -1))
    s = None
    for pos_, idx in enumerate(bwd_order):
        if s is not None:
            sb_ref[idx] = s.astype(BF16)
        if pos_ + 1 < n:
            upd = kv(idx, kdec_b)
            s = upd if s is None else s * cdec_b + upd

    s = None
    scores = qk(0)
    for idx in range(n):
        q_ref, _, v_ref, g_ref, o_ref, sl = seq[idx]
        qf = q_ref[0, sl, :].astype(F32)
        cross = None
        if idx != bwd_order[0]:
            cross = jnp.dot((qf * qdec_b).astype(BF16), sb_ref[idx], preferred_element_type=F32)
        if s is not None:
            cf = jnp.dot((qf * qdec_f).astype(BF16), s.astype(BF16), preferred_element_type=F32)
            cross = cf if cross is None else cross + cf
        upd = kv(idx, kdec_f) if idx + 1 < n else None
        nxt = qk(idx + 1) if idx + 1 < n else None
        o = jnp.dot((scores * dmat).astype(BF16), v_ref[0, sl, :], preferred_element_type=F32)
        if cross is not None:
            o = o + cross
        if upd is not None:
            s = upd if s is None else s * cdec_f + upd
        o = o * lax.rsqrt(jnp.mean(o * o, axis=-1, keepdims=True) + NORM_EPS)
        o_ref[0, sl, :] = (g_ref[0, sl, :].astype(F32) * o).astype(BF16)
        scores = nxt


def _retention(lg, pl_, pc):
    b, l, n = pl_.shape
    lc = pc.shape[1]
    hh = RET_HEADS
    dk = n // (6 * hh)
    dv = 2 * dk
    c = RET_CHUNK

    def specs(length):
        return [
            pl.BlockSpec((1, length, dk), lambda b_, h: (b_, 0, h)),
            pl.BlockSpec((1, length, dk), lambda b_, h: (b_, 0, hh + h)),
            pl.BlockSpec((1, length, dv), lambda b_, h: (b_, 0, hh + h)),
            pl.BlockSpec((1, length, dv), lambda b_, h: (b_, 0, 2 * hh + h)),
        ]

    return pl.pallas_call(
        _ret_kernel,
        grid=(b, hh),
        in_specs=[pl.BlockSpec(memory_space=pltpu.SMEM)] + specs(l) + specs(lc),
        out_specs=[
            pl.BlockSpec((1, l, dv), lambda b_, h: (b_, 0, h)),
            pl.BlockSpec((1, lc, dv), lambda b_, h: (b_, 0, h)),
        ],
        out_shape=[jax.ShapeDtypeStruct((b, l, hh * dv), BF16), jax.ShapeDtypeStruct((b, lc, hh * dv), BF16)],
        scratch_shapes=[pltpu.VMEM((lc // c + l // c, dk, dv), BF16)],
        compiler_params=_cparams(("parallel", "arbitrary")),
        name="retention",
    )(lg, pl_, pl_, pl_, pl_, pc, pc, pc, pc)


def _rope_tables(seq, dk):
    quarter = dk // 4
    t = np.arange(seq)
    inv = ROPE_BASE ** (-np.arange(quarter, dtype=np.float64) / quarter)
    ang_r = (t // GRID_W)[:, None] * inv[None, :]
    ang_c = (t % GRID_W)[:, None] * inv[None, :]
    cos = np.concatenate([np.cos(ang_r)] * 2 + [np.cos(ang_c)] * 2, axis=-1)
    sin = np.concatenate([-np.sin(ang_r), np.sin(ang_r), -np.sin(ang_c), np.sin(ang_c)], axis=-1)
    return jnp.asarray(cos, F32), jnp.asarray(sin, F32)


def _na_window(rows):
    kh = min(NA_MAX_KH, rows)
    return kh, NA_QROWS + kh


def _na_window_start(blk, rows):
    kh, wrows = _na_window(rows)
    return min(max(blk * NA_QROWS - kh // 2, 0), rows - wrows)


def _na_kernel(m_ref, q_ref, k_ref, v_ref, kc_ref, vc_ref, o_ref, bias_ref, sl_ref, sc_ref, pl_ref, pc_ref, *, rows):
    b = pl.program_id(1)
    kh, wrows = _na_window(rows)
    qr_n = NA_QROWS
    w = GRID_W
    hd = q_ref.shape[2] // 2
    nblk = rows // qr_n

    @pl.when(b == 0)
    def _():
        qc = lax.broadcasted_iota(jnp.int32, (w, LANES), 0)
        kc = lax.broadcasted_iota(jnp.int32, (w, LANES), 1)
        cs = jnp.clip(qc - NA_KW // 2, 0, w - NA_KW)
        col_valid = (kc >= cs) & (kc < cs + NA_KW)
        toe = [[jnp.where(col_valid,
                          pltpu.roll(jnp.broadcast_to(m_ref[e, dr:dr + 1, :], (w, LANES)), 0, 1, stride=1, stride_axis=0),
                          MASK_VALUE)[:, :w]
                for dr in range(2 * kh - 1)] for e in range(2)]
        for variant in range(3):
            r0 = (0, qr_n, rows - qr_n)[variant]
            ws = _na_window_start(r0 // qr_n, rows)
            for qr in range(qr_n):
                r = r0 + qr
                rs = min(max(r - kh // 2, 0), rows - kh)
                for krr in range(wrows):
                    kr = ws + krr
                    for e in range(2):
                        dst = (e, variant, slice(qr * w, (qr + 1) * w), slice(krr * w, (krr + 1) * w))
                        if rs <= kr < rs + kh:
                            bias_ref[dst] = toe[e][kr - r + kh - 1]
                        else:
                            bias_ref[dst] = jnp.full((w, w), MASK_VALUE, F32)

    vc = vc_ref[0]
    nq = qr_n * w
    lane = lax.broadcasted_iota(jnp.int32, (nq, q_ref.shape[2]), 1)

    kc = kc_ref[0]
    trans_b = (((1,), (1,)), ((), ()))

    chains = []
    for k in range(nblk):
        variant = 0 if k == 0 else (2 if k == nblk - 1 else 1)
        ws = _na_window_start(k, rows)
        win = slice(ws * w, (ws + wrows) * w)
        q = q_ref[0, k * nq:(k + 1) * nq, :]
        for e in range(2):
            qe = jnp.where((lane >= e * hd) & (lane < (e + 1) * hd), q, jnp.zeros_like(q))
            ci = len(chains)
            sl_ref[ci] = lax.dot_general(qe, k_ref[0, win, :], trans_b, preferred_element_type=F32) + bias_ref[e, variant]
            sc_ref[ci] = lax.dot_general(qe, kc, trans_b, preferred_element_type=F32)
            chains.append((k, e, win, ci))

    def head_values(v, e):
        vlane = lax.broadcasted_iota(jnp.int32, v.shape, 1)
        return jnp.where((vlane >= e * hd) & (vlane < (e + 1) * hd), v, jnp.ones_like(v))

    vc_e = [head_values(vc, e) for e in range(2)]
    outs = {}
    for k, e, win, ci in chains:
        m = jnp.maximum(jnp.max(sl_ref[ci], axis=-1, keepdims=True), jnp.max(sc_ref[ci], axis=-1, keepdims=True))
        pl_ref[ci] = jnp.exp2(sl_ref[ci] - m).astype(BF16)
        pc_ref[ci] = jnp.exp2(sc_ref[ci] - m).astype(BF16)
        o = jnp.dot(pl_ref[ci], head_values(v_ref[0, win, :], e), preferred_element_type=F32)
        o += jnp.dot(pc_ref[ci], vc_e[e], preferred_element_type=F32)
        outs[k, e] = o / pltpu.roll(o, hd, 1)
        if e == 1:
            o_ref[0, k * nq:(k + 1) * nq, :] = jnp.where(lane < hd, outs[k, 0], outs[k, 1]).astype(o_ref.dtype)


def _na_bias_table(rpb):
    padded = jnp.pad(rpb * LOG2E, ((0, 0), (0, 0), (0, LANES - rpb.shape[2])))
    return jnp.roll(padded, -(NA_KW - 1), axis=2)


def _na_attention(qkv, kvc, rpb):
    b, s, d3 = qkv.shape
    d = d3 // 3
    lc = kvc.shape[1]
    rows = s // GRID_W
    pairs = d // LANES
    kh, wrows = _na_window(rows)
    qrows = NA_QROWS * GRID_W
    nchain = 2 * (rows // NA_QROWS)
    assert kh == 2 * NA_QROWS and rows % NA_QROWS == 0 and rows >= wrows + NA_QROWS, rows
    mtab = _na_bias_table(rpb)
    return pl.pallas_call(
        functools.partial(_na_kernel, rows=rows),
        grid=(pairs, b),
        in_specs=[
            pl.BlockSpec((2,) + mtab.shape[1:], lambda p, b_: (p, 0, 0)),
            pl.BlockSpec((1, s, LANES), lambda p, b_: (b_, 0, p)),
            pl.BlockSpec((1, s, LANES), lambda p, b_: (b_, 0, pairs + p)),
            pl.BlockSpec((1, s, LANES), lambda p, b_: (b_, 0, 2 * pairs + p)),
            pl.BlockSpec((1, lc, LANES), lambda p, b_: (b_, 0, p)),
            pl.BlockSpec((1, lc, LANES), lambda p, b_: (b_, 0, pairs + p)),
        ],
        out_specs=pl.BlockSpec((1, s, LANES), lambda p, b_: (b_, 0, p)),
        out_shape=jax.ShapeDtypeStruct((b, s, d), BF16),
        scratch_shapes=[pltpu.VMEM((2, 3, qrows, wrows * GRID_W), F32),
                        pltpu.VMEM((nchain, qrows, wrows * GRID_W), F32), pltpu.VMEM((nchain, qrows, lc), F32),
                        pltpu.VMEM((nchain, qrows, wrows * GRID_W), BF16), pltpu.VMEM((nchain, qrows, lc), BF16)],
        compiler_params=_cparams(("arbitrary", "arbitrary")),
        name="na_attention",
    )(mtab, qkv, qkv, qkv, kvc, kvc)


def kernel(x, c, ctx, c_ctx, ada_w, ada_b, norm1_w, norm2_w, ret_w_in, ret_decay_fwd, ret_decay_bwd, ret_w_out,
           na_w_qkv, na_rpb, na_w_out, ffn_w_up, ffn_conv_w, ffn_conv_b, ffn_w_down, final_norm_w):
    b, seq, d = x.shape
    depth = ada_w.shape[0]
    n_mixers = 2

    pad = BF16_SUBLANES - (b + 1) % BF16_SUBLANES
    cc = jnp.concatenate([c, c_ctx[None, :], jnp.zeros((pad, d), F32)], axis=0)
    mods, ret_w_in_bf16 = _ada(cc, ada_w, ada_b, ret_w_in)
    mods = mods.reshape(depth, cc.shape[0], 6, d)

    def mod(i, k):
        return mods[i, :b, k][:, None, :], mods[i, b:b + 1, k][:, None, :]

    bf16_w = None
    for i in range(depth):
        last = i == depth - 1
        j = i // n_mixers
        (sh1, csh1), (sc1, csc1), (g1, cg1) = mod(i, 0), mod(i, 1), mod(i, 2)
        (sh2, csh2), (sc2, csc2), (g2, cg2) = mod(i, 3), mod(i, 4), mod(i, 5)
        if i % n_mixers == 0:
            hk = ret_w_in.shape[2] // 6
            dk = hk // RET_HEADS
            w_in = ret_w_in_bf16[j]
            kw = dict(scale_cols=(hk, 2 * hk), scale_val=dk ** -0.5, silu_cols=(4 * hk, 6 * hk))
            rope = dict(rope=_rope_tables(seq, dk), rope_cols=2 * hk)
            if bf16_w is None:
                pl_, bf16_w = _proj(x, norm1_w[i], sh1, sc1, w_in, **rope, **kw,
                                    cast=(ffn_w_up, ffn_w_down, ret_w_out, na_w_qkv, na_w_out))
                ffn_w = (bf16_w[0], ffn_conv_w, ffn_conv_b[:, None, :], bf16_w[1])
            else:
                pl_ = _proj(x, norm1_w[i], sh1, sc1, w_in, **rope, **kw)
            pc = _proj(ctx, norm1_w[i], csh1, csc1, w_in, **kw)
            lg = jnp.stack([-jnp.exp(ret_decay_fwd[j].astype(F32)), -jnp.exp(ret_decay_bwd[j].astype(F32))])
            y, yc = _retention(lg, pl_, pc)
            w_out = bf16_w[2][j]
        else:
            w_qkv = bf16_w[3][j]
            qkv = _proj(x, norm1_w[i], sh1, sc1, w_qkv, scale_cols=(0, d), scale_val=(d // NA_HEADS) ** -0.5 * LOG2E)
            kvc = _proj(ctx, norm1_w[i], csh1, csc1, w_qkv[:, d:])
            if not last:
                raise NotImplementedError("context output of the attention mixer is only needed for depth > 2")
            y, yc = _na_attention(qkv, kvc, na_rpb[j]), None
            w_out = bf16_w[4][j]
        x, h2 = _outproj(y, w_out, x, g1, norm2_w[i], sh2, sc2)
        x = _ffn(x, h2, g2, *ffn_w, final_norm_w, layer=i, final_norm=last)
        if not last:
            ctx, hc2 = _outproj(yc, w_out, ctx, cg1, norm2_w[i], csh2, csc2)
            ctx = _ffn(ctx, hc2, cg2, *ffn_w, final_norm_w, layer=i, final_norm=False)
    return x
```

```python
import functools

import jax
import jax.numpy as jnp
import numpy as np
from jax import lax
from jax.experimental import pallas as pl
from jax.experimental.pallas import tpu as pltpu

F32 = jnp.float32
BF16 = jnp.bfloat16

GRID_W = 64
RET_HEADS = 4
ROPE_BASE = 10000.0
NA_HEADS = 16
NA_MAX_KH = 8
NA_KW = 16
CONV_W = 3
NORM_EPS = 1e-6

LANES = 128
SUBLANES = 8
BF16_SUBLANES = 16
VMEM_LIMIT_BYTES = 56 * 1024 * 1024

ADA_COL_TILE = 3072
ROW_TILE = 1024
PROJ_ROW_TILES = (1024, 512, 256)
PROJ_VMEM_BUDGET = 40 * 1024 * 1024
MM_SUB = 256
RET_CHUNK = 256
NA_QROWS = 4
MASK_VALUE = -1e30
LOG2E = 1.4426950408889634


def _cparams(sem, fusible_inputs=None):
    return pltpu.CompilerParams(dimension_semantics=sem, vmem_limit_bytes=VMEM_LIMIT_BYTES,
                                allow_input_fusion=fusible_inputs)


def _bsel(arr):
    if arr.shape[0] == 1:
        return lambda b, *_: (0, 0, 0)
    return lambda b, *_: (b, 0, 0)


def _norm_mod(x, nw, shift, scale):
    y = x * lax.rsqrt(jnp.mean(x * x, axis=-1, keepdims=True) + NORM_EPS)
    return (y * nw) * (1.0 + scale) + shift


def _silu(x):
    return x / (1.0 + jnp.exp(-x))


def _ada_kernel(c_ref, w_ref, b_ref, ci_ref, o_ref, co_ref):
    s = _silu(c_ref[...])
    o_ref[0] = jnp.dot(s, w_ref[0], preferred_element_type=F32) + b_ref[0]
    co_ref[...] = ci_ref[...].astype(co_ref.dtype)


def _ada(cc, ada_w, ada_b, cast):
    depth, d, n = ada_w.shape
    tn = ADA_COL_TILE
    nj = n // tn
    cast2d = cast.reshape(-1, cast.shape[-1])
    assert cast2d.shape[0] % (depth * nj * BF16_SUBLANES) == 0, cast.shape
    slab = pl.BlockSpec((cast2d.shape[0] // (depth * nj), cast2d.shape[1]), lambda i, j: (i * nj + j, 0))
    mods, converted = pl.pallas_call(
        _ada_kernel,
        grid=(depth, nj),
        in_specs=[
            pl.BlockSpec(cc.shape, lambda i, j: (0, 0)),
            pl.BlockSpec((1, d, tn), lambda i, j: (i, 0, j)),
            pl.BlockSpec((1, 1, tn), lambda i, j: (i, 0, j)),
            slab,
        ],
        out_specs=[pl.BlockSpec((1, cc.shape[0], tn), lambda i, j: (i, 0, j)), slab],
        out_shape=[jax.ShapeDtypeStruct((depth, cc.shape[0], n), F32), jax.ShapeDtypeStruct(cast2d.shape, BF16)],
        compiler_params=_cparams(("parallel", "parallel")),
        name="ada",
    )(cc, ada_w, ada_b.reshape(depth, 1, n), cast2d)
    return mods, converted.reshape(cast.shape)


def _pipelined_units(units, matmul, epilogue):
    acc = matmul(units[0])
    for n, u in enumerate(units):
        nxt = matmul(units[n + 1]) if n + 1 < len(units) else None
        epilogue(acc, u)
        acc = nxt


def _proj_kernel(*refs, tn, rope_cols, scale_cols, scale_val, silu_cols, n_cast):
    n_in = 7 if rope_cols else 5
    cast_in, cast_out = refs[n_in:n_in + n_cast], refs[n_in + n_cast + 1:n_in + 2 * n_cast + 1]
    refs = refs[:n_in] + (refs[n_in + n_cast],) + refs[n_in + 2 * n_cast + 1:]
    if rope_cols:
        x_ref, nw_ref, sh_ref, sc_ref, w_ref, cos_ref, sin_ref, o_ref, h_ref = refs
    else:
        x_ref, nw_ref, sh_ref, sc_ref, w_ref, o_ref, h_ref = refs
    tm = x_ref.shape[1]
    sub = min(MM_SUB, tm)

    def matmul(u):
        s, j = u
        rows = slice(s * sub, (s + 1) * sub)
        if j == 0:
            h_ref[rows] = _norm_mod(x_ref[0, rows], nw_ref[...], sh_ref[0], sc_ref[0]).astype(BF16)
        return jnp.dot(h_ref[rows], w_ref[:, j * tn:(j + 1) * tn], preferred_element_type=F32)

    def epilogue(acc, u):
        s, j = u
        if u == (0, 1):
            for src, dst in zip(cast_in, cast_out):
                dst[...] = src[...].astype(dst.dtype)
        rows = slice(s * sub, (s + 1) * sub)
        col = j * tn
        if scale_cols[0] <= col < scale_cols[1]:
            acc = acc * scale_val
        if silu_cols[0] <= col < silu_cols[1]:
            acc = _silu(acc)
        if col >= rope_cols:
            o_ref[0, rows, col:col + tn] = acc.astype(o_ref.dtype)
            return
        for g in range(tn // LANES):
            part = acc[:, g * LANES:(g + 1) * LANES]
            t = (g % 2) * LANES
            rot = part * cos_ref[rows, t:t + LANES] + pltpu.roll(part, LANES // 2, 1) * sin_ref[rows, t:t + LANES]
            o_ref[0, rows, col + g * LANES:col + (g + 1) * LANES] = rot.astype(o_ref.dtype)

    units = [(s, j) for s in range(tm // sub) for j in range(w_ref.shape[1] // tn)]
    _pipelined_units(units, matmul, epilogue)


def _proj_row_tile(l, d, n, rope_width):
    for tm in PROJ_ROW_TILES:
        if tm > l or l % tm:
            continue
        blocks = 2 * tm * d * 4 + 2 * tm * n * 2 + 4 * tm * rope_width * 4
        if d * n * 2 + blocks + tm * d * 2 <= PROJ_VMEM_BUDGET:
            return tm
    raise ValueError((l, d, n))


def _proj(x, nw, shift, scale, w, *, tn=512, rope=None, rope_cols=0, scale_cols=(0, 0), scale_val=1.0, silu_cols=(0, 0),
          cast=()):
    b, l, d = x.shape
    n = w.shape[1]
    tm = _proj_row_tile(l, d, n, rope[0].shape[1] if rope is not None else 0)
    assert all(v % tn == 0 for v in (n, rope_cols) + tuple(scale_cols) + tuple(silu_cols))
    steps = b * (l // tm)
    cast2d = [a.reshape(-1, a.shape[-1]) for a in cast]
    assert all(a.shape[0] % (steps * BF16_SUBLANES) == 0 for a in cast2d), [a.shape for a in cast2d]
    assert not cast or n // tn >= 2
    slab = lambda a: pl.BlockSpec((a.shape[0] // steps, a.shape[1]), lambda b_, i: (b_ * (l // tm) + i, 0))
    in_specs = [
        pl.BlockSpec((1, tm, d), lambda b_, i: (b_, i, 0)),
        pl.BlockSpec((1, d), lambda b_, i: (0, 0)),
        pl.BlockSpec((1, 1, d), _bsel(shift)),
        pl.BlockSpec((1, 1, d), _bsel(scale)),
        pl.BlockSpec((d, n), lambda b_, i: (0, 0), pipeline_mode=pl.Buffered(1)),
    ]
    args = [x, nw.reshape(1, d), shift, scale, w]
    if rope is not None:
        cos, sin = rope
        in_specs += [pl.BlockSpec((tm, cos.shape[1]), lambda b_, i: (i, 0))] * 2
        args += [cos, sin]
    kern = functools.partial(_proj_kernel, tn=tn, rope_cols=(rope_cols if rope is not None else 0),
                             scale_cols=scale_cols, scale_val=scale_val, silu_cols=silu_cols, n_cast=len(cast))
    out, *converted = pl.pallas_call(
        kern,
        grid=(b, l // tm),
        in_specs=in_specs + [slab(a) for a in cast2d],
        out_specs=[pl.BlockSpec((1, tm, n), lambda b_, i: (b_, i, 0))] + [slab(a) for a in cast2d],
        out_shape=[jax.ShapeDtypeStruct((b, l, n), BF16)] + [jax.ShapeDtypeStruct(a.shape, BF16) for a in cast2d],
        scratch_shapes=[pltpu.VMEM((tm, d), BF16)],
        compiler_params=_cparams(("parallel", "parallel"), [k == 4 for k in range(len(args) + len(cast2d))]),
        name="proj",
    )(*args, *cast2d)
    if not cast:
        return out
    return out, [c.reshape(a.shape) for c, a in zip(converted, cast)]


def _interleave_rows(v):
    sub, d = v.shape
    return v.reshape(SUBLANES, sub // SUBLANES, d).swapaxes(0, 1).reshape(sub, d)


def _deinterleave_rows(v):
    sub, d = v.shape
    return v.reshape(sub // SUBLANES, SUBLANES, d).swapaxes(0, 1).reshape(sub, d)


def _outproj_kernel(a_ref, w_ref, res_ref, gate_ref, nw_ref, sh_ref, sc_ref, o_ref, h_ref, *, tn):
    tm = a_ref.shape[1]
    sub = min(MM_SUB, tm)
    nj = w_ref.shape[1] // tn

    def matmul(u):
        s, j = u
        return jnp.dot(a_ref[0, s * sub:(s + 1) * sub], w_ref[:, j * tn:(j + 1) * tn], preferred_element_type=F32)

    def epilogue(acc, u):
        s, j = u
        rows, cols = slice(s * sub, (s + 1) * sub), slice(j * tn, (j + 1) * tn)
        o_ref[0, rows, cols] = res_ref[0, rows, cols] + gate_ref[0, :, cols] * acc
        if j == nj - 1:
            h_ref[0, rows, :] = _norm_mod(o_ref[0, rows, :], nw_ref[...], sh_ref[0], sc_ref[0]).astype(h_ref.dtype)

    units = [(s, j) for s in range(tm // sub) for j in range(nj)]
    _pipelined_units(units, matmul, epilogue)


def _outproj(a, w, res, gate, nw, shift, scale, *, tn=512):
    b, l, k = a.shape
    n = w.shape[1]
    tm = min(ROW_TILE, l)
    assert tm % min(MM_SUB, tm) == 0 and n % tn == 0
    vec = lambda arr: pl.BlockSpec((1, 1, n), _bsel(arr))
    return pl.pallas_call(
        functools.partial(_outproj_kernel, tn=tn),
        grid=(b, l // tm),
        in_specs=[
            pl.BlockSpec((1, tm, k), lambda b_, i: (b_, i, 0)),
            pl.BlockSpec((k, n), lambda b_, i: (0, 0), pipeline_mode=pl.Buffered(1)),
            pl.BlockSpec((1, tm, n), lambda b_, i: (b_, i, 0)),
            vec(gate),
            pl.BlockSpec((1, n), lambda b_, i: (0, 0)),
            vec(shift), vec(scale),
        ],
        out_specs=[pl.BlockSpec((1, tm, n), lambda b_, i: (b_, i, 0))] * 2,
        out_shape=[jax.ShapeDtypeStruct((b, l, n), F32), jax.ShapeDtypeStruct((b, l, n), BF16)],
        compiler_params=_cparams(("parallel", "parallel")),
        name="outproj",
    )(a, w, res, gate, nw.reshape(1, n), shift, scale)


FFN_TILE = 256
FFN_HALO = BF16_SUBLANES
FFN_SUB = 1024
FFN_ROW_TILE = 1024


def _ffn_kernel(x_ref, hm_ref, hp_ref, hn_ref, g_ref, wu_ref, cw_ref, cb_ref, wd_ref, fw_ref,
                o_ref, h_ref, ua_ref, ug_ref, act_ref, *, final_norm):
    i = pl.program_id(1)
    tm = x_ref.shape[1]
    hl = FFN_HALO
    tf = FFN_TILE
    f = wd_ref.shape[0]
    sub = min(FFN_SUB, tm)
    nsub = tm // sub
    chunks = [(hl * (c > 0) + c * sub, hl + (c + 1) * sub + hl * (c == nsub - 1)) for c in range(nsub)]

    def stage(c):
        if c == 0:
            h_ref[0:hl] = jnp.where(i > 0, hp_ref[0], jnp.zeros_like(hp_ref[0]))
        if c == nsub - 1:
            h_ref[hl + tm:2 * hl + tm] = jnp.where(i < pl.num_programs(1) - 1, hn_ref[0], jnp.zeros_like(hn_ref[0]))
        h_ref[hl + c * sub:hl + (c + 1) * sub] = _interleave_rows(hm_ref[0, c * sub:(c + 1) * sub].astype(F32)).astype(BF16)

    def conv(u_ref, slot, col, s):
        base = hl + s * sub
        st = SUBLANES
        cw = cw_ref[:, col:col + tf]
        blk = u_ref[slot, base - st:base + sub + st]
        sublane = lax.broadcasted_iota(jnp.int32, (st, tf), 0)
        first_prev = jnp.where(sublane == 0, pltpu.roll(blk[0:st], 1, 0), pltpu.roll(blk[sub:sub + st], 1, 0))
        last_next = jnp.where(sublane == st - 1, pltpu.roll(blk[sub + st:sub + 2 * st], st - 1, 0),
                              pltpu.roll(blk[st:2 * st], st - 1, 0))
        prev = jnp.concatenate([first_prev, blk[st:sub]], axis=0)
        nxt = jnp.concatenate([blk[2 * st:sub + st], last_next], axis=0)
        return (prev * cw[0:1] + blk[st:sub + st] * cw[1:2] + nxt * cw[2:3]) + cb_ref[:, col:col + tf]

    def up(t, c):
        lo, hi = chunks[c]
        if t == 0:
            stage(c)
        ua_ref[t % 2, lo:hi] = jnp.dot(h_ref[lo:hi], wu_ref[:, t * tf:(t + 1) * tf], preferred_element_type=F32)
        ug_ref[t % 2, lo:hi] = jnp.dot(h_ref[lo:hi], wu_ref[:, f + t * tf:f + (t + 1) * tf], preferred_element_type=F32)

    units = [(t, c) for t in range(f // tf) for c in range(nsub)]
    up(*units[0])
    for n, (t, s) in enumerate(units):
        if n + 1 < len(units):
            up(*units[n + 1])
        act = _silu(conv(ua_ref, t % 2, t * tf, s)) * conv(ug_ref, t % 2, f + t * tf, s)
        act_ref[s * sub:(s + 1) * sub, t * tf:(t + 1) * tf] = act.astype(BF16)

    for s in range(nsub):
        rows = slice(s * sub, (s + 1) * sub)
        down = _deinterleave_rows(jnp.dot(act_ref[rows], wd_ref[...], preferred_element_type=F32))
        y = x_ref[0, rows] + g_ref[0] * down
        if final_norm:
            y = (y * lax.rsqrt(jnp.mean(y * y, axis=-1, keepdims=True) + NORM_EPS)) * fw_ref[...]
        o_ref[0, rows] = y


def _ffn(x, h, gate, wu, cw, cb, wd, final_w, *, layer, final_norm):
    b, l, d = x.shape
    f = wd.shape[1]
    tf = FFN_TILE
    assert f % tf == 0, wd.shape
    tm = min(FFN_ROW_TILE, l)
    hl = FFN_HALO
    per = tm // hl
    nblk = l // hl
    whole = lambda arr: pl.BlockSpec((None,) + arr.shape[1:], lambda b_, i: (layer, 0, 0), pipeline_mode=pl.Buffered(1))
    return pl.pallas_call(
        functools.partial(_ffn_kernel, final_norm=final_norm),
        grid=(b, l // tm),
        in_specs=[
            pl.BlockSpec((1, tm, d), lambda b_, i: (b_, i, 0)),
            pl.BlockSpec((1, tm, d), lambda b_, i: (b_, i, 0)),
            pl.BlockSpec((1, hl, d), lambda b_, i: (b_, jnp.maximum(i * per - 1, 0), 0)),
            pl.BlockSpec((1, hl, d), lambda b_, i: (b_, jnp.minimum((i + 1) * per, nblk - 1), 0)),
            pl.BlockSpec((1, 1, d), _bsel(gate)),
            whole(wu), whole(cw), whole(cb), whole(wd),
            pl.BlockSpec((1, d), lambda b_, i: (0, 0)),
        ],
        out_specs=pl.BlockSpec((1, tm, d), lambda b_, i: (b_, i, 0)),
        out_shape=jax.ShapeDtypeStruct((b, l, d), F32),
        scratch_shapes=[pltpu.VMEM((tm + 2 * hl, d), BF16), pltpu.VMEM((2, tm + 2 * hl, tf), F32),
                        pltpu.VMEM((2, tm + 2 * hl, tf), F32), pltpu.VMEM((tm, f), BF16)],
        compiler_params=_cparams(("parallel", "parallel")),
        name="ffn",
    )(x, h, h, h, gate, wu, cw, cb, wd, final_w.reshape(1, d))


def _ret_kernel(lg_ref, ql_ref, kl_ref, vl_ref, gl_ref, qc_ref, kc_ref, vc_ref, gc_ref, ol_ref, oc_ref, sb_ref):
    c = RET_CHUNK
    h = pl.program_id(1)
    lgf = lg_ref[0, h]
    lgb = lg_ref[1, h]
    n_ctx = qc_ref.shape[1] // c
    n_lat = ql_ref.shape[1] // c

    pos = lax.broadcasted_iota(jnp.int32, (c, 1), 0).astype(F32)
    kdec_f = jnp.exp(lgf * (c - 1.0 - pos))
    kdec_b = jnp.exp(lgb * pos)
    qdec_f = jnp.exp(lgf * (pos + 1.0))
    qdec_b = jnp.exp(lgb * (c - pos))
    one = jnp.ones((1, 1), F32)
    cdec_f = jnp.exp(lgf * c * one)
    cdec_b = jnp.exp(lgb * c * one)
    diff = (lax.broadcasted_iota(jnp.int32, (c, c), 0) - lax.broadcasted_iota(jnp.int32, (c, c), 1)).astype(F32)
    dmat = jnp.exp(jnp.where(diff >= 0, lgf, lgb) * jnp.abs(diff))

    seq = [(qc_ref, kc_ref, vc_ref, gc_ref, oc_ref, slice(t * c, (t + 1) * c)) for t in range(n_ctx)]
    seq += [(ql_ref, kl_ref, vl_ref, gl_ref, ol_ref, slice(t * c, (t + 1) * c)) for t in range(n_lat)]
    n = len(seq)

    def kv(idx, kdec):
        _, k_ref, v_ref, _, _, sl = seq[idx]
        kd = (k_ref[0, sl, :].astype(F32) * kdec).astype(BF16)
        return lax.dot_general(kd, v_ref[0, sl, :], (((0,), (0,)), ((), ())), preferred_element_type=F32)

    def qk(idx):
        q_ref, k_ref, _, _, _, sl = seq[idx]
        return lax.dot_general(q_ref[0, sl, :], k_ref[0, sl, :], (((1,), (1,)), ((), ())), preferred_element_type=F32)

    bwd_order = list(range(n_ctx - 1, -1, -1)) + list(range(n - 1, n_ctx - 1, -1))
    s = None
    for pos_, idx in enumerate(bwd_order):
        if s is not None:
            sb_ref[idx] = s.astype(BF16)
        if pos_ + 1 < n:
            upd = kv(idx, kdec_b)
            s = upd if s is None else s * cdec_b + upd

    s = None
    scores = qk(0)
    for idx in range(n):
        q_ref, _, v_ref, g_ref, o_ref, sl = seq[idx]
        qf = q_ref[0, sl, :].astype(F32)
        cross = None
        if idx != bwd_order[0]:
            cross = jnp.dot((qf * qdec_b).astype(BF16), sb_ref[idx], preferred_element_type=F32)
        if s is not None:
            cf = jnp.dot((qf * qdec_f).astype(BF16), s.astype(BF16), preferred_element_type=F32)
            cross = cf if cross is None else cross + cf
        upd = kv(idx, kdec_f) if idx + 1 < n else None
        nxt = qk(idx + 1) if idx + 1 < n else None
        o = jnp.dot((scores * dmat).astype(BF16), v_ref[0, sl, :], preferred_element_type=F32)
        if cross is not None:
            o = o + cross
        if upd is not None:
            s = upd if s is None else s * cdec_f + upd
        o = o * lax.rsqrt(jnp.mean(o * o, axis=-1, keepdims=True) + NORM_EPS)
        o_ref[0, sl, :] = (g_ref[0, sl, :].astype(F32) * o).astype(BF16)
        scores = nxt


def _retention(lg, pl_, pc):
    b, l, n = pl_.shape
    lc = pc.shape[1]
    hh = RET_HEADS
    dk = n // (6 * hh)
    dv = 2 * dk
    c = RET_CHUNK

    def specs(length):
        return [
            pl.BlockSpec((1, length, dk), lambda b_, h: (b_, 0, h)),
            pl.BlockSpec((1, length, dk), lambda b_, h: (b_, 0, hh + h)),
            pl.BlockSpec((1, length, dv), lambda b_, h: (b_, 0, hh + h)),
            pl.BlockSpec((1, length, dv), lambda b_, h: (b_, 0, 2 * hh + h)),
        ]

    return pl.pallas_call(
        _ret_kernel,
        grid=(b, hh),
        in_specs=[pl.BlockSpec(memory_space=pltpu.SMEM)] + specs(l) + specs(lc),
        out_specs=[
            pl.BlockSpec((1, l, dv), lambda b_, h: (b_, 0, h)),
            pl.BlockSpec((1, lc, dv), lambda b_, h: (b_, 0, h)),
        ],
        out_shape=[jax.ShapeDtypeStruct((b, l, hh * dv), BF16), jax.ShapeDtypeStruct((b, lc, hh * dv), BF16)],
        scratch_shapes=[pltpu.VMEM((lc // c + l // c, dk, dv), BF16)],
        compiler_params=_cparams(("parallel", "arbitrary")),
        name="retention",
    )(lg, pl_, pl_, pl_, pl_, pc, pc, pc, pc)


def _rope_tables(seq, dk):
    quarter = dk // 4
    t = np.arange(seq)
    inv = ROPE_BASE ** (-np.arange(quarter, dtype=np.float64) / quarter)
    ang_r = (t // GRID_W)[:, None] * inv[None, :]
    ang_c = (t % GRID_W)[:, None] * inv[None, :]
    cos = np.concatenate([np.cos(ang_r)] * 2 + [np.cos(ang_c)] * 2, axis=-1)
    sin = np.concatenate([-np.sin(ang_r), np.sin(ang_r), -np.sin(ang_c), np.sin(ang_c)], axis=-1)
    return jnp.asarray(cos, F32), jnp.asarray(sin, F32)


def _na_window(rows):
    kh = min(NA_MAX_KH, rows)
    return kh, NA_QROWS + kh


def _na_window_start(blk, rows):
    kh, wrows = _na_window(rows)
    return min(max(blk * NA_QROWS - kh // 2, 0), rows - wrows)


def _na_kernel(m_ref, q_ref, k_ref, v_ref, kc_ref, vc_ref, o_ref, bias_ref, sl_ref, sc_ref, pl_ref, pc_ref, *, rows):
    b = pl.program_id(1)
    kh, wrows = _na_window(rows)
    qr_n = NA_QROWS
    w = GRID_W
    hd = q_ref.shape[2] // 2
    nblk = rows // qr_n

    @pl.when(b == 0)
    def _():
        qc = lax.broadcasted_iota(jnp.int32, (w, LANES), 0)
        kc = lax.broadcasted_iota(jnp.int32, (w, LANES), 1)
        cs = jnp.clip(qc - NA_KW // 2, 0, w - NA_KW)
        col_valid = (kc >= cs) & (kc < cs + NA_KW)
        toe = [[jnp.where(col_valid,
                          pltpu.roll(jnp.broadcast_to(m_ref[e, dr:dr + 1, :], (w, LANES)), 0, 1, stride=1, stride_axis=0),
                          MASK_VALUE)[:, :w]
                for dr in range(2 * kh - 1)] for e in range(2)]
        for variant in range(3):
            r0 = (0, qr_n, rows - qr_n)[variant]
            ws = _na_window_start(r0 // qr_n, rows)
            for qr in range(qr_n):
                r = r0 + qr
                rs = min(max(r - kh // 2, 0), rows - kh)
                for krr in range(wrows):
                    kr = ws + krr
                    for e in range(2):
                        dst = (e, variant, slice(qr * w, (qr + 1) * w), slice(krr * w, (krr + 1) * w))
                        if rs <= kr < rs + kh:
                            bias_ref[dst] = toe[e][kr - r + kh - 1]
                        else:
                            bias_ref[dst] = jnp.full((w, w), MASK_VALUE, F32)

    vc = vc_ref[0]
    nq = qr_n * w
    lane = lax.broadcasted_iota(jnp.int32, (nq, q_ref.shape[2]), 1)

    kc = kc_ref[0]
    trans_b = (((1,), (1,)), ((), ()))

    chains = []
    for k in range(nblk):
        variant = 0 if k == 0 else (2 if k == nblk - 1 else 1)
        ws = _na_window_start(k, rows)
        win = slice(ws * w, (ws + wrows) * w)
        q = q_ref[0, k * nq:(k + 1) * nq, :]
        for e in range(2):
            qe = jnp.where((lane >= e * hd) & (lane < (e + 1) * hd), q, jnp.zeros_like(q))
            ci = len(chains)
            sl_ref[ci] = lax.dot_general(qe, k_ref[0, win, :], trans_b, preferred_element_type=F32) + bias_ref[e, variant]
            sc_ref[ci] = lax.dot_general(qe, kc, trans_b, preferred_element_type=F32)
            chains.append((k, e, win, ci))

    def head_values(v, e):
        vlane = lax.broadcasted_iota(jnp.int32, v.shape, 1)
        return jnp.where((vlane >= e * hd) & (vlane < (e + 1) * hd), v, jnp.ones_like(v))

    vc_e = [head_values(vc, e) for e in range(2)]
    outs = {}
    for k, e, win, ci in chains:
        m = jnp.maximum(jnp.max(sl_ref[ci], axis=-1, keepdims=True), jnp.max(sc_ref[ci], axis=-1, keepdims=True))
        pl_ref[ci] = jnp.exp2(sl_ref[ci] - m).astype(BF16)
        pc_ref[ci] = jnp.exp2(sc_ref[ci] - m).astype(BF16)
        o = jnp.dot(pl_ref[ci], head_values(v_ref[0, win, :], e), preferred_element_type=F32)
        o += jnp.dot(pc_ref[ci], vc_e[e], preferred_element_type=F32)
        outs[k, e] = o / pltpu.roll(o, hd, 1)
        if e == 1:
            o_ref[0, k * nq:(k + 1) * nq, :] = jnp.where(lane < hd, outs[k, 0], outs[k, 1]).astype(o_ref.dtype)


def _na_bias_table(rpb):
    padded = jnp.pad(rpb * LOG2E, ((0, 0), (0, 0), (0, LANES - rpb.shape[2])))
    return jnp.roll(padded, -(NA_KW - 1), axis=2)


def _na_attention(qkv, kvc, rpb):
    b, s, d3 = qkv.shape
    d = d3 // 3
    lc = kvc.shape[1]
    rows = s // GRID_W
    pairs = d // LANES
    kh, wrows = _na_window(rows)
    qrows = NA_QROWS * GRID_W
    nchain = 2 * (rows // NA_QROWS)
    assert kh == 2 * NA_QROWS and rows % NA_QROWS == 0 and rows >= wrows + NA_QROWS, rows
    mtab = _na_bias_table(rpb)
    return pl.pallas_call(
        functools.partial(_na_kernel, rows=rows),
        grid=(pairs, b),
        in_specs=[
            pl.BlockSpec((2,) + mtab.shape[1:], lambda p, b_: (p, 0, 0)),
            pl.BlockSpec((1, s, LANES), lambda p, b_: (b_, 0, p)),
            pl.BlockSpec((1, s, LANES), lambda p, b_: (b_, 0, pairs + p)),
            pl.BlockSpec((1, s, LANES), lambda p, b_: (b_, 0, 2 * pairs + p)),
            pl.BlockSpec((1, lc, LANES), lambda p, b_: (b_, 0, p)),
            pl.BlockSpec((1, lc, LANES), lambda p, b_: (b_, 0, pairs + p)),
        ],
        out_specs=pl.BlockSpec((1, s, LANES), lambda p, b_: (b_, 0, p)),
        out_shape=jax.ShapeDtypeStruct((b, s, d), BF16),
        scratch_shapes=[pltpu.VMEM((2, 3, qrows, wrows * GRID_W), F32),
                        pltpu.VMEM((nchain, qrows, wrows * GRID_W), F32), pltpu.VMEM((nchain, qrows, lc), F32),
                        pltpu.VMEM((nchain, qrows, wrows * GRID_W), BF16), pltpu.VMEM((nchain, qrows, lc), BF16)],
        compiler_params=_cparams(("arbitrary", "arbitrary")),
        name="na_attention",
    )(mtab, qkv, qkv, qkv, kvc, kvc)


def kernel(x, c, ctx, c_ctx, ada_w, ada_b, norm1_w, norm2_w, ret_w_in, ret_decay_fwd, ret_decay_bwd, ret_w_out,
           na_w_qkv, na_rpb, na_w_out, ffn_w_up, ffn_conv_w, ffn_conv_b, ffn_w_down, final_norm_w):
    b, seq, d = x.shape
    depth = ada_w.shape[0]
    n_mixers = 2

    pad = BF16_SUBLANES - (b + 1) % BF16_SUBLANES
    cc = jnp.concatenate([c, c_ctx[None, :], jnp.zeros((pad, d), F32)], axis=0)
    mods, ret_w_in_bf16 = _ada(cc, ada_w, ada_b, ret_w_in)
    mods = mods.reshape(depth, cc.shape[0], 6, d)

    def mod(i, k):
        return mods[i, :b, k][:, None, :], mods[i, b:b + 1, k][:, None, :]

    bf16_w = None
    for i in range(depth):
        last = i == depth - 1
        j = i // n_mixers
        (sh1, csh1), (sc1, csc1), (g1, cg1) = mod(i, 0), mod(i, 1), mod(i, 2)
        (sh2, csh2), (sc2, csc2), (g2, cg2) = mod(i, 3), mod(i, 4), mod(i, 5)
        if i % n_mixers == 0:
            hk = ret_w_in.shape[2] // 6
            dk = hk // RET_HEADS
            w_in = ret_w_in_bf16[j]
            kw = dict(scale_cols=(hk, 2 * hk), scale_val=dk ** -0.5, silu_cols=(4 * hk, 6 * hk))
            rope = dict(rope=_rope_tables(seq, dk), rope_cols=2 * hk)
            if bf16_w is None:
                pl_, bf16_w = _proj(x, norm1_w[i], sh1, sc1, w_in, **rope, **kw,
                                    cast=(ffn_w_up, ffn_w_down, ret_w_out, na_w_qkv, na_w_out))
                ffn_w = (bf16_w[0], ffn_conv_w, ffn_conv_b[:, None, :], bf16_w[1])
            else:
                pl_ = _proj(x, norm1_w[i], sh1, sc1, w_in, **rope, **kw)
            pc = _proj(ctx, norm1_w[i], csh1, csc1, w_in, **kw)
            lg = jnp.stack([-jnp.exp(ret_decay_fwd[j].astype(F32)), -jnp.exp(ret_decay_bwd[j].astype(F32))])
            y, yc = _retention(lg, pl_, pc)
            w_out = bf16_w[2][j]
        else:
            w_qkv = bf16_w[3][j]
            qkv = _proj(x, norm1_w[i], sh1, sc1, w_qkv, scale_cols=(0, d), scale_val=(d // NA_HEADS) ** -0.5 * LOG2E)
            kvc = _proj(ctx, norm1_w[i], csh1, csc1, w_qkv[:, d:])
            if not last:
                raise NotImplementedError("context output of the attention mixer is only needed for depth > 2")
            y, yc = _na_attention(qkv, kvc, na_rpb[j]), None
            w_out = bf16_w[4][j]
        x, h2 = _outproj(y, w_out, x, g1, norm2_w[i], sh2, sc2)
        x = _ffn(x, h2, g2, *ffn_w, final_norm_w, layer=i, final_norm=last)
        if not last:
            ctx, hc2 = _outproj(yc, w_out, ctx, cg1, norm2_w[i], csh2, csc2)
            ctx = _ffn(ctx, hc2, cg2, *ffn_w, final_norm_w, layer=i, final_norm=False)
    return x
```

```python
import functools

import jax
import jax.numpy as jnp
import numpy as np
from jax import lax
from jax.experimental import pallas as pl
from jax.experimental.pallas import tpu as pltpu

F32 = jnp.float32
BF16 = jnp.bfloat16

GRID_W = 64
RET_HEADS = 4
ROPE_BASE = 10000.0
NA_HEADS = 16
NA_MAX_KH = 8
NA_KW = 16
CONV_W = 3
NORM_EPS = 1e-6

LANES = 128
SUBLANES = 8
BF16_SUBLANES = 16
VMEM_LIMIT_BYTES = 56 * 1024 * 1024

ADA_COL_TILE = 3072
ROW_TILE = 1024
PROJ_ROW_TILES = (1024, 512, 256)
PROJ_VMEM_BUDGET = 40 * 1024 * 1024
MM_SUB = 256
RET_CHUNK = 256
NA_QROWS = 4
MASK_VALUE = -1e30
LOG2E = 1.4426950408889634


def _cparams(sem, fusible_inputs=None):
    return pltpu.CompilerParams(dimension_semantics=sem, vmem_limit_bytes=VMEM_LIMIT_BYTES,
                                allow_input_fusion=fusible_inputs)


def _bsel(arr):
    if arr.shape[0] == 1:
        return lambda b, *_: (0, 0, 0)
    return lambda b, *_: (b, 0, 0)


def _norm_mod(x, nw, shift, scale):
    y = x * lax.rsqrt(jnp.mean(x * x, axis=-1, keepdims=True) + NORM_EPS)
    return (y * nw) * (1.0 + scale) + shift


def _silu(x):
    return x / (1.0 + jnp.exp(-x))


def _ada_kernel(c_ref, w_ref, b_ref, ci_ref, o_ref, co_ref):
    s = _silu(c_ref[...])
    o_ref[0] = jnp.dot(s, w_ref[0], preferred_element_type=F32) + b_ref[0]
    co_ref[...] = ci_ref[...].astype(co_ref.dtype)


def _ada(cc, ada_w, ada_b, cast):
    depth, d, n = ada_w.shape
    tn = ADA_COL_TILE
    nj = n // tn
    cast2d = cast.reshape(-1, cast.shape[-1])
    assert cast2d.shape[0] % (depth * nj * BF16_SUBLANES) == 0, cast.shape
    slab = pl.BlockSpec((cast2d.shape[0] // (depth * nj), cast2d.shape[1]), lambda i, j: (i * nj + j, 0))
    mods, converted = pl.pallas_call(
        _ada_kernel,
        grid=(depth, nj),
        in_specs=[
            pl.BlockSpec(cc.shape, lambda i, j: (0, 0)),
            pl.BlockSpec((1, d, tn), lambda i, j: (i, 0, j)),
            pl.BlockSpec((1, 1, tn), lambda i, j: (i, 0, j)),
            slab,
        ],
        out_specs=[pl.BlockSpec((1, cc.shape[0], tn), lambda i, j: (i, 0, j)), slab],
        out_shape=[jax.ShapeDtypeStruct((depth, cc.shape[0], n), F32), jax.ShapeDtypeStruct(cast2d.shape, BF16)],
        compiler_params=_cparams(("parallel", "parallel")),
        name="ada",
    )(cc, ada_w, ada_b.reshape(depth, 1, n), cast2d)
    return mods, converted.reshape(cast.shape)


def _pipelined_units(units, matmul, epilogue):
    acc = matmul(units[0])
    for n, u in enumerate(units):
        nxt = matmul(units[n + 1]) if n + 1 < len(units) else None
        epilogue(acc, u)
        acc = nxt


def _proj_kernel(*refs, tn, rope_cols, scale_cols, scale_val, silu_cols, n_cast):
    n_in = 7 if rope_cols else 5
    cast_in, cast_out = refs[n_in:n_in + n_cast], refs[n_in + n_cast + 1:n_in + 2 * n_cast + 1]
    refs = refs[:n_in] + (refs[n_in + n_cast],) + refs[n_in + 2 * n_cast + 1:]
    if rope_cols:
        x_ref, nw_ref, sh_ref, sc_ref, w_ref, cos_ref, sin_ref, o_ref, h_ref = refs
    else:
        x_ref, nw_ref, sh_ref, sc_ref, w_ref, o_ref, h_ref = refs
    tm = x_ref.shape[1]
    sub = min(MM_SUB, tm)

    def matmul(u):
        s, j = u
        rows = slice(s * sub, (s + 1) * sub)
        if j == 0:
            h_ref[rows] = _norm_mod(x_ref[0, rows], nw_ref[...], sh_ref[0], sc_ref[0]).astype(BF16)
        return jnp.dot(h_ref[rows], w_ref[:, j * tn:(j + 1) * tn], preferred_element_type=F32)

    def epilogue(acc, u):
        s, j = u
        if u == (0, 1):
            for src, dst in zip(cast_in, cast_out):
                dst[...] = src[...].astype(dst.dtype)
        rows = slice(s * sub, (s + 1) * sub)
        col = j * tn
        if scale_cols[0] <= col < scale_cols[1]:
            acc = acc * scale_val
        if silu_cols[0] <= col < silu_cols[1]:
            acc = _silu(acc)
        if col >= rope_cols:
            o_ref[0, rows, col:col + tn] = acc.astype(o_ref.dtype)
            return
        for g in range(tn // LANES):
            part = acc[:, g * LANES:(g + 1) * LANES]
            t = (g % 2) * LANES
            rot = part * cos_ref[rows, t:t + LANES] + pltpu.roll(part, LANES // 2, 1) * sin_ref[rows, t:t + LANES]
            o_ref[0, rows, col + g * LANES:col + (g + 1) * LANES] = rot.astype(o_ref.dtype)

    units = [(s, j) for s in range(tm // sub) for j in range(w_ref.shape[1] // tn)]
    _pipelined_units(units, matmul, epilogue)


def _proj_row_tile(l, d, n, rope_width):
    for tm in PROJ_ROW_TILES:
        if tm > l or l % tm:
            continue
        blocks = 2 * tm * d * 4 + 2 * tm * n * 2 + 4 * tm * rope_width * 4
        if d * n * 2 + blocks + tm * d * 2 <= PROJ_VMEM_BUDGET:
            return tm
    raise ValueError((l, d, n))


def _proj(x, nw, shift, scale, w, *, tn=512, rope=None, rope_cols=0, scale_cols=(0, 0), scale_val=1.0, silu_cols=(0, 0),
          cast=()):
    b, l, d = x.shape
    n = w.shape[1]
    tm = _proj_row_tile(l, d, n, rope[0].shape[1] if rope is not None else 0)
    assert all(v % tn == 0 for v in (n, rope_cols) + tuple(scale_cols) + tuple(silu_cols))
    steps = b * (l // tm)
    cast2d = [a.reshape(-1, a.shape[-1]) for a in cast]
    assert all(a.shape[0] % (steps * BF16_SUBLANES) == 0 for a in cast2d), [a.shape for a in cast2d]
    assert not cast or n // tn >= 2
    slab = lambda a: pl.BlockSpec((a.shape[0] // steps, a.shape[1]), lambda b_, i: (b_ * (l // tm) + i, 0))
    in_specs = [
        pl.BlockSpec((1, tm, d), lambda b_, i: (b_, i, 0)),
        pl.BlockSpec((1, d), lambda b_, i: (0, 0)),
        pl.BlockSpec((1, 1, d), _bsel(shift)),
        pl.BlockSpec((1, 1, d), _bsel(scale)),
        pl.BlockSpec((d, n), lambda b_, i: (0, 0), pipeline_mode=pl.Buffered(1)),
    ]
    args = [x, nw.reshape(1, d), shift, scale, w]
    if rope is not None:
        cos, sin = rope
        in_specs += [pl.BlockSpec((tm, cos.shape[1]), lambda b_, i: (i, 0))] * 2
        args += [cos, sin]
    kern = functools.partial(_proj_kernel, tn=tn, rope_cols=(rope_cols if rope is not None else 0),
                             scale_cols=scale_cols, scale_val=scale_val, silu_cols=silu_cols, n_cast=len(cast))
    out, *converted = pl.pallas_call(
        kern,
        grid=(b, l // tm),
        in_specs=in_specs + [slab(a) for a in cast2d],
        out_specs=[pl.BlockSpec((1, tm, n), lambda b_, i: (b_, i, 0))] + [slab(a) for a in cast2d],
        out_shape=[jax.ShapeDtypeStruct((b, l, n), BF16)] + [jax.ShapeDtypeStruct(a.shape, BF16) for a in cast2d],
        scratch_shapes=[pltpu.VMEM((tm, d), BF16)],
        compiler_params=_cparams(("parallel", "parallel"), [k in (2, 3, 4) for k in range(len(args) + len(cast2d))]),
        name="proj",
    )(*args, *cast2d)
    if not cast:
        return out
    return out, [c.reshape(a.shape) for c, a in zip(converted, cast)]


def _interleave_rows(v):
    sub, d = v.shape
    return v.reshape(SUBLANES, sub // SUBLANES, d).swapaxes(0, 1).reshape(sub, d)


def _deinterleave_rows(v):
    sub, d = v.shape
    return v.reshape(sub // SUBLANES, SUBLANES, d).swapaxes(0, 1).reshape(sub, d)


def _outproj_kernel(a_ref, w_ref, res_ref, gate_ref, nw_ref, sh_ref, sc_ref, o_ref, h_ref, *, tn):
    tm = a_ref.shape[1]
    sub = min(MM_SUB, tm)
    nj = w_ref.shape[1] // tn

    def matmul(u):
        s, j = u
        return jnp.dot(a_ref[0, s * sub:(s + 1) * sub], w_ref[:, j * tn:(j + 1) * tn], preferred_element_type=F32)

    def epilogue(acc, u):
        s, j = u
        rows, cols = slice(s * sub, (s + 1) * sub), slice(j * tn, (j + 1) * tn)
        o_ref[0, rows, cols] = res_ref[0, rows, cols] + gate_ref[0, :, cols] * acc
        if j == nj - 1:
            h_ref[0, rows, :] = _norm_mod(o_ref[0, rows, :], nw_ref[...], sh_ref[0], sc_ref[0]).astype(h_ref.dtype)

    units = [(s, j) for s in range(tm // sub) for j in range(nj)]
    _pipelined_units(units, matmul, epilogue)


def _outproj(a, w, res, gate, nw, shift, scale, *, tn=512):
    b, l, k = a.shape
    n = w.shape[1]
    tm = min(ROW_TILE, l)
    assert tm % min(MM_SUB, tm) == 0 and n % tn == 0
    vec = lambda arr: pl.BlockSpec((1, 1, n), _bsel(arr))
    return pl.pallas_call(
        functools.partial(_outproj_kernel, tn=tn),
        grid=(b, l // tm),
        in_specs=[
            pl.BlockSpec((1, tm, k), lambda b_, i: (b_, i, 0)),
            pl.BlockSpec((k, n), lambda b_, i: (0, 0), pipeline_mode=pl.Buffered(1)),
            pl.BlockSpec((1, tm, n), lambda b_, i: (b_, i, 0)),
            vec(gate),
            pl.BlockSpec((1, n), lambda b_, i: (0, 0)),
            vec(shift), vec(scale),
        ],
        out_specs=[pl.BlockSpec((1, tm, n), lambda b_, i: (b_, i, 0))] * 2,
        out_shape=[jax.ShapeDtypeStruct((b, l, n), F32), jax.ShapeDtypeStruct((b, l, n), BF16)],
        compiler_params=_cparams(("parallel", "parallel"), [k in (3, 5, 6) for k in range(7)]),
        name="outproj",
    )(a, w, res, gate, nw.reshape(1, n), shift, scale)


FFN_TILE = 256
FFN_HALO = BF16_SUBLANES
FFN_SUB = 1024
FFN_ROW_TILE = 1024


def _ffn_kernel(x_ref, hm_ref, hp_ref, hn_ref, g_ref, wu_ref, cw_ref, cb_ref, wd_ref, fw_ref,
                o_ref, h_ref, ua_ref, ug_ref, act_ref, *, final_norm):
    i = pl.program_id(1)
    tm = x_ref.shape[1]
    hl = FFN_HALO
    tf = FFN_TILE
    f = wd_ref.shape[0]
    sub = min(FFN_SUB, tm)
    nsub = tm // sub
    chunks = [(hl * (c > 0) + c * sub, hl + (c + 1) * sub + hl * (c == nsub - 1)) for c in range(nsub)]

    def stage(c):
        if c == 0:
            h_ref[0:hl] = jnp.where(i > 0, hp_ref[0], jnp.zeros_like(hp_ref[0]))
        if c == nsub - 1:
            h_ref[hl + tm:2 * hl + tm] = jnp.where(i < pl.num_programs(1) - 1, hn_ref[0], jnp.zeros_like(hn_ref[0]))
        h_ref[hl + c * sub:hl + (c + 1) * sub] = _interleave_rows(hm_ref[0, c * sub:(c + 1) * sub].astype(F32)).astype(BF16)

    def conv(u_ref, slot, col, s):
        base = hl + s * sub
        st = SUBLANES
        cw = cw_ref[:, col:col + tf]
        blk = u_ref[slot, base - st:base + sub + st]
        sublane = lax.broadcasted_iota(jnp.int32, (st, tf), 0)
        first_prev = jnp.where(sublane == 0, pltpu.roll(blk[0:st], 1, 0), pltpu.roll(blk[sub:sub + st], 1, 0))
        last_next = jnp.where(sublane == st - 1, pltpu.roll(blk[sub + st:sub + 2 * st], st - 1, 0),
                              pltpu.roll(blk[st:2 * st], st - 1, 0))
        prev = jnp.concatenate([first_prev, blk[st:sub]], axis=0)
        nxt = jnp.concatenate([blk[2 * st:sub + st], last_next], axis=0)
        return (prev * cw[0:1] + blk[st:sub + st] * cw[1:2] + nxt * cw[2:3]) + cb_ref[:, col:col + tf]

    def up(t, c):
        lo, hi = chunks[c]
        if t == 0:
            stage(c)
        ua_ref[t % 2, lo:hi] = jnp.dot(h_ref[lo:hi], wu_ref[:, t * tf:(t + 1) * tf], preferred_element_type=F32)
        ug_ref[t % 2, lo:hi] = jnp.dot(h_ref[lo:hi], wu_ref[:, f + t * tf:f + (t + 1) * tf], preferred_element_type=F32)

    units = [(t, c) for t in range(f // tf) for c in range(nsub)]
    up(*units[0])
    for n, (t, s) in enumerate(units):
        if n + 1 < len(units):
            up(*units[n + 1])
        act = _silu(conv(ua_ref, t % 2, t * tf, s)) * conv(ug_ref, t % 2, f + t * tf, s)
        act_ref[s * sub:(s + 1) * sub, t * tf:(t + 1) * tf] = act.astype(BF16)

    for s in range(nsub):
        rows = slice(s * sub, (s + 1) * sub)
        down = _deinterleave_rows(jnp.dot(act_ref[rows], wd_ref[...], preferred_element_type=F32))
        y = x_ref[0, rows] + g_ref[0] * down
        if final_norm:
            y = (y * lax.rsqrt(jnp.mean(y * y, axis=-1, keepdims=True) + NORM_EPS)) * fw_ref[...]
        o_ref[0, rows] = y


def _ffn(x, h, gate, wu, cw, cb, wd, final_w, *, layer, final_norm):
    b, l, d = x.shape
    f = wd.shape[1]
    tf = FFN_TILE
    assert f % tf == 0, wd.shape
    tm = min(FFN_ROW_TILE, l)
    hl = FFN_HALO
    per = tm // hl
    nblk = l // hl
    whole = lambda arr: pl.BlockSpec((None,) + arr.shape[1:], lambda b_, i: (layer, 0, 0), pipeline_mode=pl.Buffered(1))
    return pl.pallas_call(
        functools.partial(_ffn_kernel, final_norm=final_norm),
        grid=(b, l // tm),
        in_specs=[
            pl.BlockSpec((1, tm, d), lambda b_, i: (b_, i, 0)),
            pl.BlockSpec((1, tm, d), lambda b_, i: (b_, i, 0)),
            pl.BlockSpec((1, hl, d), lambda b_, i: (b_, jnp.maximum(i * per - 1, 0), 0)),
            pl.BlockSpec((1, hl, d), lambda b_, i: (b_, jnp.minimum((i + 1) * per, nblk - 1), 0)),
            pl.BlockSpec((1, 1, d), _bsel(gate)),
            whole(wu), whole(cw), whole(cb), whole(wd),
            pl.BlockSpec((1, d), lambda b_, i: (0, 0)),
        ],
        out_specs=pl.BlockSpec((1, tm, d), lambda b_, i: (b_, i, 0)),
        out_shape=jax.ShapeDtypeStruct((b, l, d), F32),
        scratch_shapes=[pltpu.VMEM((tm + 2 * hl, d), BF16), pltpu.VMEM((2, tm + 2 * hl, tf), F32),
                        pltpu.VMEM((2, tm + 2 * hl, tf), F32), pltpu.VMEM((tm, f), BF16)],
        compiler_params=_cparams(("parallel", "parallel"), [k == 4 for k in range(10)]),
        name="ffn",
    )(x, h, h, h, gate, wu, cw, cb, wd, final_w.reshape(1, d))


def _ret_kernel(lg_ref, ql_ref, kl_ref, vl_ref, gl_ref, qc_ref, kc_ref, vc_ref, gc_ref, ol_ref, oc_ref, sb_ref):
    c = RET_CHUNK
    h = pl.program_id(1)
    lgf = lg_ref[0, h]
    lgb = lg_ref[1, h]
    n_ctx = qc_ref.shape[1] // c
    n_lat = ql_ref.shape[1] // c

    pos = lax.broadcasted_iota(jnp.int32, (c, 1), 0).astype(F32)
    kdec_f = jnp.exp(lgf * (c - 1.0 - pos))
    kdec_b = jnp.exp(lgb * pos)
    qdec_f = jnp.exp(lgf * (pos + 1.0))
    qdec_b = jnp.exp(lgb * (c - pos))
    one = jnp.ones((1, 1), F32)
    cdec_f = jnp.exp(lgf * c * one)
    cdec_b = jnp.exp(lgb * c * one)
    diff = (lax.broadcasted_iota(jnp.int32, (c, c), 0) - lax.broadcasted_iota(jnp.int32, (c, c), 1)).astype(F32)
    dmat = jnp.exp(jnp.where(diff >= 0, lgf, lgb) * jnp.abs(diff))

    seq = [(qc_ref, kc_ref, vc_ref, gc_ref, oc_ref, slice(t * c, (t + 1) * c)) for t in range(n_ctx)]
    seq += [(ql_ref, kl_ref, vl_ref, gl_ref, ol_ref, slice(t * c, (t + 1) * c)) for t in range(n_lat)]
    n = len(seq)

    def kv(idx, kdec):
        _, k_ref, v_ref, _, _, sl = seq[idx]
        kd = (k_ref[0, sl, :].astype(F32) * kdec).astype(BF16)
        return lax.dot_general(kd, v_ref[0, sl, :], (((0,), (0,)), ((), ())), preferred_element_type=F32)

    def qk(idx):
        q_ref, k_ref, _, _, _, sl = seq[idx]
        return lax.dot_general(q_ref[0, sl, :], k_ref[0, sl, :], (((1,), (1,)), ((), ())), preferred_element_type=F32)

    bwd_order = list(range(n_ctx - 1, -1, -1)) + list(range(n - 1, n_ctx - 1, -1))
    s = None
    for pos_, idx in enumerate(bwd_order):
        if s is not None:
            sb_ref[idx] = s.astype(BF16)
        if pos_ + 1 < n:
            upd = kv(idx, kdec_b)
            s = upd if s is None else s * cdec_b + upd

    s = None
    scores = qk(0)
    for idx in range(n):
        q_ref, _, v_ref, g_ref, o_ref, sl = seq[idx]
        qf = q_ref[0, sl, :].astype(F32)
        cross = None
        if idx != bwd_order[0]:
            cross = jnp.dot((qf * qdec_b).astype(BF16), sb_ref[idx], preferred_element_type=F32)
        if s is not None:
            cf = jnp.dot((qf * qdec_f).astype(BF16), s.astype(BF16), preferred_element_type=F32)
            cross = cf if cross is None else cross + cf
        upd = kv(idx, kdec_f) if idx + 1 < n else None
        nxt = qk(idx + 1) if idx + 1 < n else None
        o = jnp.dot((scores * dmat).astype(BF16), v_ref[0, sl, :], preferred_element_type=F32)
        if cross is not None:
            o = o + cross
        if upd is not None:
            s = upd if s is None else s * cdec_f + upd
        o = o * lax.rsqrt(jnp.mean(o * o, axis=-1, keepdims=True) + NORM_EPS)
        o_ref[0, sl, :] = (g_ref[0, sl, :].astype(F32) * o).astype(BF16)
        scores = nxt


def _retention(lg, pl_, pc):
    b, l, n = pl_.shape
    lc = pc.shape[1]
    hh = RET_HEADS
    dk = n // (6 * hh)
    dv = 2 * dk
    c = RET_CHUNK

    def specs(length):
        return [
            pl.BlockSpec((1, length, dk), lambda b_, h: (b_, 0, h)),
            pl.BlockSpec((1, length, dk), lambda b_, h: (b_, 0, hh + h)),
            pl.BlockSpec((1, length, dv), lambda b_, h: (b_, 0, hh + h)),
            pl.BlockSpec((1, length, dv), lambda b_, h: (b_, 0, 2 * hh + h)),
        ]

    return pl.pallas_call(
        _ret_kernel,
        grid=(b, hh),
        in_specs=[pl.BlockSpec(memory_space=pltpu.SMEM)] + specs(l) + specs(lc),
        out_specs=[
            pl.BlockSpec((1, l, dv), lambda b_, h: (b_, 0, h)),
            pl.BlockSpec((1, lc, dv), lambda b_, h: (b_, 0, h)),
        ],
        out_shape=[jax.ShapeDtypeStruct((b, l, hh * dv), BF16), jax.ShapeDtypeStruct((b, lc, hh * dv), BF16)],
        scratch_shapes=[pltpu.VMEM((lc // c + l // c, dk, dv), BF16)],
        compiler_params=_cparams(("parallel", "arbitrary")),
        name="retention",
    )(lg, pl_, pl_, pl_, pl_, pc, pc, pc, pc)


def _rope_tables(seq, dk):
    quarter = dk // 4
    t = np.arange(seq)
    inv = ROPE_BASE ** (-np.arange(quarter, dtype=np.float64) / quarter)
    ang_r = (t // GRID_W)[:, None] * inv[None, :]
    ang_c = (t % GRID_W)[:, None] * inv[None, :]
    cos = np.concatenate([np.cos(ang_r)] * 2 + [np.cos(ang_c)] * 2, axis=-1)
    sin = np.concatenate([-np.sin(ang_r), np.sin(ang_r), -np.sin(ang_c), np.sin(ang_c)], axis=-1)
    return jnp.asarray(cos, F32), jnp.asarray(sin, F32)


def _na_window(rows):
    kh = min(NA_MAX_KH, rows)
    return kh, NA_QROWS + kh


def _na_window_start(blk, rows):
    kh, wrows = _na_window(rows)
    return min(max(blk * NA_QROWS - kh // 2, 0), rows - wrows)


def _na_kernel(m_ref, q_ref, k_ref, v_ref, kc_ref, vc_ref, o_ref, bias_ref, sl_ref, sc_ref, pl_ref, pc_ref, *, rows):
    b = pl.program_id(1)
    kh, wrows = _na_window(rows)
    qr_n = NA_QROWS
    w = GRID_W
    hd = q_ref.shape[2] // 2
    nblk = rows // qr_n

    @pl.when(b == 0)
    def _():
        qc = lax.broadcasted_iota(jnp.int32, (w, LANES), 0)
        kc = lax.broadcasted_iota(jnp.int32, (w, LANES), 1)
        cs = jnp.clip(qc - NA_KW // 2, 0, w - NA_KW)
        col_valid = (kc >= cs) & (kc < cs + NA_KW)
        toe = [[jnp.where(col_valid,
                          pltpu.roll(jnp.broadcast_to(m_ref[e, dr:dr + 1, :], (w, LANES)), 0, 1, stride=1, stride_axis=0),
                          MASK_VALUE)[:, :w]
                for dr in range(2 * kh - 1)] for e in range(2)]
        for variant in range(3):
            r0 = (0, qr_n, rows - qr_n)[variant]
            ws = _na_window_start(r0 // qr_n, rows)
            for qr in range(qr_n):
                r = r0 + qr
                rs = min(max(r - kh // 2, 0), rows - kh)
                for krr in range(wrows):
                    kr = ws + krr
                    for e in range(2):
                        dst = (e, variant, slice(qr * w, (qr + 1) * w), slice(krr * w, (krr + 1) * w))
                        if rs <= kr < rs + kh:
                            bias_ref[dst] = toe[e][kr - r + kh - 1]
                        else:
                            bias_ref[dst] = jnp.full((w, w), MASK_VALUE, F32)

    vc = vc_ref[0]
    nq = qr_n * w
    lane = lax.broadcasted_iota(jnp.int32, (nq, q_ref.shape[2]), 1)

    kc = kc_ref[0]
    trans_b = (((1,), (1,)), ((), ()))

    chains = []
    for k in range(nblk):
        variant = 0 if k == 0 else (2 if k == nblk - 1 else 1)
        ws = _na_window_start(k, rows)
        win = slice(ws * w, (ws + wrows) * w)
        q = q_ref[0, k * nq:(k + 1) * nq, :]
        for e in range(2):
            qe = jnp.where((lane >= e * hd) & (lane < (e + 1) * hd), q, jnp.zeros_like(q))
            ci = len(chains)
            sl_ref[ci] = lax.dot_general(qe, k_ref[0, win, :], trans_b, preferred_element_type=F32) + bias_ref[e, variant]
            sc_ref[ci] = lax.dot_general(qe, kc, trans_b, preferred_element_type=F32)
            chains.append((k, e, win, ci))

    def head_values(v, e):
        vlane = lax.broadcasted_iota(jnp.int32, v.shape, 1)
        return jnp.where((vlane >= e * hd) & (vlane < (e + 1) * hd), v, jnp.ones_like(v))

    vc_e = [head_values(vc, e) for e in range(2)]
    outs = {}
    for k, e, win, ci in chains:
        m = jnp.maximum(jnp.max(sl_ref[ci], axis=-1, keepdims=True), jnp.max(sc_ref[ci], axis=-1, keepdims=True))
        pl_ref[ci] = jnp.exp2(sl_ref[ci] - m).astype(BF16)
        pc_ref[ci] = jnp.exp2(sc_ref[ci] - m).astype(BF16)
        o = jnp.dot(pl_ref[ci], head_values(v_ref[0, win, :], e), preferred_element_type=F32)
        o += jnp.dot(pc_ref[ci], vc_e[e], preferred_element_type=F32)
        outs[k, e] = o / pltpu.roll(o, hd, 1)
        if e == 1:
            o_ref[0, k * nq:(k + 1) * nq, :] = jnp.where(lane < hd, outs[k, 0], outs[k, 1]).astype(o_ref.dtype)


def _na_bias_table(rpb):
    padded = jnp.pad(rpb * LOG2E, ((0, 0), (0, 0), (0, LANES - rpb.shape[2])))
    return jnp.roll(padded, -(NA_KW - 1), axis=2)


def _na_attention(qkv, kvc, rpb):
    b, s, d3 = qkv.shape
    d = d3 // 3
    lc = kvc.shape[1]
    rows = s // GRID_W
    pairs = d // LANES
    kh, wrows = _na_window(rows)
    qrows = NA_QROWS * GRID_W
    nchain = 2 * (rows // NA_QROWS)
    assert kh == 2 * NA_QROWS and rows % NA_QROWS == 0 and rows >= wrows + NA_QROWS, rows
    mtab = _na_bias_table(rpb)
    return pl.pallas_call(
        functools.partial(_na_kernel, rows=rows),
        grid=(pairs, b),
        in_specs=[
            pl.BlockSpec((2,) + mtab.shape[1:], lambda p, b_: (p, 0, 0)),
            pl.BlockSpec((1, s, LANES), lambda p, b_: (b_, 0, p)),
            pl.BlockSpec((1, s, LANES), lambda p, b_: (b_, 0, pairs + p)),
            pl.BlockSpec((1, s, LANES), lambda p, b_: (b_, 0, 2 * pairs + p)),
            pl.BlockSpec((1, lc, LANES), lambda p, b_: (b_, 0, p)),
            pl.BlockSpec((1, lc, LANES), lambda p, b_: (b_, 0, pairs + p)),
        ],
        out_specs=pl.BlockSpec((1, s, LANES), lambda p, b_: (b_, 0, p)),
        out_shape=jax.ShapeDtypeStruct((b, s, d), BF16),
        scratch_shapes=[pltpu.VMEM((2, 3, qrows, wrows * GRID_W), F32),
                        pltpu.VMEM((nchain, qrows, wrows * GRID_W), F32), pltpu.VMEM((nchain, qrows, lc), F32),
                        pltpu.VMEM((nchain, qrows, wrows * GRID_W), BF16), pltpu.VMEM((nchain, qrows, lc), BF16)],
        compiler_params=_cparams(("arbitrary", "arbitrary")),
        name="na_attention",
    )(mtab, qkv, qkv, qkv, kvc, kvc)


def kernel(x, c, ctx, c_ctx, ada_w, ada_b, norm1_w, norm2_w, ret_w_in, ret_decay_fwd, ret_decay_bwd, ret_w_out,
           na_w_qkv, na_rpb, na_w_out, ffn_w_up, ffn_conv_w, ffn_conv_b, ffn_w_down, final_norm_w):
    b, seq, d = x.shape
    depth = ada_w.shape[0]
    n_mixers = 2

    pad = BF16_SUBLANES - (b + 1) % BF16_SUBLANES
    cc = jnp.concatenate([c, c_ctx[None, :], jnp.zeros((pad, d), F32)], axis=0)
    mods, ret_w_in_bf16 = _ada(cc, ada_w, ada_b, ret_w_in)
    mods = mods.reshape(depth, cc.shape[0], 6, d)

    def mod(i, k):
        return mods[i, :b, k][:, None, :], mods[i, b:b + 1, k][:, None, :]

    bf16_w = None
    for i in range(depth):
        last = i == depth - 1
        j = i // n_mixers
        (sh1, csh1), (sc1, csc1), (g1, cg1) = mod(i, 0), mod(i, 1), mod(i, 2)
        (sh2, csh2), (sc2, csc2), (g2, cg2) = mod(i, 3), mod(i, 4), mod(i, 5)
        if i % n_mixers == 0:
            hk = ret_w_in.shape[2] // 6
            dk = hk // RET_HEADS
            w_in = ret_w_in_bf16[j]
            kw = dict(scale_cols=(hk, 2 * hk), scale_val=dk ** -0.5, silu_cols=(4 * hk, 6 * hk))
            rope = dict(rope=_rope_tables(seq, dk), rope_cols=2 * hk)
            if bf16_w is None:
                pl_, bf16_w = _proj(x, norm1_w[i], sh1, sc1, w_in, **rope, **kw,
                                    cast=(ffn_w_up, ffn_w_down, ret_w_out, na_w_qkv, na_w_out))
                ffn_w = (bf16_w[0], ffn_conv_w, ffn_conv_b[:, None, :], bf16_w[1])
            else:
                pl_ = _proj(x, norm1_w[i], sh1, sc1, w_in, **rope, **kw)
            pc = _proj(ctx, norm1_w[i], csh1, csc1, w_in, **kw)
            lg = jnp.stack([-jnp.exp(ret_decay_fwd[j].astype(F32)), -jnp.exp(ret_decay_bwd[j].astype(F32))])
            y, yc = _retention(lg, pl_, pc)
            w_out = bf16_w[2][j]
        else:
            w_qkv = bf16_w[3][j]
            qkv = _proj(x, norm1_w[i], sh1, sc1, w_qkv, scale_cols=(0, d), scale_val=(d // NA_HEADS) ** -0.5 * LOG2E)
            kvc = _proj(ctx, norm1_w[i], csh1, csc1, w_qkv[:, d:])
            if not last:
                raise NotImplementedError("context output of the attention mixer is only needed for depth > 2")
            y, yc = _na_attention(qkv, kvc, na_rpb[j]), None
            w_out = bf16_w[4][j]
        x, h2 = _outproj(y, w_out, x, g1, norm2_w[i], sh2, sc2)
        x = _ffn(x, h2, g2, *ffn_w, final_norm_w, layer=i, final_norm=last)
        if not last:
            ctx, hc2 = _outproj(yc, w_out, ctx, cg1, norm2_w[i], csh2, csc2)
            ctx = _ffn(ctx, hc2, cg2, *ffn_w, final_norm_w, layer=i, final_norm=False)
    return x
```
